```python
import jax, jax.numpy as jnp
from jax import lax
import numpy as np

D_MODEL = 1024
BATCH = 8
SEQ = 2048
DEPTH = 2
DEC_BATCH = 32
DEC_SEQ = 1
PAST_LEN = 8192
PAGE_SIZE = 128

HEAD_DIM = 64
D_MIX = D_MODEL
H_FOX = 6
H_NSA = 6
KV_NSA = 2
G_NSA = H_NSA // KV_NSA
POOL_WINDOWS = (2, 4, 8, 16)
N_POOL_GROUPS = len(POOL_WINDOWS)
C_POOL = D_MIX - (H_FOX + H_NSA) * HEAD_DIM
POOL_GW = C_POOL // N_POOL_GROUPS
POOL_STATE = max(POOL_WINDOWS) - 1
ROT_DIM = HEAD_DIM // 4
ROPE_THETA = 500000.0
Q_BLOCK = 128
CMP_BLOCK = 64
SEL_BLOCK = CMP_BLOCK
TOPK_BLOCKS = 8
WINDOW = 512
D_FF = 4 * D_MODEL
EPS = 1e-6
NEG = -1e30
FORCE = 1e4
SCALE = HEAD_DIM ** -0.5
_IN_SIZES = (H_FOX * HEAD_DIM, H_FOX * HEAD_DIM, H_FOX * HEAD_DIM, H_FOX,
             H_NSA * HEAD_DIM,
             KV_NSA * HEAD_DIM, KV_NSA * HEAD_DIM, KV_NSA * HEAD_DIM,
             KV_NSA * HEAD_DIM, KV_NSA * HEAD_DIM, KV_NSA * HEAD_DIM,
             3 * H_NSA,
             C_POOL)
N_IN = sum(_IN_SIZES)

kernel_name = 'hybrid_fox_nsa_pool_decoder'

f32 = jnp.float32


def _rms(x, g):
    xf = x.astype(f32)
    y = xf * lax.rsqrt(jnp.mean(jnp.square(xf), axis=-1, keepdims=True) + EPS)
    return (y * g.astype(f32)).astype(x.dtype)


def _masked_softmax(s, mask):
    s = jnp.where(mask, s.astype(f32), NEG)
    p = jax.nn.softmax(s, axis=-1)
    return jnp.where(mask, p, 0.0)


def _rope(x, pos):
    half = ROT_DIM // 2
    inv = ROPE_THETA ** (-jnp.arange(0, ROT_DIM, 2, dtype=f32) / ROT_DIM)
    ang = pos.astype(f32)[:, None] * inv[None, :]
    cos = jnp.cos(ang)[:, None, :].astype(x.dtype)
    sin = jnp.sin(ang)[:, None, :].astype(x.dtype)
    x1, x2 = x[..., :half], x[..., half:ROT_DIM]
    return jnp.concatenate([x1 * cos - x2 * sin, x2 * cos + x1 * sin, x[..., ROT_DIM:]], axis=-1)


def _split_in(z):
    cuts = np.cumsum(_IN_SIZES)[:-1].tolist()
    return jnp.split(z, cuts, axis=-1)


def _heads(t, n):
    return t.reshape(t.shape[0], t.shape[1], n, HEAD_DIM)


def _fox_attend(q, cq, qpos, k, v, ck):
    tk = k.shape[1]
    s = jnp.einsum('bqhd,bkhd->bhqk', q, k).astype(f32) * SCALE
    s = s + (jnp.transpose(cq, (0, 2, 1))[..., :, None] - jnp.transpose(ck, (0, 2, 1))[..., None, :])
    mask = jnp.arange(tk)[None, :] <= qpos[:, None]
    p = _masked_softmax(s, mask)
    return jnp.einsum('bhqk,bkhd->bqhd', p.astype(v.dtype), v)


def _fox_prompt(q, k, v, logf):
    b, t = q.shape[:2]
    nb = t // Q_BLOCK
    c = jnp.cumsum(logf.astype(f32), axis=1)
    qb = jnp.moveaxis(q.reshape(b, nb, Q_BLOCK, H_FOX, HEAD_DIM), 1, 0)
    cqb = jnp.moveaxis(c.reshape(b, nb, Q_BLOCK, H_FOX), 1, 0)
    pos = jnp.arange(t).reshape(nb, Q_BLOCK)
    o = lax.map(lambda a: _fox_attend(a[0], a[1], a[2], k, v, c), (qb, cqb, pos))
    return jnp.moveaxis(o, 0, 1).reshape(b, t, H_FOX, HEAD_DIM)


def _cmp_branch(q5, qpos, kc_rows, vc_rows):
    b, l = kc_rows.shape[:2]
    nbc = l // CMP_BLOCK
    def blocks(r):
        rb = r[:, :nbc * CMP_BLOCK].reshape(b, nbc, CMP_BLOCK, KV_NSA, HEAD_DIM)
        return jnp.mean(rb.astype(f32), axis=2).astype(r.dtype)
    kc, vc = blocks(kc_rows), blocks(vc_rows)
    s = jnp.einsum('btkgd,bnkd->btkgn', q5, kc) * SCALE
    blk_end = (jnp.arange(nbc) + 1) * CMP_BLOCK - 1
    mask = (blk_end[None, :] <= qpos[:, None])[None, :, None, None, :]
    p = _masked_softmax(s, mask)
    o = jnp.einsum('btkgn,bnkd->btkgd', p.astype(vc.dtype), vc)
    return o, jnp.sum(p, axis=3)


def _select_blocks(imp, qpos, nbs):
    nbc = imp.shape[-1]
    imp = jnp.pad(imp, ((0, 0), (0, 0), (0, 0), (0, nbs - nbc)))
    j = jnp.arange(nbs)[None, :]
    cur = (qpos // SEL_BLOCK)[:, None]
    forced = (j == 0) | (j == cur) | (j == cur - 1)
    valid = j * SEL_BLOCK <= qpos[:, None]
    score = jnp.where(forced[None, :, None, :], FORCE, imp)
    score = jnp.where(valid[None, :, None, :], score, -1.0)
    _, idx = lax.top_k(score, min(TOPK_BLOCKS, nbs))
    return idx


def _flat_blocks(ks, vs, kpos):
    b, tq, kv, nk, sb, hd = ks.shape
    return (ks.reshape(b, tq, kv, nk * sb, hd), vs.reshape(b, tq, kv, nk * sb, hd),
            kpos.reshape(b, tq, kv, nk * sb))


def _gather_blocks(kb, vb, idx):
    bi = jnp.arange(idx.shape[0])[:, None, None, None]
    hk = jnp.arange(KV_NSA)[None, None, :, None]
    kpos = idx[..., None] * SEL_BLOCK + jnp.arange(SEL_BLOCK)
    return _flat_blocks(kb[bi, idx, :, hk], vb[bi, idx, :, hk], kpos)


def _gather_paged_blocks(cache_nsa_kv, l, page_table, k_new, v_new, idx, past):
    b = idx.shape[0]
    n_past_blk = past // SEL_BLOCK
    per_page = PAGE_SIZE // SEL_BLOCK
    bi = jnp.arange(b)[:, None, None, None]
    hk = jnp.arange(KV_NSA)[None, None, :, None]
    off = jnp.arange(SEL_BLOCK)
    jp = jnp.minimum(idx, n_past_blk - 1)
    phys = page_table[bi, jp // per_page][..., None]
    within = (jp % per_page)[..., None] * SEL_BLOCK + off
    k_old = cache_nsa_kv[l, phys, within, 2, hk[..., None]]
    v_old = cache_nsa_kv[l, phys, within, 3, hk[..., None]]
    tn = k_new.shape[1]
    n_new_blk = -(-tn // SEL_BLOCK)
    padw = ((0, 0), (0, n_new_blk * SEL_BLOCK - tn), (0, 0), (0, 0))
    kn = jnp.pad(k_new, padw).reshape(b, n_new_blk, SEL_BLOCK, KV_NSA, HEAD_DIM)
    vn = jnp.pad(v_new, padw).reshape(b, n_new_blk, SEL_BLOCK, KV_NSA, HEAD_DIM)
    jn = jnp.clip(idx - n_past_blk, 0, n_new_blk - 1)
    is_old = (idx < n_past_blk)[..., None, None]
    ks = jnp.where(is_old, k_old, kn[bi, jn, :, hk])
    vs = jnp.where(is_old, v_old, vn[bi, jn, :, hk])
    kpos = idx[..., None] * SEL_BLOCK + off
    return _flat_blocks(ks, vs, kpos)


def _sel_attend(q5, qpos, ks, vs, kpos):
    s = jnp.einsum('btkgd,btknd->btkgn', q5, ks) * SCALE
    mask = (kpos <= qpos[None, :, None, None])[:, :, :, None, :]
    p = _masked_softmax(s, mask)
    return jnp.einsum('btkgn,btknd->btkgd', p.astype(vs.dtype), vs)


def _win_attend(q5, qpos, kw, vw, kpos):
    s = jnp.einsum('btkgd,bnkd->btkgn', q5, kw) * SCALE
    rel = qpos[:, None] - kpos[None, :]
    mask = (rel >= 0) & (rel < WINDOW) & (kpos[None, :] >= 0)
    p = _masked_softmax(s, mask[None, :, None, None, :])
    return jnp.einsum('btkgn,bnkd->btkgd', p.astype(vw.dtype), vw)


def _nsa_combine(o_cmp, o_sel, o_win, gate_logits):
    b, t = gate_logits.shape[:2]
    g = jax.nn.sigmoid(gate_logits.astype(f32)).reshape(b, t, KV_NSA, G_NSA, 3)
    o = (g[..., 0:1] * o_cmp.astype(f32) + g[..., 1:2] * o_sel.astype(f32)
         + g[..., 2:3] * o_win.astype(f32))
    return o.reshape(b, t, H_NSA * HEAD_DIM).astype(o_cmp.dtype)


def _nsa_prompt(q, kc, vc, ks, vs, kw, vw, gates):
    b, t = q.shape[:2]
    pos = jnp.arange(t)
    q5 = q.reshape(b, t, KV_NSA, G_NSA, HEAD_DIM)
    o_cmp, imp = _cmp_branch(q5, pos, kc, vc)
    nbs = t // SEL_BLOCK
    idx = _select_blocks(imp, pos, nbs)
    ksb = ks.reshape(b, nbs, SEL_BLOCK, KV_NSA, HEAD_DIM)
    vsb = vs.reshape(b, nbs, SEL_BLOCK, KV_NSA, HEAD_DIM)
    padw = ((0, 0), (WINDOW, 0), (0, 0), (0, 0))
    kwp, vwp = jnp.pad(kw, padw), jnp.pad(vw, padw)

    def one_block(s0):
        qb = lax.dynamic_slice_in_dim(q5, s0, Q_BLOCK, axis=1)
        qp = s0 + jnp.arange(Q_BLOCK)
        ib = lax.dynamic_slice_in_dim(idx, s0, Q_BLOCK, axis=1)
        ksg, vsg, kp = _gather_blocks(ksb, vsb, ib)
        o_sel = _sel_attend(qb, qp, ksg, vsg, kp)
        kwb = lax.dynamic_slice_in_dim(kwp, s0, WINDOW + Q_BLOCK, axis=1)
        vwb = lax.dynamic_slice_in_dim(vwp, s0, WINDOW + Q_BLOCK, axis=1)
        wp = s0 - WINDOW + jnp.arange(WINDOW + Q_BLOCK)
        o_win = _win_attend(qb, qp, kwb, vwb, wp)
        return o_sel, o_win

    o_sel, o_win = lax.map(one_block, jnp.arange(t // Q_BLOCK) * Q_BLOCK)
    unblock = lambda o: jnp.moveaxis(o, 0, 1).reshape(b, t, KV_NSA, G_NSA, HEAD_DIM)
    return _nsa_combine(o_cmp, unblock(o_sel), unblock(o_win), gates)


def _pool_mix(u_ext, n_prev, start_pos, w_pool, pool_scale):
    b, l, _ = u_ext.shape
    tn = l - n_prev
    cs = jnp.pad(jnp.cumsum(u_ext.astype(f32), axis=1), ((0, 0), (1, 0), (0, 0)))
    pos = start_pos + jnp.arange(tn)
    outs = []
    for g, w in enumerate(POOL_WINDOWS):
        csg = cs[:, :, g * POOL_GW:(g + 1) * POOL_GW]
        hi = csg[:, n_prev + 1:]
        lo = jnp.pad(csg, ((0, 0), (w, 0), (0, 0)))[:, n_prev + 1:n_prev + 1 + tn]
        cnt = jnp.minimum(w, pos + 1).astype(f32)[None, :, None]
        outs.append((hi - lo) / cnt - u_ext[:, n_prev:, g * POOL_GW:(g + 1) * POOL_GW].astype(f32))
    d = jnp.stack(outs, axis=2).astype(u_ext.dtype)
    y = jnp.einsum('btgc,gcd->btgd', d, w_pool).reshape(b, tn, C_POOL)
    return y * pool_scale


def _mixer_prompt(h, w_in, b_fox_f, w_out, w_pool, pool_scale):
    b, t, _ = h.shape
    fq, fk, fv, ff, nq, nkc, nvc, nks, nvs, nkw, nvw, ng, u = _split_in(h @ w_in)
    fq, fk, fv = _heads(fq, H_FOX), _heads(fk, H_FOX), _heads(fv, H_FOX)
    logf = jax.nn.log_sigmoid(ff.astype(f32) + b_fox_f.astype(f32))
    o_fox = _fox_prompt(fq, fk, fv, logf)
    pos = jnp.arange(t)
    nq = _rope(_heads(nq, H_NSA), pos)
    nkc, nvc = _rope(_heads(nkc, KV_NSA), pos), _heads(nvc, KV_NSA)
    nks, nvs = _rope(_heads(nks, KV_NSA), pos), _heads(nvs, KV_NSA)
    nkw, nvw = _rope(_heads(nkw, KV_NSA), pos), _heads(nvw, KV_NSA)
    o_nsa = _nsa_prompt(nq, nkc, nvc, nks, nvs, nkw, nvw, ng)
    o_pool = _pool_mix(u, 0, 0, w_pool, pool_scale)
    o = jnp.concatenate([o_fox.reshape(b, t, H_FOX * HEAD_DIM), o_nsa, o_pool], axis=-1) @ w_out
    wl = min(WINDOW, t)
    states = (jnp.stack([fk, fv], axis=2), logf, jnp.stack([nkc, nvc, nks, nvs], axis=2),
              jnp.stack([nkw, nvw], axis=2)[:, t - wl:], u[:, t - POOL_STATE:])
    return o, states


def _mixer_sample(h, l, cache_fox_kv, cache_fox_logf, cache_nsa_kv, cache_nsa_win, state_pool, page_table,
                  w_in, b_fox_f, w_out, w_pool, pool_scale):
    b, tn, _ = h.shape
    past = page_table.shape[1] * PAGE_SIZE
    qpos = past + jnp.arange(tn)
    fq, fk, fv, ff, nq, nkc, nvc, nks, nvs, nkw, nvw, ng, u = _split_in(h @ w_in)
    fq, fk, fv = _heads(fq, H_FOX), _heads(fk, H_FOX), _heads(fv, H_FOX)
    logf = jax.nn.log_sigmoid(ff.astype(f32) + b_fox_f.astype(f32))
    k_all = jnp.concatenate([cache_fox_kv[l, page_table, :, 0].reshape(b, past, H_FOX, HEAD_DIM), fk], axis=1)
    v_all = jnp.concatenate([cache_fox_kv[l, page_table, :, 1].reshape(b, past, H_FOX, HEAD_DIM), fv], axis=1)
    lf_all = jnp.concatenate([cache_fox_logf[l, page_table].reshape(b, past, H_FOX).astype(f32), logf], axis=1)
    c_all = jnp.cumsum(lf_all, axis=1)
    o_fox = _fox_attend(fq, c_all[:, past:], qpos, k_all, v_all, c_all)
    nq = _rope(_heads(nq, H_NSA), qpos)
    nkc, nvc = _rope(_heads(nkc, KV_NSA), qpos), _heads(nvc, KV_NSA)
    nks, nvs = _rope(_heads(nks, KV_NSA), qpos), _heads(nvs, KV_NSA)
    nkw, nvw = _rope(_heads(nkw, KV_NSA), qpos), _heads(nvw, KV_NSA)
    q5 = nq.reshape(b, tn, KV_NSA, G_NSA, HEAD_DIM)
    kc_all = jnp.concatenate([cache_nsa_kv[l, page_table, :, 0].reshape(b, past, KV_NSA, HEAD_DIM), nkc], axis=1)
    vc_all = jnp.concatenate([cache_nsa_kv[l, page_table, :, 1].reshape(b, past, KV_NSA, HEAD_DIM), nvc], axis=1)
    o_cmp, imp = _cmp_branch(q5, qpos, kc_all, vc_all)
    idx = _select_blocks(imp, qpos, -(-(past + tn) // SEL_BLOCK))
    ksg, vsg, kp = _gather_paged_blocks(cache_nsa_kv, l, page_table, nks, nvs, idx, past)
    o_sel = _sel_attend(q5, qpos, ksg, vsg, kp)
    win = cache_nsa_win[l]
    wb = win.shape[1]
    kw_all = jnp.concatenate([win[:, :, 0], nkw], axis=1)
    vw_all = jnp.concatenate([win[:, :, 1], nvw], axis=1)
    o_win = _win_attend(q5, qpos, kw_all, vw_all, past - wb + jnp.arange(wb + tn))
    o_nsa = _nsa_combine(o_cmp, o_sel, o_win, ng)
    u_ext = jnp.concatenate([state_pool[l], u], axis=1)
    o_pool = _pool_mix(u_ext, state_pool.shape[2], past, w_pool, pool_scale)
    o = jnp.concatenate([o_fox.reshape(b, tn, H_FOX * HEAD_DIM), o_nsa, o_pool], axis=-1) @ w_out
    states = (jnp.stack([fk, fv], axis=2), logf, jnp.stack([nkc, nvc, nks, nvs], axis=2),
              jnp.stack([kw_all, vw_all], axis=2)[:, tn:], u_ext[:, tn:])
    return o, states


def _sq_relu_mlp(h, w1, w2):
    a = jax.nn.relu(h @ w1)
    return (a * a) @ w2


def _layer(x, c, g, ada_w, ada_b, w1, w2, mixer):
    mod = (jax.nn.silu(c) @ ada_w + ada_b)[:, None, :]
    shift1, scale1, gate1, shift2, scale2, gate2 = jnp.split(mod, 6, axis=-1)
    h = _rms(x, g[0]) * (1.0 + scale1) + shift1
    o, states = mixer(h)
    x = x + gate1 * _rms(o, g[1])
    h = _rms(x, g[2]) * (1.0 + scale2) + shift2
    x = x + gate2 * _rms(_sq_relu_mlp(h, w1, w2), g[3])
    return x, states


def setup_inputs(seed: int = 0) -> dict:
    key = jax.random.key(seed)
    ks = jax.random.split(key, 20)
    n_pages = PAST_LEN // PAGE_SIZE
    n_used = DEC_BATCH * n_pages
    n_pool = n_used + max(1, n_used // 4)
    wbuf = min(WINDOW, PAST_LEN)
    nrm = lambda k, shape, s=1.0: s * jax.random.normal(k, shape, jnp.float32)
    x_prompt = nrm(ks[0], (BATCH, SEQ, D_MODEL))
    x_sample = nrm(ks[1], (DEC_BATCH, DEC_SEQ, D_MODEL))
    cache_fox_kv = nrm(ks[2], (DEPTH, n_pool, PAGE_SIZE, 2, H_FOX, HEAD_DIM))
    cache_fox_logf = jax.nn.log_sigmoid(nrm(ks[3], (DEPTH, n_pool, PAGE_SIZE, H_FOX), 0.5) + 3.0)
    cache_nsa_kv = nrm(ks[4], (DEPTH, n_pool, PAGE_SIZE, 4, KV_NSA, HEAD_DIM))
    cache_nsa_win = nrm(ks[5], (DEPTH, DEC_BATCH, wbuf, 2, KV_NSA, HEAD_DIM))
    state_pool = nrm(ks[6], (DEPTH, DEC_BATCH, POOL_STATE, C_POOL))
    page_table = jax.random.permutation(ks[7], n_pool)[:n_used].reshape(DEC_BATCH, n_pages).astype(jnp.int32)
    c_prompt = nrm(ks[8], (BATCH, D_MODEL))
    c_sample = nrm(ks[9], (DEC_BATCH, D_MODEL))
    w_ada = nrm(ks[10], (DEPTH, D_MODEL, 6 * D_MODEL), 0.5 * D_MODEL ** -0.5)
    b_ada = nrm(ks[11], (DEPTH, 6 * D_MODEL), 0.01)
    norm_g = 1.0 + nrm(ks[12], (DEPTH, 4, D_MODEL), 0.05)
    w_in = nrm(ks[13], (DEPTH, D_MODEL, N_IN), D_MODEL ** -0.5)
    b_fox_f = 3.0 + nrm(ks[14], (DEPTH, H_FOX), 0.5)
    w_out = nrm(ks[15], (DEPTH, D_MIX, D_MODEL), D_MIX ** -0.5)
    w_pool = nrm(ks[16], (DEPTH, N_POOL_GROUPS, POOL_GW, POOL_GW), POOL_GW ** -0.5)
    pool_scale = 1.0 + nrm(ks[17], (DEPTH, C_POOL), 0.1)
    w_ff1 = nrm(ks[18], (DEPTH, D_MODEL, D_FF), D_MODEL ** -0.5)
    w_ff2 = nrm(ks[19], (DEPTH, D_FF, D_MODEL), D_FF ** -0.5)
    return {'x_prompt': x_prompt, 'x_sample': x_sample,
            'cache_fox_kv': cache_fox_kv, 'cache_fox_logf': cache_fox_logf,
            'cache_nsa_kv': cache_nsa_kv, 'cache_nsa_win': cache_nsa_win, 'state_pool': state_pool,
            'page_table': page_table, 'c_prompt': c_prompt, 'c_sample': c_sample,
            'w_ada': w_ada, 'b_ada': b_ada, 'norm_g': norm_g, 'w_in': w_in, 'b_fox_f': b_fox_f,
            'w_out': w_out, 'w_pool': w_pool, 'pool_scale': pool_scale, 'w_ff1': w_ff1, 'w_ff2': w_ff2}


def reference(x_prompt, x_sample, cache_fox_kv, cache_fox_logf, cache_nsa_kv, cache_nsa_win, state_pool,
              page_table, c_prompt, c_sample, w_ada, b_ada, norm_g, w_in, b_fox_f, w_out, w_pool,
              pool_scale, w_ff1, w_ff2):
    yp, ys = x_prompt, x_sample
    sp, ss = [], []
    for l in range(DEPTH):
        mix_p = lambda h, l=l: _mixer_prompt(h, w_in[l], b_fox_f[l], w_out[l], w_pool[l], pool_scale[l])
        yp, st = _layer(yp, c_prompt, norm_g[l], w_ada[l], b_ada[l], w_ff1[l], w_ff2[l], mix_p)
        sp.append(st)
        mix_s = lambda h, l=l: _mixer_sample(h, l, cache_fox_kv, cache_fox_logf, cache_nsa_kv, cache_nsa_win,
                                             state_pool, page_table, w_in[l], b_fox_f[l], w_out[l],
                                             w_pool[l], pool_scale[l])
        ys, st = _layer(ys, c_sample, norm_g[l], w_ada[l], b_ada[l], w_ff1[l], w_ff2[l], mix_s)
        ss.append(st)
    stk = lambda lst, i: jnp.stack([s[i] for s in lst], axis=0)
    return (yp, ys,
            stk(sp, 0), stk(sp, 1), stk(sp, 2), stk(sp, 3), stk(sp, 4),
            stk(ss, 0), stk(ss, 1), stk(ss, 2), stk(ss, 3), stk(ss, 4))
```

```python
import functools

import numpy as np
import jax
import jax.numpy as jnp
from jax import lax
from jax.experimental import pallas as pl
from jax.experimental.pallas import tpu as pltpu

f32 = jnp.float32
bf16 = jnp.bfloat16
i32 = jnp.int32

D_MODEL = 1024
HEAD_DIM = 64
H_FOX = 6
H_NSA = 6
KV_NSA = 2
G_NSA = H_NSA // KV_NSA
POOL_WINDOWS = (2, 4, 8, 16)
C_POOL = 256
POOL_GW = 64
POOL_STATE = 15
ROT_DIM = 16
ROPE_THETA = 500000.0
CMP_BLOCK = 64
SEL_BLOCK = 64
TOPK_BLOCKS = 8
WINDOW = 512
PAGE_SIZE = 128
D_FF = 4 * D_MODEL
EPS = 1e-6
NEG = -1e30
FORCE = 1e4
SCALE = HEAD_DIM ** -0.5
N_IN = 2584

LANES = 128
HALF = LANES // 2
VMEM_LIMIT = 56 * 1024 * 1024

C_FQ = 0
C_NQ = 384
C_FKV = 768
C_NKV = 1536
C_WIN = 2048
C_U = 2304
C_MISC = 2560
N_PROJ = 2688
GATE0 = H_FOX
NSA_PAIR_ORDER = (0, 3, 1, 4, 2, 5)

FOX_TQ = 256
FOX_TK = 256
NSA_TQ = 128
NSA_TK = 256
FOX_PPS = 8
CMP_PPS = 8


def _cp(sem):
    return pltpu.CompilerParams(dimension_semantics=sem, vmem_limit_bytes=VMEM_LIMIT)


def _nt(a, b):
    return lax.dot_general(a, b, (((1,), (1,)), ((), ())), preferred_element_type=f32)


def _rms(x, g):
    return x * lax.rsqrt(jnp.mean(x * x, axis=-1, keepdims=True) + EPS) * g


def _log_sigmoid(x):
    return jnp.minimum(x, 0.0) - jnp.log1p(jnp.exp(-jnp.abs(x)))


def _ada_kernel(c_ref, w_ref, b_ref, o_ref):
    c = c_ref[...]
    a = (c * jax.nn.sigmoid(c)).astype(bf16)
    o_ref[0] = jnp.dot(a, w_ref[0].astype(bf16), preferred_element_type=f32) + b_ref[0]


def _ada(c_all, w_ada, b_ada):
    depth = w_ada.shape[0]
    n = c_all.shape[0]
    tn = 1024
    return pl.pallas_call(
        _ada_kernel,
        grid=(depth, 6 * D_MODEL // tn),
        in_specs=[pl.BlockSpec((n, D_MODEL), lambda l, j: (0, 0)),
                  pl.BlockSpec((1, D_MODEL, tn), lambda l, j: (l, 0, j)),
                  pl.BlockSpec((1, 1, tn), lambda l, j: (l, 0, j))],
        out_specs=pl.BlockSpec((1, n, tn), lambda l, j: (l, 0, j)),
        out_shape=jax.ShapeDtypeStruct((depth, n, 6 * D_MODEL), f32),
        compiler_params=_cp(("parallel", "parallel")),
        name="ada_mod",
    )(c_all, w_ada, b_ada.reshape(depth, 1, 6 * D_MODEL))


def _inproj_kernel(x_ref, g_ref, sc_ref, sh_ref, w_ref, cos_ref, sa_ref, sb_ref,
                   fq_ref, nq_ref, fkv_ref, nkv_ref, win_ref, u_ref, misc_ref,
                   fkvb_ref, nkvb_ref, winb_ref, kcm_ref, *, tm, with_means):
    x = x_ref[...]
    h = (_rms(x, g_ref[...]) * (1.0 + sc_ref[0]) + sh_ref[0]).astype(bf16)
    cos = cos_ref[...]
    sa = sa_ref[...]
    sb = sb_ref[...]

    def rope(z):
        return z * cos + pltpu.roll(z, LANES - ROT_DIM // 2, 1) * sa + pltpu.roll(z, ROT_DIM // 2, 1) * sb

    def means(z):
        return jnp.sum(z.reshape(tm // CMP_BLOCK, CMP_BLOCK, LANES), axis=1) * (1.0 / CMP_BLOCK)

    def emit(blk, z):
        c = blk * LANES
        if c < C_NQ:
            fq_ref[:, c - C_FQ:c - C_FQ + LANES] = (z * SCALE).astype(bf16)
        elif c < C_FKV:
            nq_ref[:, c - C_NQ:c - C_NQ + LANES] = (rope(z) * SCALE).astype(bf16)
        elif c < C_NKV:
            fkv_ref[:, c - C_FKV:c - C_FKV + LANES] = z
            fkvb_ref[:, c - C_FKV:c - C_FKV + LANES] = z.astype(bf16)
        elif c < C_WIN:
            o = c - C_NKV
            if o in (0, 2 * LANES):
                z = rope(z)
            nkv_ref[:, o:o + LANES] = z
            nkvb_ref[:, o:o + LANES] = z.astype(bf16)
            if with_means and o < 2 * LANES:
                kcm_ref[:, o:o + LANES] = means(z)
        elif c < C_U:
            o = c - C_WIN
            if o == 0:
                z = rope(z)
            win_ref[:, o:o + LANES] = z
            winb_ref[:, o:o + LANES] = z.astype(bf16)
        elif c < C_MISC:
            u_ref[:, c - C_U:c - C_U + LANES] = z
        else:
            misc_ref[...] = z

    nblk = N_PROJ // LANES
    for b0 in range(0, nblk, 2):
        nb = min(2, nblk - b0)
        z2 = jnp.dot(h, w_ref[:, b0 * LANES:(b0 + nb) * LANES], preferred_element_type=f32)
        for k in range(nb):
            emit(b0 + k, z2[:, k * LANES:(k + 1) * LANES])
    if not with_means:
        kcm_ref[...] = jnp.zeros(kcm_ref.shape, f32)


def _inproj(x2, g0, sc, sh, w, cos, sa, sb, *, tm, rows_per_mod, rope_rows, with_means):
    m = x2.shape[0]
    nt = m // tm
    mod_rows = sc.shape[1]
    rope_blocks = rope_rows // tm
    kc_rows = max(tm // CMP_BLOCK, 8)
    modmap = lambda i: ((i * tm) // rows_per_mod, 0, 0)
    ropemap = lambda i: (i % rope_blocks, 0)
    row = lambda w_: pl.BlockSpec((tm, w_), lambda i: (i, 0))
    outs = [((m, 384), bf16), ((m, 384), bf16), ((m, 768), f32), ((m, 512), f32), ((m, 256), f32),
            ((m, 256), f32), ((m, LANES), f32), ((m, 768), bf16), ((m, 512), bf16), ((m, 256), bf16)]
    out_shape = [jax.ShapeDtypeStruct(s, d) for s, d in outs] + [jax.ShapeDtypeStruct((nt * kc_rows, 256), f32)]
    out_specs = [row(s[1]) for s, _ in outs] + [pl.BlockSpec((kc_rows, 256), lambda i: (i, 0))]
    return pl.pallas_call(
        functools.partial(_inproj_kernel, tm=tm, with_means=with_means),
        grid=(nt,),
        in_specs=[row(D_MODEL),
                  pl.BlockSpec((1, D_MODEL), lambda i: (0, 0)),
                  pl.BlockSpec((1, mod_rows, D_MODEL), modmap),
                  pl.BlockSpec((1, mod_rows, D_MODEL), modmap),
                  pl.BlockSpec((D_MODEL, N_PROJ), lambda i: (0, 0)),
                  pl.BlockSpec((tm, LANES), ropemap),
                  pl.BlockSpec((tm, LANES), ropemap),
                  pl.BlockSpec((tm, LANES), ropemap)],
        out_specs=out_specs,
        out_shape=out_shape,
        compiler_params=_cp(("parallel",)),
        name="in_proj",
    )(x2, g0, sc, sh, w, cos, sa, sb)


def _foxprep_kernel(misc_ref, bias_ref, lf_ref, ccol_ref, crow_ref, *, t):
    blk = 256
    lane = lax.broadcasted_iota(i32, (t, LANES), 1)
    lf = jnp.where(lane < H_FOX, _log_sigmoid(misc_ref[...] + bias_ref[...]), 0.0)
    lf_ref[...] = lf[:, :H_FOX]
    r = lax.broadcasted_iota(i32, (blk, blk), 0)
    c = lax.broadcasted_iota(i32, (blk, blk), 1)
    ltri = (r >= c).astype(f32)
    carry = jnp.zeros((1, LANES), f32)
    for b in range(t // blk):
        cb = jnp.dot(ltri, lf[b * blk:(b + 1) * blk], preferred_element_type=f32,
                     precision=lax.Precision.HIGHEST) + carry
        ccol_ref[b * blk:(b + 1) * blk, :] = cb
        carry = cb[blk - 1:blk, :]
    er = lax.broadcasted_iota(i32, (8, LANES), 0)
    ec = lax.broadcasted_iota(i32, (8, LANES), 1)
    eye = (er == ec).astype(f32)
    crow_ref[0] = lax.dot_general(eye, ccol_ref[...], (((1,), (1,)), ((), ())),
                                  preferred_element_type=f32, precision=lax.Precision.HIGHEST)


def _foxprep(misc, bias_row, b, t):
    return pl.pallas_call(
        functools.partial(_foxprep_kernel, t=t),
        grid=(b,),
        in_specs=[pl.BlockSpec((t, LANES), lambda i: (i, 0)),
                  pl.BlockSpec((1, LANES), lambda i: (0, 0))],
        out_specs=[pl.BlockSpec((t, H_FOX), lambda i: (i, 0)),
                   pl.BlockSpec((t, LANES), lambda i: (i, 0)),
                   pl.BlockSpec((1, 8, t), lambda i: (i, 0, 0))],
        out_shape=[jax.ShapeDtypeStruct((b * t, H_FOX), f32),
                   jax.ShapeDtypeStruct((b * t, LANES), f32),
                   jax.ShapeDtypeStruct((b, 8, t), f32)],
        compiler_params=_cp(("parallel",)),
        name="fox_prep",
    )(misc, bias_row)


def _softmax_step(carry, s, mask, vb):
    m, l, acc = carry
    s = jnp.where(mask, s, NEG)
    m_new = jnp.maximum(m, jnp.max(s, axis=1, keepdims=True))
    alpha = jnp.exp(m - m_new)
    p = jnp.where(mask, jnp.exp(s - m_new), 0.0)
    l = alpha * l + jnp.sum(p, axis=1, keepdims=True)
    acc = alpha * acc + jnp.dot(p.astype(bf16), vb, preferred_element_type=f32)
    return m_new, l, acc


def _fox_kernel(q_ref, kv_ref, ccol_ref, crow_ref, o_ref, *, tq, tk):
    qi = pl.program_id(1)
    lo = lax.broadcasted_iota(i32, (tq, LANES), 1) < HALF
    rows = lax.broadcasted_iota(i32, (tq, tk), 0) + qi * tq
    cols = lax.broadcasted_iota(i32, (tq, tk), 1)
    zero = jnp.zeros((), bf16)
    for p in range(H_FOX // 2):
        q = q_ref[:, p * LANES:(p + 1) * LANES]
        qst = jnp.concatenate([jnp.where(lo, q, zero), jnp.where(lo, zero, q)], axis=0)
        cq0 = ccol_ref[:, 2 * p:2 * p + 1]
        cq1 = ccol_ref[:, 2 * p + 1:2 * p + 2]

        def body(c, carry, p=p, qst=qst, cq0=cq0, cq1=cq1):
            k0 = pl.multiple_of(c * tk, tk)
            kb = kv_ref[pl.ds(k0, tk), p * LANES:(p + 1) * LANES]
            vb = kv_ref[pl.ds(k0, tk), 384 + p * LANES:384 + (p + 1) * LANES]
            ck0 = crow_ref[0, 2 * p:2 * p + 1, pl.ds(k0, tk)]
            ck1 = crow_ref[0, 2 * p + 1:2 * p + 2, pl.ds(k0, tk)]
            s = _nt(qst, kb) + jnp.concatenate([cq0 - ck0, cq1 - ck1], axis=0)
            mask = (cols + k0) <= rows
            return _softmax_step(carry, s, jnp.concatenate([mask, mask], axis=0), vb)

        init = (jnp.full((2 * tq, 1), NEG, f32), jnp.zeros((2 * tq, 1), f32), jnp.zeros((2 * tq, LANES), f32))
        nchunks = (qi * tq + tq + tk - 1) // tk
        _, l, acc = lax.fori_loop(0, nchunks, body, init)
        o = acc / l
        o_ref[:, p * LANES:(p + 1) * LANES] = jnp.where(lo, o[:tq], o[tq:]).astype(bf16)


def _fox_attn(fq, fkvb, ccol, crow, b, t):
    tq, tk = FOX_TQ, FOX_TK
    nq = t // tq
    return pl.pallas_call(
        functools.partial(_fox_kernel, tq=tq, tk=tk),
        grid=(b, nq),
        in_specs=[pl.BlockSpec((tq, 384), lambda i, j: (i * nq + j, 0)),
                  pl.BlockSpec((t, 768), lambda i, j: (i, 0)),
                  pl.BlockSpec((tq, LANES), lambda i, j: (i * nq + j, 0)),
                  pl.BlockSpec((1, 8, t), lambda i, j: (i, 0, 0))],
        out_specs=pl.BlockSpec((tq, 384), lambda i, j: (i * nq + j, 0)),
        out_shape=jax.ShapeDtypeStruct((b * t, 384), bf16),
        compiler_params=_cp(("parallel", "parallel")),
        name="fox_attn",
    )(fq, fkvb, ccol, crow)


def _select_topk(score, nsel):
    nb = score.shape[1]
    jf = lax.broadcasted_iota(i32, score.shape, 1).astype(f32)
    sel = jnp.zeros(score.shape, f32)
    firsts = []
    for _ in range(nsel):
        mx = jnp.max(score, axis=1, keepdims=True)
        first = jnp.min(jnp.where(score == mx, jf, float(nb)), axis=1, keepdims=True)
        hit = jf == first
        sel = jnp.where(hit, 1.0, sel)
        score = jnp.where(hit, -3.0, score)
        firsts.append(first)
    return sel, firsts


def _nsa_kernel(q_ref, kcm_ref, nkv_ref, win_ref, misc_ref, o_ref, *, tq, tk, t):
    qi = pl.program_id(1)
    s0 = qi * tq
    nbc = t // CMP_BLOCK
    lane = lax.broadcasted_iota(i32, (tq, LANES), 1)
    lo = lane < HALF
    zero = jnp.zeros((), bf16)
    rows = lax.broadcasted_iota(i32, (tq, tk), 0) + s0
    cols = lax.broadcasted_iota(i32, (tq, tk), 1)
    qpos_b = lax.broadcasted_iota(i32, (tq, LANES), 0) + s0
    jb = lax.broadcasted_iota(i32, (tq, LANES), 1)
    misc = misc_ref[...]

    def tile3(a):
        return jnp.concatenate([a, a, a], axis=0)

    def attend(qk, kref, klane, vlane, c_lo, c_hi, maskfn):
        def body(c, carry):
            k0 = pl.multiple_of(c * tk, tk)
            kb = kref[pl.ds(k0, tk), klane:klane + LANES]
            vb = kref[pl.ds(k0, tk), vlane:vlane + LANES]
            return _softmax_step(carry, _nt(qk, kb), tile3(maskfn(k0)), vb)
        init = (jnp.full((3 * tq, 1), NEG, f32), jnp.zeros((3 * tq, 1), f32), jnp.zeros((3 * tq, LANES), f32))
        _, l, acc = lax.fori_loop(c_lo, c_hi, body, init)
        return acc / l

    kcm = jnp.concatenate([kcm_ref[...], jnp.zeros((LANES - nbc, 2 * LANES), f32)], axis=0)
    kc = kcm[:, :LANES].astype(bf16)
    vc = kcm[:, LANES:].astype(bf16)
    outs = []
    for kv in range(KV_NSA):
        keep = lo if kv == 0 else jnp.logical_not(lo)
        qk = jnp.concatenate([jnp.where(keep, q_ref[:, p * LANES:(p + 1) * LANES], zero)
                              for p in range(G_NSA)], axis=0)
        cmask = (((jb + 1) * CMP_BLOCK - 1) <= qpos_b) & (jb < nbc)
        cmask3 = tile3(cmask)
        sc = jnp.where(cmask3, _nt(qk, kc), NEG)
        mx = jnp.max(sc, axis=1, keepdims=True)
        pe = jnp.where(cmask3, jnp.exp(sc - mx), 0.0)
        den = jnp.sum(pe, axis=1, keepdims=True)
        pc = pe / jnp.where(den > 0.0, den, 1.0)
        o_cmp = jnp.dot(pc.astype(bf16), vc, preferred_element_type=f32)
        imp = pc[:tq] + pc[tq:2 * tq] + pc[2 * tq:]
        cur = qpos_b // SEL_BLOCK
        forced = (jb == 0) | (jb == cur) | (jb == cur - 1)
        score = jnp.where(forced, FORCE, imp)
        score = jnp.where(jb * SEL_BLOCK <= qpos_b, score, -1.0)
        score = jnp.where(jb < nbc, score, -2.0)
        sel, _ = _select_topk(score, min(TOPK_BLOCKS, nbc))
        selb = sel.astype(bf16)

        def sel_mask(k0, selb=selb):
            er = lax.broadcasted_iota(i32, (LANES, tk), 0)
            ec = (lax.broadcasted_iota(i32, (LANES, tk), 1) + k0) // SEL_BLOCK
            expand = jnp.where(er == ec, 1.0, 0.0).astype(bf16)
            picked = jnp.dot(selb, expand, preferred_element_type=f32) > 0.5
            return picked & ((cols + k0) <= rows)

        def win_mask(k0):
            rel = rows - (cols + k0)
            return (rel >= 0) & (rel < WINDOW)

        o_sel = attend(qk, nkv_ref, 2 * LANES, 3 * LANES, 0, (s0 + tq + tk - 1) // tk, sel_mask)
        w_lo = jnp.maximum(s0 - WINDOW + 1, 0) // tk
        o_win = attend(qk, win_ref, 0, LANES, w_lo, (s0 + tq + tk - 1) // tk, win_mask)
        for g in range(G_NSA):
            h = kv * G_NSA + g
            gate = jax.nn.sigmoid(misc[:, GATE0 + 3 * h:GATE0 + 3 * h + 3])
            sl = slice(g * tq, (g + 1) * tq)
            outs.append(gate[:, 0:1] * o_cmp[sl] + gate[:, 1:2] * o_sel[sl] + gate[:, 2:3] * o_win[sl])
    for p in range(G_NSA):
        o_ref[:, p * LANES:(p + 1) * LANES] = jnp.where(lo, outs[p], outs[G_NSA + p]).astype(bf16)


def _nsa_attn(nq, kcm, nkvb, winb, misc, b, t):
    tq, tk = NSA_TQ, NSA_TK
    nq_t = t // tq
    nbc = t // CMP_BLOCK
    return pl.pallas_call(
        functools.partial(_nsa_kernel, tq=tq, tk=tk, t=t),
        grid=(b, nq_t),
        in_specs=[pl.BlockSpec((tq, 384), lambda i, j: (i * nq_t + j, 0)),
                  pl.BlockSpec((nbc, 256), lambda i, j: (i, 0)),
                  pl.BlockSpec((t, 512), lambda i, j: (i, 0)),
                  pl.BlockSpec((t, 256), lambda i, j: (i, 0)),
                  pl.BlockSpec((tq, LANES), lambda i, j: (i * nq_t + j, 0))],
        out_specs=pl.BlockSpec((tq, 384), lambda i, j: (i * nq_t + j, 0)),
        out_shape=jax.ShapeDtypeStruct((b * t, 384), bf16),
        compiler_params=_cp(("parallel", "parallel")),
        name="nsa_attn",
    )(nq, kcm, nkvb, winb, misc)


def _pool_project(d, wp_ref, ps_ref):
    return (jnp.dot(d.astype(bf16), wp_ref[...], preferred_element_type=f32) * ps_ref[...]).astype(bf16)


def _out_tail(x, ofox, onsa, opool, wo_ref, g_ref, gate):
    o = (jnp.dot(ofox, wo_ref[0:384, :], preferred_element_type=f32)
         + jnp.dot(onsa, wo_ref[384:768, :], preferred_element_type=f32)
         + jnp.dot(opool, wo_ref[768:1024, :], preferred_element_type=f32))
    return x + gate * _rms(o, g_ref[...])


def _outproj_kernel(x_ref, ofox_ref, onsa_ref, u_ref, halo_ref, wo_ref, wp_ref, ps_ref, g_ref, gate_ref,
                    o_ref, ext_ref, *, tm, t):
    i = pl.program_id(0)
    pos0 = (i * tm) % t
    u = u_ref[...]
    ext_ref[0:16, :] = jnp.where(pos0 > 0, halo_ref[...], 0.0)
    ext_ref[16:, :] = u
    pos = lax.broadcasted_iota(i32, (tm, LANES), 0) + pos0
    lo = lax.broadcasted_iota(i32, (tm, LANES), 1) < HALF

    def shifted(k, c):
        return ext_ref[16 - k:16 - k + tm, c * LANES:(c + 1) * LANES]

    def cnt(w):
        return jnp.minimum(w, pos + 1).astype(f32)

    ds = []
    for c, (wa, wb) in enumerate(((POOL_WINDOWS[0], POOL_WINDOWS[1]), (POOL_WINDOWS[2], POOL_WINDOWS[3]))):
        run = shifted(0, c)
        sums = {}
        for k in range(1, wb):
            run = run + shifted(k, c)
            if k + 1 in (wa, wb):
                sums[k + 1] = run
        ds.append(jnp.where(lo, sums[wa] / cnt(wa), sums[wb] / cnt(wb)) - u[:, c * LANES:(c + 1) * LANES])
    opool = _pool_project(jnp.concatenate(ds, axis=1), wp_ref, ps_ref)
    o_ref[...] = _out_tail(x_ref[...], ofox_ref[...], onsa_ref[...], opool, wo_ref, g_ref, gate_ref[0])


def _outproj(x2, ofox, onsa, u, wo, wp, ps, g1, gate, *, tm, t):
    m = x2.shape[0]
    row = lambda w_: pl.BlockSpec((tm, w_), lambda i: (i, 0))
    const = lambda s: pl.BlockSpec(s, lambda i: (0, 0))
    return pl.pallas_call(
        functools.partial(_outproj_kernel, tm=tm, t=t),
        grid=(m // tm,),
        in_specs=[row(D_MODEL), row(384), row(384), row(256),
                  pl.BlockSpec((16, 256), lambda i: (jnp.maximum(i * (tm // 16) - 1, 0), 0)),
                  const((D_MODEL, D_MODEL)), const((256, 256)), const((1, 256)), const((1, D_MODEL)),
                  pl.BlockSpec((1, 1, D_MODEL), lambda i: ((i * tm) // t, 0, 0))],
        out_specs=row(D_MODEL),
        out_shape=jax.ShapeDtypeStruct((m, D_MODEL), f32),
        scratch_shapes=[pltpu.VMEM((tm + 16, 256), f32)],
        compiler_params=_cp(("parallel",)),
        name="out_proj",
    )(x2, ofox, onsa, u, u, wo, wp, ps, g1, gate)


def _outproj_dec_kernel(x_ref, ofox_ref, onsa_ref, ext_ref, wo_ref, wp_ref, ps_ref, g_ref, gate_ref, o_ref, *, past):
    ext = ext_ref[...]
    n = ext.shape[1]
    r = lax.broadcasted_iota(i32, ext.shape, 1)
    lane = lax.broadcasted_iota(i32, (ext.shape[0], C_POOL), 1)
    u_new = jnp.sum(jnp.where(r == n - 1, ext, 0.0), axis=1)
    d = jnp.zeros(u_new.shape, f32)
    for g, w in enumerate(POOL_WINDOWS):
        sw = jnp.sum(jnp.where(r >= n - w, ext, 0.0), axis=1)
        d = jnp.where(lane // POOL_GW == g, sw / float(min(w, past + 1)), d)
    opool = _pool_project(d - u_new, wp_ref, ps_ref)
    o_ref[...] = _out_tail(x_ref[...], ofox_ref[...], onsa_ref[...], opool, wo_ref, g_ref, gate_ref[...])


def _outproj_dec(x2, ofox, onsa, ext, wo, wp, ps, g1, gate, past):
    m = x2.shape[0]
    full = lambda a: pl.BlockSpec(a.shape, lambda i: (0,) * a.ndim)
    args = (x2, ofox, onsa, ext, wo, wp, ps, g1, gate)
    return pl.pallas_call(
        functools.partial(_outproj_dec_kernel, past=past),
        grid=(1,),
        in_specs=[full(a) for a in args],
        out_specs=pl.BlockSpec((m, D_MODEL), lambda i: (0, 0)),
        out_shape=jax.ShapeDtypeStruct((m, D_MODEL), f32),
        compiler_params=_cp(("arbitrary",)),
        name="out_proj_dec",
    )(*args)


def _mlp_kernel(x_ref, g2_ref, sc_ref, sh_ref, w1_ref, w2_ref, g3_ref, gate_ref, o_ref, *, tf):
    x = x_ref[...]
    h = (_rms(x, g2_ref[...]) * (1.0 + sc_ref[0]) + sh_ref[0]).astype(bf16)
    acc = jnp.zeros(x.shape, f32)
    for c in range(D_FF // tf):
        a = jnp.maximum(jnp.dot(h, w1_ref[:, c * tf:(c + 1) * tf], preferred_element_type=f32), 0.0)
        acc = acc + jnp.dot((a * a).astype(bf16), w2_ref[c * tf:(c + 1) * tf, :], preferred_element_type=f32)
    o_ref[...] = x + gate_ref[0] * _rms(acc, g3_ref[...])


def _mlp(x2, g2, sc, sh, w1, w2, g3, gate, *, tm, rows_per_mod):
    m = x2.shape[0]
    mod_rows = sc.shape[1]
    modmap = lambda i: ((i * tm) // rows_per_mod, 0, 0)
    const = lambda s: pl.BlockSpec(s, lambda i: (0, 0))
    mod = pl.BlockSpec((1, mod_rows, D_MODEL), modmap)
    return pl.pallas_call(
        functools.partial(_mlp_kernel, tf=512),
        grid=(m // tm,),
        in_specs=[pl.BlockSpec((tm, D_MODEL), lambda i: (i, 0)), const((1, D_MODEL)), mod, mod,
                  const((D_MODEL, D_FF)), const((D_FF, D_MODEL)), const((1, D_MODEL)), mod],
        out_specs=pl.BlockSpec((tm, D_MODEL), lambda i: (i, 0)),
        out_shape=jax.ShapeDtypeStruct((m, D_MODEL), f32),
        compiler_params=_cp(("parallel",)),
        name="mlp",
    )(x2, g2, sc, sh, w1, w2, g3, gate)


def _diag_rows(row, width):
    sub = lax.broadcasted_iota(i32, (8, width), 0)
    lane = lax.broadcasted_iota(i32, (8, width), 1)
    kept = jnp.where(lane // HEAD_DIM == sub, jnp.broadcast_to(row.astype(f32), (8, width)), 0.0)
    return kept.astype(row.dtype)


def _lane_to_sublane(row, offset, stride):
    sub = lax.broadcasted_iota(i32, (8, LANES), 0)
    lane = lax.broadcasted_iota(i32, (8, LANES), 1)
    return jnp.sum(jnp.where(lane == offset + stride * sub, jnp.broadcast_to(row, (8, LANES)), 0.0),
                   axis=1, keepdims=True)


def _nsa_q8(q_ref):
    sub = lax.broadcasted_iota(i32, (8, LANES), 0)
    lane = lax.broadcasted_iota(i32, (8, LANES), 1)
    q8 = jnp.zeros((8, LANES), f32)
    for p in range(G_NSA):
        blk = jnp.broadcast_to(q_ref[0, :, p * LANES:(p + 1) * LANES].astype(f32), (8, LANES))
        take = ((sub == p) & (lane < HALF)) | ((sub == p + G_NSA) & (lane >= HALF))
        q8 = jnp.where(take, blk, q8)
    return q8.astype(bf16)


def _dec_step(m, l, acc, s, mask, vb):
    if mask is not None:
        s = jnp.where(mask, s, NEG)
    m_new = jnp.maximum(m, jnp.max(s, axis=1, keepdims=True))
    alpha = jnp.exp(m - m_new)
    p = jnp.exp(s - m_new)
    if mask is not None:
        p = jnp.where(mask, p, 0.0)
    l = alpha * l + jnp.sum(p, axis=1, keepdims=True)
    acc = alpha * acc + jnp.dot(p.astype(bf16), vb, preferred_element_type=f32)
    return m_new, l, acc


def _fox_dec_kernel(pt_ref, fq_ref, misc_ref, bias_ref, knew_ref, *rest, pps):
    lf_refs = rest[:pps]
    kv_refs = rest[pps:2 * pps]
    o_ref, lfo_ref, m_scr, l_scr, acc_scr, c_scr = rest[2 * pps:]
    g = pl.program_id(1)
    q8 = _diag_rows(fq_ref[0], 384)
    lane = lax.broadcasted_iota(i32, (8, LANES), 1)

    @pl.when(g == 0)
    def _():
        lfrow = jnp.where(lane[0:1] < H_FOX, _log_sigmoid(misc_ref[0] + bias_ref[...]), 0.0)
        lfo_ref[0] = lfrow
        knew = knew_ref[0]
        m_scr[...] = jnp.sum(q8.astype(f32) * knew[:, :384], axis=1, keepdims=True)
        l_scr[...] = jnp.ones((8, 1), f32)
        acc_scr[...] = jnp.broadcast_to(knew[:, 384:], (8, 384))
        c_scr[...] = _lane_to_sublane(lfrow, 0, 1)

    m, l, acc, carry = m_scr[...], l_scr[...], acc_scr[...], c_scr[...]
    for i in range(pps):
        blk = kv_refs[i][0, 0]
        s = _nt(q8, blk[:, :384].astype(bf16))
        lf = lf_refs[i][0, 0]
        suf = lf
        for sft in (1, 2, 4, 8, 16, 32, 64):
            suf = suf + jnp.where(lane + sft < LANES, pltpu.roll(suf, LANES - sft, 1), 0.0)
        s = s + (carry + (suf - lf))
        carry = carry + suf[:, 0:1]
        m, l, acc = _dec_step(m, l, acc, s, None, blk[:, 384:].astype(bf16))
    m_scr[...], l_scr[...], acc_scr[...], c_scr[...] = m, l, acc, carry

    @pl.when(g == pl.num_programs(1) - 1)
    def _():
        o = _diag_rows(jnp.ones((1, 384), f32), 384) * (acc / l)
        o_ref[0] = jnp.sum(o, axis=0, keepdims=True).astype(bf16)


def _fox_decode(layer, page_table, fq, misc, bias_row, fkv_new, lf_t, cache_kv):
    b, n_pages = page_table.shape
    pps = FOX_PPS
    ng = n_pages // pps
    row3 = lambda w_: pl.BlockSpec((1, 1, w_), lambda i, g, pt: (i, 0, 0))

    def page(i, shape):
        return pl.BlockSpec((1, 1) + shape,
                            lambda bi, g, pt, i=i: (layer, pt[jnp.minimum(bi, b - 1),
                                                                  n_pages - 1 - (jnp.minimum(g, ng - 1) * pps + i)], 0, 0))

    grid_spec = pltpu.PrefetchScalarGridSpec(
        num_scalar_prefetch=1,
        grid=(b, ng),
        in_specs=[row3(384), row3(LANES), pl.BlockSpec((1, LANES), lambda i, g, pt: (0, 0)), row3(768)]
                 + [page(i, (8, PAGE_SIZE)) for i in range(pps)]
                 + [page(i, (PAGE_SIZE, 768)) for i in range(pps)],
        out_specs=[row3(384), row3(LANES)],
        scratch_shapes=[pltpu.VMEM((8, 1), f32), pltpu.VMEM((8, 1), f32), pltpu.VMEM((8, 384), f32),
                        pltpu.VMEM((8, 1), f32)],
    )
    return pl.pallas_call(
        functools.partial(_fox_dec_kernel, pps=pps),
        grid_spec=grid_spec,
        out_shape=[jax.ShapeDtypeStruct((b, 1, 384), bf16), jax.ShapeDtypeStruct((b, 1, LANES), f32)],
        compiler_params=_cp(("parallel", "arbitrary")),
        name="fox_decode",
    )(page_table, fq, misc, bias_row, fkv_new, *([lf_t] * pps), *([cache_kv] * pps))


def _nsa_cmp_dec_kernel(pt_ref, q_ref, *rest, pps, past):
    pg_refs = rest[:pps]
    p_ref, o_ref, mean_scr = rest[pps:]
    g = pl.program_id(1)
    per_page = PAGE_SIZE // CMP_BLOCK
    parts = []
    for i in range(pps):
        blk = pg_refs[i][0, 0]
        parts.append(jnp.sum(blk.reshape(per_page, CMP_BLOCK, 2 * LANES), axis=1) * (1.0 / CMP_BLOCK))
    rows = pps * per_page
    mean_scr[pl.ds(pl.multiple_of(g * rows, rows), rows), :] = jnp.concatenate(parts, axis=0)

    @pl.when(g == pl.num_programs(1) - 1)
    def _():
        mean = mean_scr[...]
        nbc = mean.shape[0]
        q8 = _nsa_q8(q_ref)
        s = _nt(q8, mean[:, :LANES].astype(bf16))
        j = lax.broadcasted_iota(i32, (8, nbc), 1)
        mask = ((j + 1) * CMP_BLOCK - 1) <= past
        s = jnp.where(mask, s, NEG)
        pe = jnp.where(mask, jnp.exp(s - jnp.max(s, axis=1, keepdims=True)), 0.0)
        den = jnp.sum(pe, axis=1, keepdims=True)
        p = pe / jnp.where(den > 0.0, den, 1.0)
        p_ref[0] = p
        o_ref[0] = jnp.dot(p.astype(bf16), mean[:, LANES:].astype(bf16), preferred_element_type=f32)


def _nsa_cmp_decode(layer, page_table, nq, cache_nsa, past):
    b, n_pages = page_table.shape
    pps = CMP_PPS
    nbc = past // CMP_BLOCK
    grid_spec = pltpu.PrefetchScalarGridSpec(
        num_scalar_prefetch=1,
        grid=(b, n_pages // pps),
        in_specs=[pl.BlockSpec((1, 1, 384), lambda i, g, pt: (i, 0, 0))]
                 + [pl.BlockSpec((1, 1, PAGE_SIZE, 2 * LANES),
                                 lambda bi, g, pt, i=i: (layer, pt[jnp.minimum(bi, b - 1),
                                                                   jnp.minimum(g, n_pages // pps - 1) * pps + i], 0, 0))
                    for i in range(pps)],
        out_specs=[pl.BlockSpec((1, 8, nbc), lambda i, g, pt: (i, 0, 0)),
                   pl.BlockSpec((1, 8, LANES), lambda i, g, pt: (i, 0, 0))],
        scratch_shapes=[pltpu.VMEM((nbc, 2 * LANES), f32)],
    )
    return pl.pallas_call(
        functools.partial(_nsa_cmp_dec_kernel, pps=pps, past=past),
        grid_spec=grid_spec,
        out_shape=[jax.ShapeDtypeStruct((b, 8, nbc), f32), jax.ShapeDtypeStruct((b, 8, LANES), f32)],
        compiler_params=_cp(("parallel", "arbitrary")),
        name="nsa_cmp_decode",
    )(page_table, nq, *([cache_nsa] * pps))


def _nsa_topk_dec_kernel(p_ref, idx_ref, *, past, width):
    b = p_ref.shape[0]
    nbc = p_ref.shape[2]
    nbs = -(-(past + 1) // SEL_BLOCK)
    imps = []
    for kv in range(KV_NSA):
        imps.append(p_ref[:, kv * G_NSA, :] + p_ref[:, kv * G_NSA + 1, :] + p_ref[:, kv * G_NSA + 2, :])
    imp = jnp.concatenate(imps, axis=0)
    imp = jnp.concatenate([imp, jnp.zeros((2 * b, width - nbc), f32)], axis=1)
    j = lax.broadcasted_iota(i32, (2 * b, width), 1)
    cur = past // SEL_BLOCK
    forced = (j == 0) | (j == cur) | (j == cur - 1)
    score = jnp.where(forced, FORCE, imp)
    score = jnp.where(j * SEL_BLOCK <= past, score, -1.0)
    score = jnp.where(j < nbs, score, -2.0)
    _, firsts = _select_topk(score, min(TOPK_BLOCKS, nbs))
    lane = lax.broadcasted_iota(i32, (2 * b, LANES), 1)
    out = jnp.zeros((2 * b, LANES), i32)
    for k, first in enumerate(firsts):
        out = jnp.where(lane == k, first.astype(i32), out)
    idx_ref[...] = out


def _nsa_topk_decode(p8, past):
    b, _, nbc = p8.shape
    width = 2 * nbc
    return pl.pallas_call(
        functools.partial(_nsa_topk_dec_kernel, past=past, width=width),
        grid=(1,),
        in_specs=[pl.BlockSpec(p8.shape, lambda i: (0, 0, 0))],
        out_specs=pl.BlockSpec((2 * b, LANES), lambda i: (0, 0)),
        out_shape=jax.ShapeDtypeStruct((2 * b, LANES), i32),
        compiler_params=_cp(("arbitrary",)),
        name="nsa_topk_decode",
    )(p8)


def _nsa_sel_dec_kernel(pt_ref, idx_ref, q_ref, misc_ref, ocmp_ref, nkv_ref, wnew_ref, win_ref, *rest, past, nsel):
    blk_refs = rest[:KV_NSA * nsel]
    o_ref = rest[KV_NSA * nsel]
    b = pl.program_id(0)
    n_past_blk = past // SEL_BLOCK
    q8 = _nsa_q8(q_ref)
    q8f = q8.astype(f32)
    sub = lax.broadcasted_iota(i32, (8, 1), 0)
    nkv_new = nkv_ref[0]
    m = jnp.full((8, 1), NEG, f32)
    l = jnp.zeros((8, 1), f32)
    acc = jnp.zeros((8, LANES), f32)
    s_new = jnp.sum(q8f * nkv_new[:, 2 * LANES:3 * LANES], axis=1, keepdims=True)
    v_new = jnp.broadcast_to(nkv_new[:, 3 * LANES:], (8, LANES))
    for kv in range(KV_NSA):
        in_group = (sub // G_NSA) == kv
        has_new = jnp.zeros((), jnp.bool_)
        for k in range(nsel):
            j = idx_ref[kv * pl.num_programs(0) + b, k]
            has_new = has_new | (j == n_past_blk)
            blk = blk_refs[kv * nsel + k][0, 0]
            mask = jnp.broadcast_to(in_group & (j < n_past_blk), (8, SEL_BLOCK))
            m, l, acc = _dec_step(m, l, acc, _nt(q8, blk[:, :LANES].astype(bf16)), mask,
                                  blk[:, LANES:].astype(bf16))
        new_ok = in_group & has_new
        m_new = jnp.maximum(m, jnp.where(new_ok, s_new, NEG))
        alpha = jnp.exp(m - m_new)
        p_new = jnp.where(new_ok, jnp.exp(s_new - m_new), 0.0)
        l = alpha * l + p_new
        acc = alpha * acc + p_new * v_new
        m = m_new
    o_sel = acc / jnp.where(l > 0.0, l, 1.0)
    win = win_ref[0, 0]
    wb = win.shape[0]
    kpos = past - wb + lax.broadcasted_iota(i32, (8, wb), 1)
    rel = past - kpos
    wmask = (rel >= 0) & (rel < WINDOW) & (kpos >= 0)
    m, l, acc = _dec_step(jnp.full((8, 1), NEG, f32), jnp.zeros((8, 1), f32), jnp.zeros((8, LANES), f32),
                          _nt(q8, win[:, :LANES].astype(bf16)), wmask, win[:, LANES:].astype(bf16))
    wnew = wnew_ref[0]
    s_w = jnp.sum(q8f * wnew[:, :LANES], axis=1, keepdims=True)
    m_new = jnp.maximum(m, s_w)
    alpha = jnp.exp(m - m_new)
    p_w = jnp.exp(s_w - m_new)
    o_win = (alpha * acc + p_w * jnp.broadcast_to(wnew[:, LANES:], (8, LANES))) / (alpha * l + p_w)
    misc = misc_ref[0]
    gates = [jax.nn.sigmoid(_lane_to_sublane(misc, GATE0 + r, 3)) for r in range(3)]
    o8 = gates[0] * ocmp_ref[0] + gates[1] * o_sel + gates[2] * o_win
    lo = lax.broadcasted_iota(i32, (1, LANES), 1) < HALF
    for p in range(G_NSA):
        o_ref[0, :, p * LANES:(p + 1) * LANES] = jnp.where(lo, o8[p:p + 1], o8[p + G_NSA:p + G_NSA + 1]).astype(bf16)


def _nsa_sel_decode(layer, page_table, idx, nq, misc, ocmp, nkv_new, win_new, cache_win, cache_nsa_blocks, past):
    b = page_table.shape[0]
    nsel = idx.shape[1]
    n_past_blk = past // SEL_BLOCK
    per_page = PAGE_SIZE // SEL_BLOCK
    wb = cache_win.shape[2]
    row3 = lambda w_: pl.BlockSpec((1, 1, w_), lambda i, pt, ix: (i, 0, 0))

    def blk_spec(kv, k):
        def imap(i, pt, ix):
            ii = jnp.minimum(i, b - 1)
            jp = jnp.clip(ix[kv * b + ii, k], 0, n_past_blk - 1)
            return (layer, pt[ii, jp // per_page] * per_page + jp % per_page, 0, 1)
        return pl.BlockSpec((1, 1, SEL_BLOCK, 2 * LANES), imap)

    grid_spec = pltpu.PrefetchScalarGridSpec(
        num_scalar_prefetch=2,
        grid=(b,),
        in_specs=[row3(384), row3(LANES), pl.BlockSpec((1, 8, LANES), lambda i, pt, ix: (i, 0, 0)),
                  row3(512), row3(256),
                  pl.BlockSpec((1, 1, wb, 2 * LANES), lambda i, pt, ix: (layer, i, 0, 0))]
                 + [blk_spec(kv, k) for kv in range(KV_NSA) for k in range(nsel)],
        out_specs=row3(384),
    )
    return pl.pallas_call(
        functools.partial(_nsa_sel_dec_kernel, past=past, nsel=nsel),
        grid_spec=grid_spec,
        out_shape=jax.ShapeDtypeStruct((b, 1, 384), bf16),
        compiler_params=_cp(("parallel",)),
        name="nsa_sel_decode",
    )(page_table, idx, nq, misc, ocmp, nkv_new, win_new, cache_win, *([cache_nsa_blocks] * (KV_NSA * nsel)))


def _proj_columns():
    off = np.cumsum([0, 384, 384, 384, 6, 384, 128, 128, 128, 128, 128, 128, 18, 256])
    fq, fk, fv, ff, nq, nkc, nvc, nks, nvs, nkw, nvw, ng, u = [int(o) for o in off[:13]]
    cols = list(range(fq, fq + 384))
    for h in NSA_PAIR_ORDER:
        cols += list(range(nq + h * HEAD_DIM, nq + (h + 1) * HEAD_DIM))
    cols += list(range(fk, fk + 768))
    cols += list(range(nkc, nkc + 512))
    cols += list(range(nkw, nkw + 256))
    cols += list(range(u, u + 256))
    cols += list(range(ff, ff + 6)) + list(range(ng, ng + 18)) + [N_IN] * (LANES - 24)
    return np.asarray(cols, np.int32)


def _out_rows():
    rows = list(range(0, 384))
    for h in NSA_PAIR_ORDER:
        rows += list(range(384 + h * HEAD_DIM, 384 + (h + 1) * HEAD_DIM))
    rows += list(range(768, 1024))
    return np.asarray(rows, np.int32)


def _rope_tables(pos):
    half = ROT_DIM // 2
    inv = ROPE_THETA ** (-jnp.arange(0, ROT_DIM, 2, dtype=f32) / ROT_DIM)
    ang = pos.astype(f32)[:, None] * inv[None, :]
    cos, sin = jnp.cos(ang), jnp.sin(ang)
    n = pos.shape[0]
    one = jnp.ones((n, HEAD_DIM - ROT_DIM), f32)
    zero8 = jnp.zeros((n, half), f32)
    zrest = jnp.zeros((n, HEAD_DIM - ROT_DIM), f32)
    c = jnp.concatenate([cos, cos, one], axis=1)
    sa = jnp.concatenate([-sin, zero8, zrest], axis=1)
    sb = jnp.concatenate([zero8, sin, zrest], axis=1)
    return tuple(jnp.concatenate([a, a], axis=1) for a in (c, sa, sb))


def kernel(x_prompt, x_sample, cache_fox_kv, cache_fox_logf, cache_nsa_kv, cache_nsa_win, state_pool, page_table,
           c_prompt, c_sample, w_ada, b_ada, norm_g, w_in, b_fox_f, w_out, w_pool, pool_scale, w_ff1, w_ff2):
    depth = w_in.shape[0]
    bp, t, _ = x_prompt.shape
    bs = x_sample.shape[0]
    n_pool = cache_fox_kv.shape[1]
    past = page_table.shape[1] * PAGE_SIZE
    wb = cache_nsa_win.shape[2]
    mp = bp * t

    cols = _proj_columns()
    w_in_p = jnp.concatenate([w_in, jnp.zeros((depth, D_MODEL, 1), f32)], axis=2)[:, :, cols].astype(bf16)
    w_out_p = w_out[:, _out_rows(), :].astype(bf16)
    w_pool_bd = jnp.zeros((depth, C_POOL, C_POOL), f32)
    for g in range(len(POOL_WINDOWS)):
        w_pool_bd = w_pool_bd.at[:, g * POOL_GW:(g + 1) * POOL_GW, g * POOL_GW:(g + 1) * POOL_GW].set(w_pool[:, g])
    w_pool_bd = w_pool_bd.astype(bf16)
    w1 = w_ff1.astype(bf16)
    w2 = w_ff2.astype(bf16)
    bias_rows = jnp.pad(b_fox_f, ((0, 0), (0, LANES - H_FOX))).reshape(depth, 1, LANES)

    rope_p = _rope_tables(jnp.arange(t))
    rope_s = _rope_tables(jnp.full((bs,), past, i32))

    fox_pages = cache_fox_kv.reshape(depth, n_pool, PAGE_SIZE, 2 * H_FOX * HEAD_DIM)
    nsa_pages = cache_nsa_kv.reshape(depth, n_pool, PAGE_SIZE, 4 * KV_NSA * HEAD_DIM)
    nsa_blocks = cache_nsa_kv.reshape(depth, n_pool * (PAGE_SIZE // SEL_BLOCK), SEL_BLOCK, 4 * KV_NSA * HEAD_DIM)
    win_rows = cache_nsa_win.reshape(depth, bs, wb, 2 * KV_NSA * HEAD_DIM)
    lf_t = jnp.pad(jnp.swapaxes(cache_fox_logf, 2, 3), ((0, 0), (0, 0), (0, 8 - H_FOX), (0, 0)))

    mod = _ada(jnp.concatenate([c_prompt, c_sample], axis=0), w_ada, b_ada)
    mod = mod.reshape(depth, bp + bs, 6, D_MODEL)

    yp = x_prompt.reshape(mp, D_MODEL)
    ys = x_sample.reshape(bs, D_MODEL)
    sp, ss = [], []
    for l in range(depth):
        g = norm_g[l].reshape(4, 1, D_MODEL)
        modp = [mod[l, :bp, k].reshape(bp, 1, D_MODEL) for k in range(6)]
        mods = [mod[l, bp:, k].reshape(1, bs, D_MODEL) for k in range(6)]

        (fq, nq, fkv, nkv, win, u, misc, fkvb, nkvb, winb, kcm) = _inproj(
            yp, g[0], modp[1], modp[0], w_in_p[l], *rope_p, tm=512, rows_per_mod=t, rope_rows=t, with_means=True)
        logf, ccol, crow = _foxprep(misc, bias_rows[l], bp, t)
        o_fox = _fox_attn(fq, fkvb, ccol, crow, bp, t)
        o_nsa = _nsa_attn(nq, kcm, nkvb, winb, misc, bp, t)
        y1 = _outproj(yp, o_fox, o_nsa, u, w_out_p[l], w_pool_bd[l], pool_scale[l].reshape(1, C_POOL), g[1],
                      modp[2], tm=512, t=t)
        yp = _mlp(y1, g[2], modp[4], modp[3], w1[l], w2[l], g[3], modp[5], tm=512, rows_per_mod=t)
        wl = min(WINDOW, t)
        sp.append((fkv.reshape(bp, t, 2, H_FOX, HEAD_DIM), logf.reshape(bp, t, H_FOX),
                   nkv.reshape(bp, t, 4, KV_NSA, HEAD_DIM),
                   win.reshape(bp, t, 2, KV_NSA, HEAD_DIM)[:, t - wl:],
                   u.reshape(bp, t, C_POOL)[:, t - POOL_STATE:]))

        (fq_s, nq_s, fkv_s, nkv_s, win_s, u_s, misc_s, _, _, _, _) = _inproj(
            ys, g[0], mods[1], mods[0], w_in_p[l], *rope_s, tm=bs, rows_per_mod=bs, rope_rows=bs, with_means=False)
        r3 = lambda a: a.reshape(bs, 1, a.shape[-1])
        o_fox_s, logf_s = _fox_decode(l, page_table, r3(fq_s), r3(misc_s), bias_rows[l], r3(fkv_s), lf_t, fox_pages)
        p8, o_cmp = _nsa_cmp_decode(l, page_table, r3(nq_s), nsa_pages, past)
        idx = _nsa_topk_decode(p8, past)[:, :min(TOPK_BLOCKS, -(-(past + 1) // SEL_BLOCK))]
        o_nsa_s = _nsa_sel_decode(l, page_table, idx, r3(nq_s), r3(misc_s), o_cmp, r3(nkv_s), r3(win_s),
                                  win_rows, nsa_blocks, past)
        u_ext = jnp.concatenate([state_pool[l], u_s.reshape(bs, 1, C_POOL)], axis=1)
        y1s = _outproj_dec(ys, o_fox_s.reshape(bs, 384), o_nsa_s.reshape(bs, 384), u_ext, w_out_p[l],
                           w_pool_bd[l], pool_scale[l].reshape(1, C_POOL), g[1], mods[2][0], past)
        ys = _mlp(y1s, g[2], mods[4], mods[3], w1[l], w2[l], g[3], mods[5], tm=bs, rows_per_mod=bs)
        win_all = jnp.concatenate([win_rows[l], win_s.reshape(bs, 1, 2 * KV_NSA * HEAD_DIM)], axis=1)
        ss.append((fkv_s.reshape(bs, 1, 2, H_FOX, HEAD_DIM), logf_s[:, :, :H_FOX],
                   nkv_s.reshape(bs, 1, 4, KV_NSA, HEAD_DIM),
                   win_all[:, 1:].reshape(bs, wb, 2, KV_NSA, HEAD_DIM),
                   u_ext[:, 1:]))

    stk = lambda lst, i: jnp.stack([s[i] for s in lst], axis=0)
    return (yp.reshape(bp, t, D_MODEL), ys.reshape(bs, 1, D_MODEL),
            stk(sp, 0), stk(sp, 1), stk(sp, 2), stk(sp, 3), stk(sp, 4),
            stk(ss, 0), stk(ss, 1), stk(ss, 2), stk(ss, 3), stk(ss, 4))
```

```python
import functools

import numpy as np
import jax
import jax.numpy as jnp
from jax import lax
from jax.experimental import pallas as pl
from jax.experimental.pallas import tpu as pltpu

f32 = jnp.float32
bf16 = jnp.bfloat16
i32 = jnp.int32

D_MODEL = 1024
HEAD_DIM = 64
H_FOX = 6
H_NSA = 6
KV_NSA = 2
G_NSA = H_NSA // KV_NSA
POOL_WINDOWS = (2, 4, 8, 16)
C_POOL = 256
POOL_GW = 64
POOL_STATE = 15
ROT_DIM = 16
ROPE_THETA = 500000.0
CMP_BLOCK = 64
SEL_BLOCK = 64
TOPK_BLOCKS = 8
WINDOW = 512
PAGE_SIZE = 128
D_FF = 4 * D_MODEL
EPS = 1e-6
NEG = -1e30
FORCE = 1e4
SCALE = HEAD_DIM ** -0.5
N_IN = 2584

LANES = 128
HALF = LANES // 2
VMEM_LIMIT = 56 * 1024 * 1024

C_FQ = 0
C_NQ = 384
C_FKV = 768
C_NKV = 1536
C_WIN = 2048
C_U = 2304
C_MISC = 2560
N_PROJ = 2688
GATE0 = H_FOX
NSA_PAIR_ORDER = (0, 3, 1, 4, 2, 5)

FOX_TQ = 256
FOX_TK = 256
NSA_TQ = 128
NSA_TK = 256
FOX_PPS = 8
CMP_PPS = 8


def _cp(sem):
    return pltpu.CompilerParams(dimension_semantics=sem, vmem_limit_bytes=VMEM_LIMIT)


def _nt(a, b):
    return lax.dot_general(a, b, (((1,), (1,)), ((), ())), preferred_element_type=f32)


def _rms(x, g):
    return x * lax.rsqrt(jnp.mean(x * x, axis=-1, keepdims=True) + EPS) * g


def _log_sigmoid(x):
    return jnp.minimum(x, 0.0) - jnp.log1p(jnp.exp(-jnp.abs(x)))


def _ada_kernel(c_ref, w_ref, b_ref, o_ref):
    c = c_ref[...]
    a = (c * jax.nn.sigmoid(c)).astype(bf16)
    o_ref[0] = jnp.dot(a, w_ref[0].astype(bf16), preferred_element_type=f32) + b_ref[0]


def _ada(c_all, w_ada, b_ada):
    depth = w_ada.shape[0]
    n = c_all.shape[0]
    tn = 1024
    return pl.pallas_call(
        _ada_kernel,
        grid=(depth, 6 * D_MODEL // tn),
        in_specs=[pl.BlockSpec((n, D_MODEL), lambda l, j: (0, 0)),
                  pl.BlockSpec((1, D_MODEL, tn), lambda l, j: (l, 0, j)),
                  pl.BlockSpec((1, 1, tn), lambda l, j: (l, 0, j))],
        out_specs=pl.BlockSpec((1, n, tn), lambda l, j: (l, 0, j)),
        out_shape=jax.ShapeDtypeStruct((depth, n, 6 * D_MODEL), f32),
        compiler_params=_cp(("parallel", "parallel")),
        name="ada_mod",
    )(c_all, w_ada, b_ada.reshape(depth, 1, 6 * D_MODEL))


def _inproj_kernel(x_ref, g_ref, sc_ref, sh_ref, w_ref, cos_ref, sa_ref, sb_ref,
                   fq_ref, nq_ref, fkv_ref, nkv_ref, win_ref, u_ref, misc_ref,
                   fkvb_ref, nkvb_ref, winb_ref, kcm_ref, *, tm, with_means):
    x = x_ref[...]
    h = (_rms(x, g_ref[...]) * (1.0 + sc_ref[0]) + sh_ref[0]).astype(bf16)
    cos = cos_ref[...]
    sa = sa_ref[...]
    sb = sb_ref[...]

    def rope(z):
        return z * cos + pltpu.roll(z, LANES - ROT_DIM // 2, 1) * sa + pltpu.roll(z, ROT_DIM // 2, 1) * sb

    def means(z):
        return jnp.sum(z.reshape(tm // CMP_BLOCK, CMP_BLOCK, LANES), axis=1) * (1.0 / CMP_BLOCK)

    def emit(blk, z):
        c = blk * LANES
        if c < C_NQ:
            fq_ref[:, c - C_FQ:c - C_FQ + LANES] = (z * SCALE).astype(bf16)
        elif c < C_FKV:
            nq_ref[:, c - C_NQ:c - C_NQ + LANES] = (rope(z) * SCALE).astype(bf16)
        elif c < C_NKV:
            fkv_ref[:, c - C_FKV:c - C_FKV + LANES] = z
            fkvb_ref[:, c - C_FKV:c - C_FKV + LANES] = z.astype(bf16)
        elif c < C_WIN:
            o = c - C_NKV
            if o in (0, 2 * LANES):
                z = rope(z)
            nkv_ref[:, o:o + LANES] = z
            nkvb_ref[:, o:o + LANES] = z.astype(bf16)
            if with_means and o < 2 * LANES:
                kcm_ref[:, o:o + LANES] = means(z)
        elif c < C_U:
            o = c - C_WIN
            if o == 0:
                z = rope(z)
            win_ref[:, o:o + LANES] = z
            winb_ref[:, o:o + LANES] = z.astype(bf16)
        elif c < C_MISC:
            u_ref[:, c - C_U:c - C_U + LANES] = z
        else:
            misc_ref[...] = z

    nblk = N_PROJ // LANES
    for b0 in range(0, nblk, 2):
        nb = min(2, nblk - b0)
        z2 = jnp.dot(h, w_ref[:, b0 * LANES:(b0 + nb) * LANES], preferred_element_type=f32)
        for k in range(nb):
            emit(b0 + k, z2[:, k * LANES:(k + 1) * LANES])
    if not with_means:
        kcm_ref[...] = jnp.zeros(kcm_ref.shape, f32)


def _inproj(x2, g0, sc, sh, w, cos, sa, sb, *, tm, rows_per_mod, rope_rows, with_means):
    m = x2.shape[0]
    nt = m // tm
    mod_rows = sc.shape[1]
    rope_blocks = rope_rows // tm
    kc_rows = max(tm // CMP_BLOCK, 8)
    modmap = lambda i: ((i * tm) // rows_per_mod, 0, 0)
    ropemap = lambda i: (i % rope_blocks, 0)
    row = lambda w_: pl.BlockSpec((tm, w_), lambda i: (i, 0))
    outs = [((m, 384), bf16), ((m, 384), bf16), ((m, 768), f32), ((m, 512), f32), ((m, 256), f32),
            ((m, 256), f32), ((m, LANES), f32), ((m, 768), bf16), ((m, 512), bf16), ((m, 256), bf16)]
    out_shape = [jax.ShapeDtypeStruct(s, d) for s, d in outs] + [jax.ShapeDtypeStruct((nt * kc_rows, 256), f32)]
    out_specs = [row(s[1]) for s, _ in outs] + [pl.BlockSpec((kc_rows, 256), lambda i: (i, 0))]
    return pl.pallas_call(
        functools.partial(_inproj_kernel, tm=tm, with_means=with_means),
        grid=(nt,),
        in_specs=[row(D_MODEL),
                  pl.BlockSpec((1, D_MODEL), lambda i: (0, 0)),
                  pl.BlockSpec((1, mod_rows, D_MODEL), modmap),
                  pl.BlockSpec((1, mod_rows, D_MODEL), modmap),
                  pl.BlockSpec((D_MODEL, N_PROJ), lambda i: (0, 0)),
                  pl.BlockSpec((tm, LANES), ropemap),
                  pl.BlockSpec((tm, LANES), ropemap),
                  pl.BlockSpec((tm, LANES), ropemap)],
        out_specs=out_specs,
        out_shape=out_shape,
        compiler_params=_cp(("parallel",)),
        name="in_proj",
    )(x2, g0, sc, sh, w, cos, sa, sb)


def _foxprep_kernel(misc_ref, bias_ref, lf_ref, ccol_ref, crow_ref, *, t):
    blk = 256
    lane = lax.broadcasted_iota(i32, (t, LANES), 1)
    lf = jnp.where(lane < H_FOX, _log_sigmoid(misc_ref[...] + bias_ref[...]), 0.0)
    lf_ref[...] = lf[:, :H_FOX]
    r = lax.broadcasted_iota(i32, (blk, blk), 0)
    c = lax.broadcasted_iota(i32, (blk, blk), 1)
    ltri = (r >= c).astype(f32)
    carry = jnp.zeros((1, LANES), f32)
    for b in range(t // blk):
        cb = jnp.dot(ltri, lf[b * blk:(b + 1) * blk], preferred_element_type=f32,
                     precision=lax.Precision.HIGHEST) + carry
        ccol_ref[b * blk:(b + 1) * blk, :] = cb
        carry = cb[blk - 1:blk, :]
    er = lax.broadcasted_iota(i32, (8, LANES), 0)
    ec = lax.broadcasted_iota(i32, (8, LANES), 1)
    eye = (er == ec).astype(f32)
    crow_ref[0] = lax.dot_general(eye, ccol_ref[...], (((1,), (1,)), ((), ())),
                                  preferred_element_type=f32, precision=lax.Precision.HIGHEST)


def _foxprep(misc, bias_row, b, t):
    return pl.pallas_call(
        functools.partial(_foxprep_kernel, t=t),
        grid=(b,),
        in_specs=[pl.BlockSpec((t, LANES), lambda i: (i, 0)),
                  pl.BlockSpec((1, LANES), lambda i: (0, 0))],
        out_specs=[pl.BlockSpec((t, H_FOX), lambda i: (i, 0)),
                   pl.BlockSpec((t, LANES), lambda i: (i, 0)),
                   pl.BlockSpec((1, 8, t), lambda i: (i, 0, 0))],
        out_shape=[jax.ShapeDtypeStruct((b * t, H_FOX), f32),
                   jax.ShapeDtypeStruct((b * t, LANES), f32),
                   jax.ShapeDtypeStruct((b, 8, t), f32)],
        compiler_params=_cp(("parallel",)),
        name="fox_prep",
    )(misc, bias_row)


def _softmax_step(carry, s, mask, vb):
    m, l, acc = carry
    s = jnp.where(mask, s, NEG)
    m_new = jnp.maximum(m, jnp.max(s, axis=1, keepdims=True))
    alpha = jnp.exp(m - m_new)
    p = jnp.where(mask, jnp.exp(s - m_new), 0.0)
    l = alpha * l + jnp.sum(p, axis=1, keepdims=True)
    acc = alpha * acc + jnp.dot(p.astype(bf16), vb, preferred_element_type=f32)
    return m_new, l, acc


def _fox_kernel(q_ref, kv_ref, ccol_ref, crow_ref, o_ref, *, tq, tk):
    qi = pl.program_id(1)
    lo = lax.broadcasted_iota(i32, (tq, LANES), 1) < HALF
    rows = lax.broadcasted_iota(i32, (tq, tk), 0) + qi * tq
    cols = lax.broadcasted_iota(i32, (tq, tk), 1)
    zero = jnp.zeros((), bf16)
    for p in range(H_FOX // 2):
        q = q_ref[:, p * LANES:(p + 1) * LANES]
        qst = jnp.concatenate([jnp.where(lo, q, zero), jnp.where(lo, zero, q)], axis=0)
        cq0 = ccol_ref[:, 2 * p:2 * p + 1]
        cq1 = ccol_ref[:, 2 * p + 1:2 * p + 2]

        def body(c, carry, p=p, qst=qst, cq0=cq0, cq1=cq1):
            k0 = pl.multiple_of(c * tk, tk)
            kb = kv_ref[pl.ds(k0, tk), p * LANES:(p + 1) * LANES]
            vb = kv_ref[pl.ds(k0, tk), 384 + p * LANES:384 + (p + 1) * LANES]
            ck0 = crow_ref[0, 2 * p:2 * p + 1, pl.ds(k0, tk)]
            ck1 = crow_ref[0, 2 * p + 1:2 * p + 2, pl.ds(k0, tk)]
            s = _nt(qst, kb) + jnp.concatenate([cq0 - ck0, cq1 - ck1], axis=0)
            mask = (cols + k0) <= rows
            return _softmax_step(carry, s, jnp.concatenate([mask, mask], axis=0), vb)

        init = (jnp.full((2 * tq, 1), NEG, f32), jnp.zeros((2 * tq, 1), f32), jnp.zeros((2 * tq, LANES), f32))
        nchunks = (qi * tq + tq + tk - 1) // tk
        _, l, acc = lax.fori_loop(0, nchunks, body, init)
        o = acc / l
        o_ref[:, p * LANES:(p + 1) * LANES] = jnp.where(lo, o[:tq], o[tq:]).astype(bf16)


def _fox_attn(fq, fkvb, ccol, crow, b, t):
    tq, tk = FOX_TQ, FOX_TK
    nq = t // tq
    return pl.pallas_call(
        functools.partial(_fox_kernel, tq=tq, tk=tk),
        grid=(b, nq),
        in_specs=[pl.BlockSpec((tq, 384), lambda i, j: (i * nq + j, 0)),
                  pl.BlockSpec((t, 768), lambda i, j: (i, 0)),
                  pl.BlockSpec((tq, LANES), lambda i, j: (i * nq + j, 0)),
                  pl.BlockSpec((1, 8, t), lambda i, j: (i, 0, 0))],
        out_specs=pl.BlockSpec((tq, 384), lambda i, j: (i * nq + j, 0)),
        out_shape=jax.ShapeDtypeStruct((b * t, 384), bf16),
        compiler_params=_cp(("parallel", "parallel")),
        name="fox_attn",
    )(fq, fkvb, ccol, crow)


def _select_topk(score, nsel):
    nb = score.shape[1]
    jf = lax.broadcasted_iota(i32, score.shape, 1).astype(f32)
    sel = jnp.zeros(score.shape, f32)
    firsts = []
    for _ in range(nsel):
        mx = jnp.max(score, axis=1, keepdims=True)
        first = jnp.min(jnp.where(score == mx, jf, float(nb)), axis=1, keepdims=True)
        hit = jf == first
        sel = jnp.where(hit, 1.0, sel)
        score = jnp.where(hit, -3.0, score)
        firsts.append(first)
    return sel, firsts


def _nsa_kernel(q_ref, kcm_ref, nkv_ref, win_ref, misc_ref, o_ref, *, tq, tk, t):
    qi = pl.program_id(1)
    s0 = qi * tq
    nbc = t // CMP_BLOCK
    lane = lax.broadcasted_iota(i32, (tq, LANES), 1)
    lo = lane < HALF
    zero = jnp.zeros((), bf16)
    rows = lax.broadcasted_iota(i32, (tq, tk), 0) + s0
    cols = lax.broadcasted_iota(i32, (tq, tk), 1)
    qpos_b = lax.broadcasted_iota(i32, (tq, LANES), 0) + s0
    jb = lax.broadcasted_iota(i32, (tq, LANES), 1)
    misc = misc_ref[...]

    def tile3(a):
        return jnp.concatenate([a, a, a], axis=0)

    def attend(qk, kref, klane, vlane, c_lo, c_hi, maskfn):
        def body(c, carry):
            k0 = pl.multiple_of(c * tk, tk)
            kb = kref[pl.ds(k0, tk), klane:klane + LANES]
            vb = kref[pl.ds(k0, tk), vlane:vlane + LANES]
            return _softmax_step(carry, _nt(qk, kb), tile3(maskfn(k0)), vb)
        init = (jnp.full((3 * tq, 1), NEG, f32), jnp.zeros((3 * tq, 1), f32), jnp.zeros((3 * tq, LANES), f32))
        _, l, acc = lax.fori_loop(c_lo, c_hi, body, init)
        return acc / l

    kcm = jnp.concatenate([kcm_ref[...], jnp.zeros((LANES - nbc, 2 * LANES), f32)], axis=0)
    kc = kcm[:, :LANES].astype(bf16)
    vc = kcm[:, LANES:].astype(bf16)
    outs = []
    for kv in range(KV_NSA):
        keep = lo if kv == 0 else jnp.logical_not(lo)
        qk = jnp.concatenate([jnp.where(keep, q_ref[:, p * LANES:(p + 1) * LANES], zero)
                              for p in range(G_NSA)], axis=0)
        cmask = (((jb + 1) * CMP_BLOCK - 1) <= qpos_b) & (jb < nbc)
        cmask3 = tile3(cmask)
        sc = jnp.where(cmask3, _nt(qk, kc), NEG)
        mx = jnp.max(sc, axis=1, keepdims=True)
        pe = jnp.where(cmask3, jnp.exp(sc - mx), 0.0)
        den = jnp.sum(pe, axis=1, keepdims=True)
        pc = pe / jnp.where(den > 0.0, den, 1.0)
        o_cmp = jnp.dot(pc.astype(bf16), vc, preferred_element_type=f32)
        imp = pc[:tq] + pc[tq:2 * tq] + pc[2 * tq:]
        cur = qpos_b // SEL_BLOCK
        forced = (jb == 0) | (jb == cur) | (jb == cur - 1)
        score = jnp.where(forced, FORCE, imp)
        score = jnp.where(jb * SEL_BLOCK <= qpos_b, score, -1.0)
        score = jnp.where(jb < nbc, score, -2.0)
        sel, _ = _select_topk(score, min(TOPK_BLOCKS, nbc))
        selb = sel.astype(bf16)

        def sel_mask(k0, selb=selb):
            er = lax.broadcasted_iota(i32, (LANES, tk), 0)
            ec = (lax.broadcasted_iota(i32, (LANES, tk), 1) + k0) // SEL_BLOCK
            expand = jnp.where(er == ec, 1.0, 0.0).astype(bf16)
            picked = jnp.dot(selb, expand, preferred_element_type=f32) > 0.5
            return picked & ((cols + k0) <= rows)

        def win_mask(k0):
            rel = rows - (cols + k0)
            return (rel >= 0) & (rel < WINDOW)

        o_sel = attend(qk, nkv_ref, 2 * LANES, 3 * LANES, 0, (s0 + tq + tk - 1) // tk, sel_mask)
        w_lo = jnp.maximum(s0 - WINDOW + 1, 0) // tk
        o_win = attend(qk, win_ref, 0, LANES, w_lo, (s0 + tq + tk - 1) // tk, win_mask)
        for g in range(G_NSA):
            h = kv * G_NSA + g
            gate = jax.nn.sigmoid(misc[:, GATE0 + 3 * h:GATE0 + 3 * h + 3])
            sl = slice(g * tq, (g + 1) * tq)
            outs.append(gate[:, 0:1] * o_cmp[sl] + gate[:, 1:2] * o_sel[sl] + gate[:, 2:3] * o_win[sl])
    for p in range(G_NSA):
        o_ref[:, p * LANES:(p + 1) * LANES] = jnp.where(lo, outs[p], outs[G_NSA + p]).astype(bf16)


def _nsa_attn(nq, kcm, nkvb, winb, misc, b, t):
    tq, tk = NSA_TQ, NSA_TK
    nq_t = t // tq
    nbc = t // CMP_BLOCK
    return pl.pallas_call(
        functools.partial(_nsa_kernel, tq=tq, tk=tk, t=t),
        grid=(b, nq_t),
        in_specs=[pl.BlockSpec((tq, 384), lambda i, j: (i * nq_t + j, 0)),
                  pl.BlockSpec((nbc, 256), lambda i, j: (i, 0)),
                  pl.BlockSpec((t, 512), lambda i, j: (i, 0)),
                  pl.BlockSpec((t, 256), lambda i, j: (i, 0)),
                  pl.BlockSpec((tq, LANES), lambda i, j: (i * nq_t + j, 0))],
        out_specs=pl.BlockSpec((tq, 384), lambda i, j: (i * nq_t + j, 0)),
        out_shape=jax.ShapeDtypeStruct((b * t, 384), bf16),
        compiler_params=_cp(("parallel", "parallel")),
        name="nsa_attn",
    )(nq, kcm, nkvb, winb, misc)


def _pool_project(d, wp_ref, ps_ref):
    return (jnp.dot(d.astype(bf16), wp_ref[...], preferred_element_type=f32) * ps_ref[...]).astype(bf16)


def _out_tail(x, ofox, onsa, opool, wo_ref, g_ref, gate):
    o = (jnp.dot(ofox, wo_ref[0:384, :], preferred_element_type=f32)
         + jnp.dot(onsa, wo_ref[384:768, :], preferred_element_type=f32)
         + jnp.dot(opool, wo_ref[768:1024, :], preferred_element_type=f32))
    return x + gate * _rms(o, g_ref[...])


def _outproj_kernel(x_ref, ofox_ref, onsa_ref, u_ref, halo_ref, wo_ref, wp_ref, ps_ref, g_ref, gate_ref,
                    o_ref, ext_ref, *, tm, t):
    i = pl.program_id(0)
    pos0 = (i * tm) % t
    u = u_ref[...]
    ext_ref[0:16, :] = jnp.where(pos0 > 0, halo_ref[...], 0.0)
    ext_ref[16:, :] = u
    pos = lax.broadcasted_iota(i32, (tm, LANES), 0) + pos0
    lo = lax.broadcasted_iota(i32, (tm, LANES), 1) < HALF

    def shifted(k, c):
        return ext_ref[16 - k:16 - k + tm, c * LANES:(c + 1) * LANES]

    def cnt(w):
        return jnp.minimum(w, pos + 1).astype(f32)

    ds = []
    for c, (wa, wb) in enumerate(((POOL_WINDOWS[0], POOL_WINDOWS[1]), (POOL_WINDOWS[2], POOL_WINDOWS[3]))):
        run = shifted(0, c)
        sums = {}
        for k in range(1, wb):
            run = run + shifted(k, c)
            if k + 1 in (wa, wb):
                sums[k + 1] = run
        ds.append(jnp.where(lo, sums[wa] / cnt(wa), sums[wb] / cnt(wb)) - u[:, c * LANES:(c + 1) * LANES])
    opool = _pool_project(jnp.concatenate(ds, axis=1), wp_ref, ps_ref)
    o_ref[...] = _out_tail(x_ref[...], ofox_ref[...], onsa_ref[...], opool, wo_ref, g_ref, gate_ref[0])


def _outproj(x2, ofox, onsa, u, wo, wp, ps, g1, gate, *, tm, t):
    m = x2.shape[0]
    row = lambda w_: pl.BlockSpec((tm, w_), lambda i: (i, 0))
    const = lambda s: pl.BlockSpec(s, lambda i: (0, 0))
    return pl.pallas_call(
        functools.partial(_outproj_kernel, tm=tm, t=t),
        grid=(m // tm,),
        in_specs=[row(D_MODEL), row(384), row(384), row(256),
                  pl.BlockSpec((16, 256), lambda i: (jnp.maximum(i * (tm // 16) - 1, 0), 0)),
                  const((D_MODEL, D_MODEL)), const((256, 256)), const((1, 256)), const((1, D_MODEL)),
                  pl.BlockSpec((1, 1, D_MODEL), lambda i: ((i * tm) // t, 0, 0))],
        out_specs=row(D_MODEL),
        out_shape=jax.ShapeDtypeStruct((m, D_MODEL), f32),
        scratch_shapes=[pltpu.VMEM((tm + 16, 256), f32)],
        compiler_params=_cp(("parallel",)),
        name="out_proj",
    )(x2, ofox, onsa, u, u, wo, wp, ps, g1, gate)


def _outproj_dec_kernel(x_ref, ofox_ref, onsa_ref, ext_ref, wo_ref, wp_ref, ps_ref, g_ref, gate_ref, o_ref, *, past):
    ext = ext_ref[...]
    n = ext.shape[1]
    r = lax.broadcasted_iota(i32, ext.shape, 1)
    lane = lax.broadcasted_iota(i32, (ext.shape[0], C_POOL), 1)
    u_new = jnp.sum(jnp.where(r == n - 1, ext, 0.0), axis=1)
    d = jnp.zeros(u_new.shape, f32)
    for g, w in enumerate(POOL_WINDOWS):
        sw = jnp.sum(jnp.where(r >= n - w, ext, 0.0), axis=1)
        d = jnp.where(lane // POOL_GW == g, sw / float(min(w, past + 1)), d)
    opool = _pool_project(d - u_new, wp_ref, ps_ref)
    o_ref[...] = _out_tail(x_ref[...], ofox_ref[...], onsa_ref[...], opool, wo_ref, g_ref, gate_ref[...])


def _outproj_dec(x2, ofox, onsa, ext, wo, wp, ps, g1, gate, past):
    m = x2.shape[0]
    full = lambda a: pl.BlockSpec(a.shape, lambda i: (0,) * a.ndim)
    args = (x2, ofox, onsa, ext, wo, wp, ps, g1, gate)
    return pl.pallas_call(
        functools.partial(_outproj_dec_kernel, past=past),
        grid=(1,),
        in_specs=[full(a) for a in args],
        out_specs=pl.BlockSpec((m, D_MODEL), lambda i: (0, 0)),
        out_shape=jax.ShapeDtypeStruct((m, D_MODEL), f32),
        compiler_params=_cp(("arbitrary",)),
        name="out_proj_dec",
    )(*args)


def _mlp_kernel(x_ref, g2_ref, sc_ref, sh_ref, w1_ref, w2_ref, g3_ref, gate_ref, o_ref, *, tf):
    x = x_ref[...]
    h = (_rms(x, g2_ref[...]) * (1.0 + sc_ref[0]) + sh_ref[0]).astype(bf16)
    acc = jnp.zeros(x.shape, f32)
    for c in range(D_FF // tf):
        a = jnp.maximum(jnp.dot(h, w1_ref[:, c * tf:(c + 1) * tf], preferred_element_type=f32), 0.0)
        acc = acc + jnp.dot((a * a).astype(bf16), w2_ref[c * tf:(c + 1) * tf, :], preferred_element_type=f32)
    o_ref[...] = x + gate_ref[0] * _rms(acc, g3_ref[...])


def _mlp(x2, g2, sc, sh, w1, w2, g3, gate, *, tm, rows_per_mod):
    m = x2.shape[0]
    mod_rows = sc.shape[1]
    modmap = lambda i: ((i * tm) // rows_per_mod, 0, 0)
    const = lambda s: pl.BlockSpec(s, lambda i: (0, 0))
    mod = pl.BlockSpec((1, mod_rows, D_MODEL), modmap)
    return pl.pallas_call(
        functools.partial(_mlp_kernel, tf=512),
        grid=(m // tm,),
        in_specs=[pl.BlockSpec((tm, D_MODEL), lambda i: (i, 0)), const((1, D_MODEL)), mod, mod,
                  const((D_MODEL, D_FF)), const((D_FF, D_MODEL)), const((1, D_MODEL)), mod],
        out_specs=pl.BlockSpec((tm, D_MODEL), lambda i: (i, 0)),
        out_shape=jax.ShapeDtypeStruct((m, D_MODEL), f32),
        compiler_params=_cp(("parallel",)),
        name="mlp",
    )(x2, g2, sc, sh, w1, w2, g3, gate)


def _pad8(a):
    return jnp.concatenate([a, jnp.zeros((8 - a.shape[0], a.shape[1]), a.dtype)], axis=0)


def _by_group(rows):
    grp = lax.broadcasted_iota(i32, (8, rows.shape[1]), 0) // G_NSA
    return jnp.where(grp == 0, jnp.broadcast_to(rows[0:1], (8, rows.shape[1])),
                     jnp.broadcast_to(rows[1:2], (8, rows.shape[1])))


def _lane_to_sublane(row, offset, stride):
    sub = lax.broadcasted_iota(i32, (8, LANES), 0)
    lane = lax.broadcasted_iota(i32, (8, LANES), 1)
    return jnp.sum(jnp.where(lane == offset + stride * sub, jnp.broadcast_to(row, (8, LANES)), 0.0),
                   axis=1, keepdims=True)


def _dec_step(m, l, acc, s, mask, vb):
    if mask is not None:
        s = jnp.where(mask, s, NEG)
    m_new = jnp.maximum(m, jnp.max(s, axis=1, keepdims=True))
    alpha = jnp.exp(m - m_new)
    p = jnp.exp(s - m_new)
    if mask is not None:
        p = jnp.where(mask, p, 0.0)
    l = alpha * l + jnp.sum(p, axis=1, keepdims=True)
    acc = alpha * acc + jnp.dot(p.astype(bf16), vb, preferred_element_type=f32)
    return m_new, l, acc


def _fox_dec_kernel(pt_ref, fq_ref, misc_ref, bias_ref, knew_ref, *rest, pps):
    lf_refs = rest[:pps]
    kv_refs = rest[pps:2 * pps]
    o_ref, lfo_ref, m_scr, l_scr, acc_scr, c_scr = rest[2 * pps:]
    g = pl.program_id(1)
    n = pps * PAGE_SIZE
    q6f = _pad8(fq_ref[0])
    q6 = q6f.astype(bf16)
    lane = lax.broadcasted_iota(i32, (8, LANES), 1)

    @pl.when(g == 0)
    def _():
        lfrow = jnp.where(lane[0:1] < H_FOX, _log_sigmoid(misc_ref[0] + bias_ref[...]), 0.0)
        lfo_ref[0] = lfrow
        m_scr[...] = jnp.sum(q6f * _pad8(knew_ref[0, 0]), axis=1, keepdims=True)
        l_scr[...] = jnp.ones((8, 1), f32)
        acc_scr[...] = _pad8(knew_ref[0, 1])
        c_scr[...] = _lane_to_sublane(lfrow, 0, 1)

    carry = c_scr[...]
    biases = []
    for i in range(pps):
        lf = lf_refs[i][0, 0]
        suf = lf
        for sft in (1, 2, 4, 8, 16, 32, 64):
            suf = suf + jnp.where(lane + sft < LANES, pltpu.roll(suf, LANES - sft, 1), 0.0)
        biases.append(carry + (suf - lf))
        carry = carry + suf[:, 0:1]
    c_scr[...] = carry
    s = jnp.concatenate(biases, axis=1)
    sub = lax.broadcasted_iota(i32, (8, n), 0)
    for h in range(H_FOX):
        kh = jnp.concatenate([kv_refs[i][0, 0, :, 0, h, :] for i in range(pps)], axis=0).astype(bf16)
        s = jnp.where(sub == h, s + _nt(q6, kh), s)
    m = m_scr[...]
    m_new = jnp.maximum(m, jnp.max(s, axis=1, keepdims=True))
    alpha = jnp.exp(m - m_new)
    p = jnp.exp(s - m_new)
    l_scr[...] = alpha * l_scr[...] + jnp.sum(p, axis=1, keepdims=True)
    m_scr[...] = m_new
    pb = p.astype(bf16)
    sub_d = lax.broadcasted_iota(i32, (8, HEAD_DIM), 0)
    pv = jnp.zeros((8, HEAD_DIM), f32)
    for h in range(H_FOX):
        vh = jnp.concatenate([kv_refs[i][0, 0, :, 1, h, :] for i in range(pps)], axis=0).astype(bf16)
        pv = jnp.where(sub_d == h, jnp.dot(pb, vh, preferred_element_type=f32), pv)
    acc_scr[...] = alpha * acc_scr[...] + pv

    @pl.when(g == pl.num_programs(1) - 1)
    def _():
        o_ref[0] = acc_scr[...] / l_scr[...]


def _fox_decode(layer, page_table, fq, misc, bias_row, fkv_new, lf_t, cache_kv):
    b, n_pages = page_table.shape
    pps = FOX_PPS
    ng = n_pages // pps
    row3 = lambda w_: pl.BlockSpec((1, 1, w_), lambda i, g, pt: (i, 0, 0))

    def page(i, shape):
        zeros = (0,) * len(shape)
        return pl.BlockSpec((1, 1) + shape,
                            lambda bi, g, pt, i=i: (layer, pt[jnp.minimum(bi, b - 1),
                                                               n_pages - 1 - (jnp.minimum(g, ng - 1) * pps + i)]) + zeros)

    grid_spec = pltpu.PrefetchScalarGridSpec(
        num_scalar_prefetch=1,
        grid=(b, ng),
        in_specs=[pl.BlockSpec((1, H_FOX, HEAD_DIM), lambda i, g, pt: (i, 0, 0)), row3(LANES),
                  pl.BlockSpec((1, LANES), lambda i, g, pt: (0, 0)),
                  pl.BlockSpec((1, 2, H_FOX, HEAD_DIM), lambda i, g, pt: (i, 0, 0, 0))]
                 + [page(i, (8, PAGE_SIZE)) for i in range(pps)]
                 + [page(i, (PAGE_SIZE, 2, H_FOX, HEAD_DIM)) for i in range(pps)],
        out_specs=[pl.BlockSpec((1, 8, HEAD_DIM), lambda i, g, pt: (i, 0, 0)), row3(LANES)],
        scratch_shapes=[pltpu.VMEM((8, 1), f32), pltpu.VMEM((8, 1), f32), pltpu.VMEM((8, HEAD_DIM), f32),
                        pltpu.VMEM((8, 1), f32)],
    )
    return pl.pallas_call(
        functools.partial(_fox_dec_kernel, pps=pps),
        grid_spec=grid_spec,
        out_shape=[jax.ShapeDtypeStruct((b, 8, HEAD_DIM), f32), jax.ShapeDtypeStruct((b, 1, LANES), f32)],
        compiler_params=_cp(("parallel", "arbitrary")),
        name="fox_decode",
    )(page_table, fq, misc, bias_row, fkv_new, *([lf_t] * pps), *([cache_kv] * pps))


def _nsa_cmp_dec_kernel(pt_ref, q_ref, *rest, pps, past):
    pg_refs = rest[:pps]
    p_ref, o_ref, mean_scr = rest[pps:]
    g = pl.program_id(1)
    per_page = PAGE_SIZE // CMP_BLOCK
    parts = []
    for i in range(pps):
        blk = pg_refs[i][0, 0]
        parts.append(jnp.sum(blk.reshape(per_page, CMP_BLOCK, 2, KV_NSA, HEAD_DIM), axis=1) * (1.0 / CMP_BLOCK))
    rows = pps * per_page
    mean_scr[pl.ds(g * rows, rows)] = jnp.concatenate(parts, axis=0)

    @pl.when(g == pl.num_programs(1) - 1)
    def _():
        nbc = mean_scr.shape[0]
        q6 = _pad8(q_ref[0]).astype(bf16)
        grp0 = lax.broadcasted_iota(i32, (8, 1), 0) // G_NSA == 0
        s = jnp.where(grp0, _nt(q6, mean_scr[:, 0, 0, :].astype(bf16)),
                      _nt(q6, mean_scr[:, 0, 1, :].astype(bf16)))
        j = lax.broadcasted_iota(i32, (8, nbc), 1)
        mask = ((j + 1) * CMP_BLOCK - 1) <= past
        s = jnp.where(mask, s, NEG)
        pe = jnp.where(mask, jnp.exp(s - jnp.max(s, axis=1, keepdims=True)), 0.0)
        den = jnp.sum(pe, axis=1, keepdims=True)
        p = pe / jnp.where(den > 0.0, den, 1.0)
        p_ref[0] = p
        pb = p.astype(bf16)
        o_ref[0] = jnp.where(grp0, jnp.dot(pb, mean_scr[:, 1, 0, :].astype(bf16), preferred_element_type=f32),
                             jnp.dot(pb, mean_scr[:, 1, 1, :].astype(bf16), preferred_element_type=f32))


def _nsa_cmp_decode(layer, page_table, nq, cache_nsa, past):
    b, n_pages = page_table.shape
    pps = CMP_PPS
    nbc = past // CMP_BLOCK
    grid_spec = pltpu.PrefetchScalarGridSpec(
        num_scalar_prefetch=1,
        grid=(b, n_pages // pps),
        in_specs=[pl.BlockSpec((1, H_NSA, HEAD_DIM), lambda i, g, pt: (i, 0, 0))]
                 + [pl.BlockSpec((1, 1, PAGE_SIZE, 2, KV_NSA, HEAD_DIM),
                                 lambda bi, g, pt, i=i: (layer, pt[jnp.minimum(bi, b - 1),
                                                                   jnp.minimum(g, n_pages // pps - 1) * pps + i],
                                                         0, 0, 0, 0))
                    for i in range(pps)],
        out_specs=[pl.BlockSpec((1, 8, nbc), lambda i, g, pt: (i, 0, 0)),
                   pl.BlockSpec((1, 8, HEAD_DIM), lambda i, g, pt: (i, 0, 0))],
        scratch_shapes=[pltpu.VMEM((nbc, 2, KV_NSA, HEAD_DIM), f32)],
    )
    return pl.pallas_call(
        functools.partial(_nsa_cmp_dec_kernel, pps=pps, past=past),
        grid_spec=grid_spec,
        out_shape=[jax.ShapeDtypeStruct((b, 8, nbc), f32), jax.ShapeDtypeStruct((b, 8, HEAD_DIM), f32)],
        compiler_params=_cp(("parallel", "arbitrary")),
        name="nsa_cmp_decode",
    )(page_table, nq, *([cache_nsa] * pps))


def _nsa_topk_dec_kernel(p_ref, idx_ref, *, past, width):
    b = p_ref.shape[0]
    nbc = p_ref.shape[2]
    nbs = -(-(past + 1) // SEL_BLOCK)
    imps = []
    for kv in range(KV_NSA):
        imps.append(p_ref[:, kv * G_NSA, :] + p_ref[:, kv * G_NSA + 1, :] + p_ref[:, kv * G_NSA + 2, :])
    imp = jnp.concatenate(imps, axis=0)
    imp = jnp.concatenate([imp, jnp.zeros((2 * b, width - nbc), f32)], axis=1)
    j = lax.broadcasted_iota(i32, (2 * b, width), 1)
    cur = past // SEL_BLOCK
    forced = (j == 0) | (j == cur) | (j == cur - 1)
    score = jnp.where(forced, FORCE, imp)
    score = jnp.where(j * SEL_BLOCK <= past, score, -1.0)
    score = jnp.where(j < nbs, score, -2.0)
    _, firsts = _select_topk(score, min(TOPK_BLOCKS, nbs))
    lane = lax.broadcasted_iota(i32, (2 * b, LANES), 1)
    out = jnp.zeros((2 * b, LANES), i32)
    for k, first in enumerate(firsts):
        out = jnp.where(lane == k, first.astype(i32), out)
    idx_ref[...] = out


def _nsa_topk_decode(p8, past):
    b, _, nbc = p8.shape
    width = 2 * nbc
    return pl.pallas_call(
        functools.partial(_nsa_topk_dec_kernel, past=past, width=width),
        grid=(1,),
        in_specs=[pl.BlockSpec(p8.shape, lambda i: (0, 0, 0))],
        out_specs=pl.BlockSpec((2 * b, LANES), lambda i: (0, 0)),
        out_shape=jax.ShapeDtypeStruct((2 * b, LANES), i32),
        compiler_params=_cp(("arbitrary",)),
        name="nsa_topk_decode",
    )(p8)


def _nsa_sel_dec_kernel(pt_ref, idx_ref, q_ref, misc_ref, ocmp_ref, nkv_ref, wnew_ref, win_ref, *rest, past, nsel):
    blk_refs = rest[:KV_NSA * nsel]
    o_ref = rest[KV_NSA * nsel]
    b = pl.program_id(0)
    n_past_blk = past // SEL_BLOCK
    q8f = _pad8(q_ref[0])
    q8 = q8f.astype(bf16)
    sub = lax.broadcasted_iota(i32, (8, 1), 0)
    grp0 = sub // G_NSA == 0
    m = jnp.full((8, 1), NEG, f32)
    l = jnp.zeros((8, 1), f32)
    acc = jnp.zeros((8, HEAD_DIM), f32)
    s_new = jnp.sum(q8f * _by_group(nkv_ref[0, 2]), axis=1, keepdims=True)
    v_new = _by_group(nkv_ref[0, 3])
    for kv in range(KV_NSA):
        in_group = (sub // G_NSA) == kv
        has_new = jnp.zeros((), jnp.bool_)
        for k in range(nsel):
            j = idx_ref[kv * pl.num_programs(0) + b, k]
            has_new = has_new | (j == n_past_blk)
            blk = blk_refs[kv * nsel + k]
            mask = jnp.broadcast_to(in_group & (j < n_past_blk), (8, SEL_BLOCK))
            m, l, acc = _dec_step(m, l, acc, _nt(q8, blk[0, 0, :, 0, kv, :].astype(bf16)), mask,
                                  blk[0, 0, :, 1, kv, :].astype(bf16))
        new_ok = in_group & has_new
        m_new = jnp.maximum(m, jnp.where(new_ok, s_new, NEG))
        alpha = jnp.exp(m - m_new)
        p_new = jnp.where(new_ok, jnp.exp(s_new - m_new), 0.0)
        l = alpha * l + p_new
        acc = alpha * acc + p_new * v_new
        m = m_new
    o_sel = acc / jnp.where(l > 0.0, l, 1.0)
    wb = win_ref.shape[2]
    kpos = past - wb + lax.broadcasted_iota(i32, (8, wb), 1)
    rel = past - kpos
    wmask = (rel >= 0) & (rel < WINDOW) & (kpos >= 0)
    s = jnp.where(grp0, _nt(q8, win_ref[0, 0, :, 0, 0, :].astype(bf16)), _nt(q8, win_ref[0, 0, :, 0, 1, :].astype(bf16)))
    s = jnp.where(wmask, s, NEG)
    s_w = jnp.sum(q8f * _by_group(wnew_ref[0, 0]), axis=1, keepdims=True)
    m = jnp.maximum(jnp.max(s, axis=1, keepdims=True), s_w)
    p = jnp.where(wmask, jnp.exp(s - m), 0.0)
    p_w = jnp.exp(s_w - m)
    pb = p.astype(bf16)
    pv = jnp.where(grp0, jnp.dot(pb, win_ref[0, 0, :, 1, 0, :].astype(bf16), preferred_element_type=f32),
                   jnp.dot(pb, win_ref[0, 0, :, 1, 1, :].astype(bf16), preferred_element_type=f32))
    o_win = (pv + p_w * _by_group(wnew_ref[0, 1])) / (jnp.sum(p, axis=1, keepdims=True) + p_w)
    misc = misc_ref[0]
    gates = [jax.nn.sigmoid(_lane_to_sublane(misc, GATE0 + r, 3)) for r in range(3)]
    o_ref[0] = gates[0] * ocmp_ref[0] + gates[1] * o_sel + gates[2] * o_win


def _nsa_sel_decode(layer, page_table, idx, nq, misc, ocmp, nkv_new, win_new, cache_win, cache_nsa, past):
    b = page_table.shape[0]
    nsel = idx.shape[1]
    n_past_blk = past // SEL_BLOCK
    per_page = PAGE_SIZE // SEL_BLOCK
    wb = cache_win.shape[2]

    def blk_spec(kv, k):
        def imap(i, pt, ix):
            ii = jnp.minimum(i, b - 1)
            jp = jnp.clip(ix[kv * b + ii, k], 0, n_past_blk - 1)
            return (layer, pt[ii, jp // per_page], jp % per_page, 1, 0, 0)
        return pl.BlockSpec((1, 1, SEL_BLOCK, 2, KV_NSA, HEAD_DIM), imap)

    grid_spec = pltpu.PrefetchScalarGridSpec(
        num_scalar_prefetch=2,
        grid=(b,),
        in_specs=[pl.BlockSpec((1, H_NSA, HEAD_DIM), lambda i, pt, ix: (i, 0, 0)),
                  pl.BlockSpec((1, 1, LANES), lambda i, pt, ix: (i, 0, 0)),
                  pl.BlockSpec((1, 8, HEAD_DIM), lambda i, pt, ix: (i, 0, 0)),
                  pl.BlockSpec((1, 4, KV_NSA, HEAD_DIM), lambda i, pt, ix: (i, 0, 0, 0)),
                  pl.BlockSpec((1, 2, KV_NSA, HEAD_DIM), lambda i, pt, ix: (i, 0, 0, 0)),
                  pl.BlockSpec((1, 1, wb, 2, KV_NSA, HEAD_DIM), lambda i, pt, ix: (layer, i, 0, 0, 0, 0))]
                 + [blk_spec(kv, k) for kv in range(KV_NSA) for k in range(nsel)],
        out_specs=pl.BlockSpec((1, 8, HEAD_DIM), lambda i, pt, ix: (i, 0, 0)),
    )
    return pl.pallas_call(
        functools.partial(_nsa_sel_dec_kernel, past=past, nsel=nsel),
        grid_spec=grid_spec,
        out_shape=jax.ShapeDtypeStruct((b, 8, HEAD_DIM), f32),
        compiler_params=_cp(("parallel",)),
        name="nsa_sel_decode",
    )(page_table, idx, nq, misc, ocmp, nkv_new, win_new, cache_win, *([cache_nsa] * (KV_NSA * nsel)))


def _proj_columns():
    off = np.cumsum([0, 384, 384, 384, 6, 384, 128, 128, 128, 128, 128, 128, 18, 256])
    fq, fk, fv, ff, nq, nkc, nvc, nks, nvs, nkw, nvw, ng, u = [int(o) for o in off[:13]]
    cols = list(range(fq, fq + 384))
    for h in NSA_PAIR_ORDER:
        cols += list(range(nq + h * HEAD_DIM, nq + (h + 1) * HEAD_DIM))
    cols += list(range(fk, fk + 768))
    cols += list(range(nkc, nkc + 512))
    cols += list(range(nkw, nkw + 256))
    cols += list(range(u, u + 256))
    cols += list(range(ff, ff + 6)) + list(range(ng, ng + 18)) + [N_IN] * (LANES - 24)
    return np.asarray(cols, np.int32)


def _out_rows():
    rows = list(range(0, 384))
    for h in NSA_PAIR_ORDER:
        rows += list(range(384 + h * HEAD_DIM, 384 + (h + 1) * HEAD_DIM))
    rows += list(range(768, 1024))
    return np.asarray(rows, np.int32)


def _rope_tables(pos):
    half = ROT_DIM // 2
    inv = ROPE_THETA ** (-jnp.arange(0, ROT_DIM, 2, dtype=f32) / ROT_DIM)
    ang = pos.astype(f32)[:, None] * inv[None, :]
    cos, sin = jnp.cos(ang), jnp.sin(ang)
    n = pos.shape[0]
    one = jnp.ones((n, HEAD_DIM - ROT_DIM), f32)
    zero8 = jnp.zeros((n, half), f32)
    zrest = jnp.zeros((n, HEAD_DIM - ROT_DIM), f32)
    c = jnp.concatenate([cos, cos, one], axis=1)
    sa = jnp.concatenate([-sin, zero8, zrest], axis=1)
    sb = jnp.concatenate([zero8, sin, zrest], axis=1)
    return tuple(jnp.concatenate([a, a], axis=1) for a in (c, sa, sb))


def kernel(x_prompt, x_sample, cache_fox_kv, cache_fox_logf, cache_nsa_kv, cache_nsa_win, state_pool, page_table,
           c_prompt, c_sample, w_ada, b_ada, norm_g, w_in, b_fox_f, w_out, w_pool, pool_scale, w_ff1, w_ff2):
    depth = w_in.shape[0]
    bp, t, _ = x_prompt.shape
    bs = x_sample.shape[0]
    past = page_table.shape[1] * PAGE_SIZE
    mp = bp * t

    cols = _proj_columns()
    w_in_p = jnp.concatenate([w_in, jnp.zeros((depth, D_MODEL, 1), f32)], axis=2)[:, :, cols].astype(bf16)
    w_out_p = w_out[:, _out_rows(), :].astype(bf16)
    w_pool_bd = jnp.zeros((depth, C_POOL, C_POOL), f32)
    for g in range(len(POOL_WINDOWS)):
        w_pool_bd = w_pool_bd.at[:, g * POOL_GW:(g + 1) * POOL_GW, g * POOL_GW:(g + 1) * POOL_GW].set(w_pool[:, g])
    w_pool_bd = w_pool_bd.astype(bf16)
    w1 = w_ff1.astype(bf16)
    w2 = w_ff2.astype(bf16)
    bias_rows = jnp.pad(b_fox_f, ((0, 0), (0, LANES - H_FOX))).reshape(depth, 1, LANES)

    rope_p = _rope_tables(jnp.arange(t))
    rope_s = _rope_tables(jnp.full((bs,), past, i32))

    lf_t = jnp.pad(jnp.swapaxes(cache_fox_logf, 2, 3), ((0, 0), (0, 0), (0, 8 - H_FOX), (0, 0)))
    nat_from_pair = np.argsort(np.asarray(NSA_PAIR_ORDER))

    mod = _ada(jnp.concatenate([c_prompt, c_sample], axis=0), w_ada, b_ada)
    mod = mod.reshape(depth, bp + bs, 6, D_MODEL)

    yp = x_prompt.reshape(mp, D_MODEL)
    ys = x_sample.reshape(bs, D_MODEL)
    sp, ss = [], []
    for l in range(depth):
        g = norm_g[l].reshape(4, 1, D_MODEL)
        modp = [mod[l, :bp, k].reshape(bp, 1, D_MODEL) for k in range(6)]
        mods = [mod[l, bp:, k].reshape(1, bs, D_MODEL) for k in range(6)]

        (fq, nq, fkv, nkv, win, u, misc, fkvb, nkvb, winb, kcm) = _inproj(
            yp, g[0], modp[1], modp[0], w_in_p[l], *rope_p, tm=512, rows_per_mod=t, rope_rows=t, with_means=True)
        logf, ccol, crow = _foxprep(misc, bias_rows[l], bp, t)
        o_fox = _fox_attn(fq, fkvb, ccol, crow, bp, t)
        o_nsa = _nsa_attn(nq, kcm, nkvb, winb, misc, bp, t)
        y1 = _outproj(yp, o_fox, o_nsa, u, w_out_p[l], w_pool_bd[l], pool_scale[l].reshape(1, C_POOL), g[1],
                      modp[2], tm=512, t=t)
        yp = _mlp(y1, g[2], modp[4], modp[3], w1[l], w2[l], g[3], modp[5], tm=512, rows_per_mod=t)
        wl = min(WINDOW, t)
        sp.append((fkv.reshape(bp, t, 2, H_FOX, HEAD_DIM), logf.reshape(bp, t, H_FOX),
                   nkv.reshape(bp, t, 4, KV_NSA, HEAD_DIM),
                   win.reshape(bp, t, 2, KV_NSA, HEAD_DIM)[:, t - wl:],
                   u.reshape(bp, t, C_POOL)[:, t - POOL_STATE:]))

        (fq_s, nq_s, fkv_s, nkv_s, win_s, u_s, misc_s, _, _, _, _) = _inproj(
            ys, g[0], mods[1], mods[0], w_in_p[l], *rope_s, tm=bs, rows_per_mod=bs, rope_rows=bs, with_means=False)
        r3 = lambda a: a.reshape(bs, 1, a.shape[-1])
        fq6 = fq_s.astype(f32).reshape(bs, H_FOX, HEAD_DIM)
        nq6 = nq_s.astype(f32).reshape(bs, H_NSA, HEAD_DIM)[:, nat_from_pair]
        nkv4 = nkv_s.reshape(bs, 4, KV_NSA, HEAD_DIM)
        win2 = win_s.reshape(bs, 2, KV_NSA, HEAD_DIM)
        o_fox_s, logf_s = _fox_decode(l, page_table, fq6, r3(misc_s), bias_rows[l],
                                      fkv_s.reshape(bs, 2, H_FOX, HEAD_DIM), lf_t, cache_fox_kv)
        p8, o_cmp = _nsa_cmp_decode(l, page_table, nq6, cache_nsa_kv, past)
        idx = _nsa_topk_decode(p8, past)[:, :min(TOPK_BLOCKS, -(-(past + 1) // SEL_BLOCK))]
        o_nsa_s = _nsa_sel_decode(l, page_table, idx, nq6, r3(misc_s), o_cmp, nkv4, win2,
                                  cache_nsa_win, cache_nsa_kv, past)
        o_fox_s = o_fox_s[:, :H_FOX].reshape(bs, 384).astype(bf16)
        o_nsa_s = o_nsa_s[:, np.asarray(NSA_PAIR_ORDER)].reshape(bs, 384).astype(bf16)
        u_ext = jnp.concatenate([state_pool[l], u_s.reshape(bs, 1, C_POOL)], axis=1)
        y1s = _outproj_dec(ys, o_fox_s, o_nsa_s, u_ext, w_out_p[l],
                           w_pool_bd[l], pool_scale[l].reshape(1, C_POOL), g[1], mods[2][0], past)
        ys = _mlp(y1s, g[2], mods[4], mods[3], w1[l], w2[l], g[3], mods[5], tm=bs, rows_per_mod=bs)
        win_all = jnp.concatenate([cache_nsa_win[l], win2[:, None]], axis=1)
        ss.append((fkv_s.reshape(bs, 1, 2, H_FOX, HEAD_DIM), logf_s[:, :, :H_FOX], nkv4[:, None],
                   win_all[:, 1:], u_ext[:, 1:]))

    stk = lambda lst, i: jnp.stack([s[i] for s in lst], axis=0)
    return (yp.reshape(bp, t, D_MODEL), ys.reshape(bs, 1, D_MODEL),
            stk(sp, 0), stk(sp, 1), stk(sp, 2), stk(sp, 3), stk(sp, 4),
            stk(ss, 0), stk(ss, 1), stk(ss, 2), stk(ss, 3), stk(ss, 4))
```

```python
import functools

import numpy as np
import jax
import jax.numpy as jnp
from jax import lax
from jax.experimental import pallas as pl
from jax.experimental.pallas import tpu as pltpu

f32 = jnp.float32
bf16 = jnp.bfloat16
i32 = jnp.int32

D_MODEL = 1024
HEAD_DIM = 64
H_FOX = 6
H_NSA = 6
KV_NSA = 2
G_NSA = H_NSA // KV_NSA
POOL_WINDOWS = (2, 4, 8, 16)
C_POOL = 256
POOL_GW = 64
POOL_STATE = 15
ROT_DIM = 16
ROPE_THETA = 500000.0
CMP_BLOCK = 64
SEL_BLOCK = 64
TOPK_BLOCKS = 8
WINDOW = 512
PAGE_SIZE = 128
D_FF = 4 * D_MODEL
EPS = 1e-6
NEG = -1e30
FORCE = 1e4
SCALE = HEAD_DIM ** -0.5
N_IN = 2584

LANES = 128
HALF = LANES // 2
VMEM_LIMIT = 56 * 1024 * 1024

C_FQ = 0
C_NQ = 384
C_FKV = 768
C_NKV = 1536
C_WIN = 2048
C_U = 2304
C_MISC = 2560
N_PROJ = 2688
GATE0 = H_FOX
NSA_PAIR_ORDER = (0, 3, 1, 4, 2, 5)

FOX_TQ = 256
FOX_TK = 256
NSA_TQ = 128
NSA_TK = 256
FOX_PPS = 16
CMP_PPS = 16


def _cp(sem):
    return pltpu.CompilerParams(dimension_semantics=sem, vmem_limit_bytes=VMEM_LIMIT)


def _nt(a, b):
    return lax.dot_general(a, b, (((1,), (1,)), ((), ())), preferred_element_type=f32)


def _rms(x, g):
    return x * lax.rsqrt(jnp.mean(x * x, axis=-1, keepdims=True) + EPS) * g


def _log_sigmoid(x):
    return jnp.minimum(x, 0.0) - jnp.log1p(jnp.exp(-jnp.abs(x)))


def _ada_kernel(c_ref, w_ref, b_ref, o_ref):
    c = c_ref[...]
    a = (c * jax.nn.sigmoid(c)).astype(bf16)
    o_ref[0] = jnp.dot(a, w_ref[0].astype(bf16), preferred_element_type=f32) + b_ref[0]


def _ada(c_all, w_ada, b_ada):
    depth = w_ada.shape[0]
    n = c_all.shape[0]
    tn = 1024
    return pl.pallas_call(
        _ada_kernel,
        grid=(depth, 6 * D_MODEL // tn),
        in_specs=[pl.BlockSpec((n, D_MODEL), lambda l, j: (0, 0)),
                  pl.BlockSpec((1, D_MODEL, tn), lambda l, j: (l, 0, j)),
                  pl.BlockSpec((1, 1, tn), lambda l, j: (l, 0, j))],
        out_specs=pl.BlockSpec((1, n, tn), lambda l, j: (l, 0, j)),
        out_shape=jax.ShapeDtypeStruct((depth, n, 6 * D_MODEL), f32),
        compiler_params=_cp(("parallel", "parallel")),
        name="ada_mod",
    )(c_all, w_ada, b_ada.reshape(depth, 1, 6 * D_MODEL))


def _inproj_kernel(x_ref, g_ref, sc_ref, sh_ref, w_ref, cos_ref, sa_ref, sb_ref,
                   fq_ref, nq_ref, fkv_ref, nkv_ref, win_ref, u_ref, misc_ref,
                   fkvb_ref, nkvb_ref, winb_ref, kcm_ref, *, tm, with_means):
    x = x_ref[...]
    h = (_rms(x, g_ref[...]) * (1.0 + sc_ref[0]) + sh_ref[0]).astype(bf16)
    cos = cos_ref[...]
    sa = sa_ref[...]
    sb = sb_ref[...]

    def rope(z):
        return z * cos + pltpu.roll(z, LANES - ROT_DIM // 2, 1) * sa + pltpu.roll(z, ROT_DIM // 2, 1) * sb

    def means(z):
        return jnp.sum(z.reshape(tm // CMP_BLOCK, CMP_BLOCK, LANES), axis=1) * (1.0 / CMP_BLOCK)

    def emit(blk, z):
        c = blk * LANES
        if c < C_NQ:
            fq_ref[:, c - C_FQ:c - C_FQ + LANES] = (z * SCALE).astype(bf16)
        elif c < C_FKV:
            nq_ref[:, c - C_NQ:c - C_NQ + LANES] = (rope(z) * SCALE).astype(bf16)
        elif c < C_NKV:
            fkv_ref[:, c - C_FKV:c - C_FKV + LANES] = z
            fkvb_ref[:, c - C_FKV:c - C_FKV + LANES] = z.astype(bf16)
        elif c < C_WIN:
            o = c - C_NKV
            if o in (0, 2 * LANES):
                z = rope(z)
            nkv_ref[:, o:o + LANES] = z
            nkvb_ref[:, o:o + LANES] = z.astype(bf16)
            if with_means and o < 2 * LANES:
                kcm_ref[:, o:o + LANES] = means(z)
        elif c < C_U:
            o = c - C_WIN
            if o == 0:
                z = rope(z)
            win_ref[:, o:o + LANES] = z
            winb_ref[:, o:o + LANES] = z.astype(bf16)
        elif c < C_MISC:
            u_ref[:, c - C_U:c - C_U + LANES] = z
        else:
            misc_ref[...] = z

    nblk = N_PROJ // LANES
    for b0 in range(0, nblk, 2):
        nb = min(2, nblk - b0)
        z2 = jnp.dot(h, w_ref[:, b0 * LANES:(b0 + nb) * LANES], preferred_element_type=f32)
        for k in range(nb):
            emit(b0 + k, z2[:, k * LANES:(k + 1) * LANES])
    if not with_means:
        kcm_ref[...] = jnp.zeros(kcm_ref.shape, f32)


def _inproj(x2, g0, sc, sh, w, cos, sa, sb, *, tm, rows_per_mod, rope_rows, with_means):
    m = x2.shape[0]
    nt = m // tm
    mod_rows = sc.shape[1]
    rope_blocks = rope_rows // tm
    kc_rows = max(tm // CMP_BLOCK, 8)
    modmap = lambda i: ((i * tm) // rows_per_mod, 0, 0)
    ropemap = lambda i: (i % rope_blocks, 0)
    row = lambda w_: pl.BlockSpec((tm, w_), lambda i: (i, 0))
    outs = [((m, 384), bf16), ((m, 384), bf16), ((m, 768), f32), ((m, 512), f32), ((m, 256), f32),
            ((m, 256), f32), ((m, LANES), f32), ((m, 768), bf16), ((m, 512), bf16), ((m, 256), bf16)]
    out_shape = [jax.ShapeDtypeStruct(s, d) for s, d in outs] + [jax.ShapeDtypeStruct((nt * kc_rows, 256), f32)]
    out_specs = [row(s[1]) for s, _ in outs] + [pl.BlockSpec((kc_rows, 256), lambda i: (i, 0))]
    return pl.pallas_call(
        functools.partial(_inproj_kernel, tm=tm, with_means=with_means),
        grid=(nt,),
        in_specs=[row(D_MODEL),
                  pl.BlockSpec((1, D_MODEL), lambda i: (0, 0)),
                  pl.BlockSpec((1, mod_rows, D_MODEL), modmap),
                  pl.BlockSpec((1, mod_rows, D_MODEL), modmap),
                  pl.BlockSpec((D_MODEL, N_PROJ), lambda i: (0, 0)),
                  pl.BlockSpec((tm, LANES), ropemap),
                  pl.BlockSpec((tm, LANES), ropemap),
                  pl.BlockSpec((tm, LANES), ropemap)],
        out_specs=out_specs,
        out_shape=out_shape,
        compiler_params=_cp(("parallel",)),
        name="in_proj",
    )(x2, g0, sc, sh, w, cos, sa, sb)


def _foxprep_kernel(misc_ref, bias_ref, lf_ref, ccol_ref, crow_ref, *, t):
    blk = 256
    lane = lax.broadcasted_iota(i32, (t, LANES), 1)
    lf = jnp.where(lane < H_FOX, _log_sigmoid(misc_ref[...] + bias_ref[...]), 0.0)
    lf_ref[...] = lf[:, :H_FOX]
    r = lax.broadcasted_iota(i32, (blk, blk), 0)
    c = lax.broadcasted_iota(i32, (blk, blk), 1)
    ltri = (r >= c).astype(f32)
    carry = jnp.zeros((1, LANES), f32)
    for b in range(t // blk):
        cb = jnp.dot(ltri, lf[b * blk:(b + 1) * blk], preferred_element_type=f32,
                     precision=lax.Precision.HIGHEST) + carry
        ccol_ref[b * blk:(b + 1) * blk, :] = cb
        carry = cb[blk - 1:blk, :]
    er = lax.broadcasted_iota(i32, (8, LANES), 0)
    ec = lax.broadcasted_iota(i32, (8, LANES), 1)
    eye = (er == ec).astype(f32)
    crow_ref[0] = lax.dot_general(eye, ccol_ref[...], (((1,), (1,)), ((), ())),
                                  preferred_element_type=f32, precision=lax.Precision.HIGHEST)


def _foxprep(misc, bias_row, b, t):
    return pl.pallas_call(
        functools.partial(_foxprep_kernel, t=t),
        grid=(b,),
        in_specs=[pl.BlockSpec((t, LANES), lambda i: (i, 0)),
                  pl.BlockSpec((1, LANES), lambda i: (0, 0))],
        out_specs=[pl.BlockSpec((t, H_FOX), lambda i: (i, 0)),
                   pl.BlockSpec((t, LANES), lambda i: (i, 0)),
                   pl.BlockSpec((1, 8, t), lambda i: (i, 0, 0))],
        out_shape=[jax.ShapeDtypeStruct((b * t, H_FOX), f32),
                   jax.ShapeDtypeStruct((b * t, LANES), f32),
                   jax.ShapeDtypeStruct((b, 8, t), f32)],
        compiler_params=_cp(("parallel",)),
        name="fox_prep",
    )(misc, bias_row)


def _softmax_step(carry, s, mask, vb):
    m, l, acc = carry
    s = jnp.where(mask, s, NEG)
    m_new = jnp.maximum(m, jnp.max(s, axis=1, keepdims=True))
    alpha = jnp.exp(m - m_new)
    p = jnp.where(mask, jnp.exp(s - m_new), 0.0)
    l = alpha * l + jnp.sum(p, axis=1, keepdims=True)
    acc = alpha * acc + jnp.dot(p.astype(bf16), vb, preferred_element_type=f32)
    return m_new, l, acc


def _fox_kernel(q_ref, kv_ref, ccol_ref, crow_ref, o_ref, *, tq, tk):
    qi = pl.program_id(1)
    lo = lax.broadcasted_iota(i32, (tq, LANES), 1) < HALF
    rows = lax.broadcasted_iota(i32, (tq, tk), 0) + qi * tq
    cols = lax.broadcasted_iota(i32, (tq, tk), 1)
    zero = jnp.zeros((), bf16)
    for p in range(H_FOX // 2):
        q = q_ref[:, p * LANES:(p + 1) * LANES]
        qst = jnp.concatenate([jnp.where(lo, q, zero), jnp.where(lo, zero, q)], axis=0)
        cq0 = ccol_ref[:, 2 * p:2 * p + 1]
        cq1 = ccol_ref[:, 2 * p + 1:2 * p + 2]

        def body(c, carry, p=p, qst=qst, cq0=cq0, cq1=cq1):
            k0 = pl.multiple_of(c * tk, tk)
            kb = kv_ref[pl.ds(k0, tk), p * LANES:(p + 1) * LANES]
            vb = kv_ref[pl.ds(k0, tk), 384 + p * LANES:384 + (p + 1) * LANES]
            ck0 = crow_ref[0, 2 * p:2 * p + 1, pl.ds(k0, tk)]
            ck1 = crow_ref[0, 2 * p + 1:2 * p + 2, pl.ds(k0, tk)]
            s = _nt(qst, kb) + jnp.concatenate([cq0 - ck0, cq1 - ck1], axis=0)
            mask = (cols + k0) <= rows
            return _softmax_step(carry, s, jnp.concatenate([mask, mask], axis=0), vb)

        init = (jnp.full((2 * tq, 1), NEG, f32), jnp.zeros((2 * tq, 1), f32), jnp.zeros((2 * tq, LANES), f32))
        nchunks = (qi * tq + tq + tk - 1) // tk
        _, l, acc = lax.fori_loop(0, nchunks, body, init)
        o = acc / l
        o_ref[:, p * LANES:(p + 1) * LANES] = jnp.where(lo, o[:tq], o[tq:]).astype(bf16)


def _fox_attn(fq, fkvb, ccol, crow, b, t):
    tq, tk = FOX_TQ, FOX_TK
    nq = t // tq
    return pl.pallas_call(
        functools.partial(_fox_kernel, tq=tq, tk=tk),
        grid=(b, nq),
        in_specs=[pl.BlockSpec((tq, 384), lambda i, j: (i * nq + j, 0)),
                  pl.BlockSpec((t, 768), lambda i, j: (i, 0)),
                  pl.BlockSpec((tq, LANES), lambda i, j: (i * nq + j, 0)),
                  pl.BlockSpec((1, 8, t), lambda i, j: (i, 0, 0))],
        out_specs=pl.BlockSpec((tq, 384), lambda i, j: (i * nq + j, 0)),
        out_shape=jax.ShapeDtypeStruct((b * t, 384), bf16),
        compiler_params=_cp(("parallel", "parallel")),
        name="fox_attn",
    )(fq, fkvb, ccol, crow)


def _select_topk(score, nsel):
    nb = score.shape[1]
    jf = lax.broadcasted_iota(i32, score.shape, 1).astype(f32)
    sel = jnp.zeros(score.shape, f32)
    firsts = []
    for _ in range(nsel):
        mx = jnp.max(score, axis=1, keepdims=True)
        first = jnp.min(jnp.where(score == mx, jf, float(nb)), axis=1, keepdims=True)
        hit = jf == first
        sel = jnp.where(hit, 1.0, sel)
        score = jnp.where(hit, -3.0, score)
        firsts.append(first)
    return sel, firsts


def _nsa_kernel(q_ref, kcm_ref, nkv_ref, win_ref, misc_ref, o_ref, *, tq, tk, t):
    qi = pl.program_id(1)
    s0 = qi * tq
    nbc = t // CMP_BLOCK
    lane = lax.broadcasted_iota(i32, (tq, LANES), 1)
    lo = lane < HALF
    zero = jnp.zeros((), bf16)
    rows = lax.broadcasted_iota(i32, (tq, tk), 0) + s0
    cols = lax.broadcasted_iota(i32, (tq, tk), 1)
    qpos_b = lax.broadcasted_iota(i32, (tq, LANES), 0) + s0
    jb = lax.broadcasted_iota(i32, (tq, LANES), 1)
    misc = misc_ref[...]

    def tile3(a):
        return jnp.concatenate([a, a, a], axis=0)

    def attend(qk, kref, klane, vlane, c_lo, c_hi, maskfn):
        def body(c, carry):
            k0 = pl.multiple_of(c * tk, tk)
            kb = kref[pl.ds(k0, tk), klane:klane + LANES]
            vb = kref[pl.ds(k0, tk), vlane:vlane + LANES]
            return _softmax_step(carry, _nt(qk, kb), tile3(maskfn(k0)), vb)
        init = (jnp.full((3 * tq, 1), NEG, f32), jnp.zeros((3 * tq, 1), f32), jnp.zeros((3 * tq, LANES), f32))
        _, l, acc = lax.fori_loop(c_lo, c_hi, body, init)
        return acc / l

    kcm = jnp.concatenate([kcm_ref[...], jnp.zeros((LANES - nbc, 2 * LANES), f32)], axis=0)
    kc = kcm[:, :LANES].astype(bf16)
    vc = kcm[:, LANES:].astype(bf16)
    outs = []
    for kv in range(KV_NSA):
        keep = lo if kv == 0 else jnp.logical_not(lo)
        qk = jnp.concatenate([jnp.where(keep, q_ref[:, p * LANES:(p + 1) * LANES], zero)
                              for p in range(G_NSA)], axis=0)
        cmask = (((jb + 1) * CMP_BLOCK - 1) <= qpos_b) & (jb < nbc)
        cmask3 = tile3(cmask)
        sc = jnp.where(cmask3, _nt(qk, kc), NEG)
        mx = jnp.max(sc, axis=1, keepdims=True)
        pe = jnp.where(cmask3, jnp.exp(sc - mx), 0.0)
        den = jnp.sum(pe, axis=1, keepdims=True)
        pc = pe / jnp.where(den > 0.0, den, 1.0)
        o_cmp = jnp.dot(pc.astype(bf16), vc, preferred_element_type=f32)
        imp = pc[:tq] + pc[tq:2 * tq] + pc[2 * tq:]
        cur = qpos_b // SEL_BLOCK
        forced = (jb == 0) | (jb == cur) | (jb == cur - 1)
        score = jnp.where(forced, FORCE, imp)
        score = jnp.where(jb * SEL_BLOCK <= qpos_b, score, -1.0)
        score = jnp.where(jb < nbc, score, -2.0)
        sel, _ = _select_topk(score, min(TOPK_BLOCKS, nbc))
        selb = sel.astype(bf16)

        def sel_mask(k0, selb=selb):
            er = lax.broadcasted_iota(i32, (LANES, tk), 0)
            ec = (lax.broadcasted_iota(i32, (LANES, tk), 1) + k0) // SEL_BLOCK
            expand = jnp.where(er == ec, 1.0, 0.0).astype(bf16)
            picked = jnp.dot(selb, expand, preferred_element_type=f32) > 0.5
            return picked & ((cols + k0) <= rows)

        def win_mask(k0):
            rel = rows - (cols + k0)
            return (rel >= 0) & (rel < WINDOW)

        o_sel = attend(qk, nkv_ref, 2 * LANES, 3 * LANES, 0, (s0 + tq + tk - 1) // tk, sel_mask)
        w_lo = jnp.maximum(s0 - WINDOW + 1, 0) // tk
        o_win = attend(qk, win_ref, 0, LANES, w_lo, (s0 + tq + tk - 1) // tk, win_mask)
        for g in range(G_NSA):
            h = kv * G_NSA + g
            gate = jax.nn.sigmoid(misc[:, GATE0 + 3 * h:GATE0 + 3 * h + 3])
            sl = slice(g * tq, (g + 1) * tq)
            outs.append(gate[:, 0:1] * o_cmp[sl] + gate[:, 1:2] * o_sel[sl] + gate[:, 2:3] * o_win[sl])
    for p in range(G_NSA):
        o_ref[:, p * LANES:(p + 1) * LANES] = jnp.where(lo, outs[p], outs[G_NSA + p]).astype(bf16)


def _nsa_attn(nq, kcm, nkvb, winb, misc, b, t):
    tq, tk = NSA_TQ, NSA_TK
    nq_t = t // tq
    nbc = t // CMP_BLOCK
    return pl.pallas_call(
        functools.partial(_nsa_kernel, tq=tq, tk=tk, t=t),
        grid=(b, nq_t),
        in_specs=[pl.BlockSpec((tq, 384), lambda i, j: (i * nq_t + j, 0)),
                  pl.BlockSpec((nbc, 256), lambda i, j: (i, 0)),
                  pl.BlockSpec((t, 512), lambda i, j: (i, 0)),
                  pl.BlockSpec((t, 256), lambda i, j: (i, 0)),
                  pl.BlockSpec((tq, LANES), lambda i, j: (i * nq_t + j, 0))],
        out_specs=pl.BlockSpec((tq, 384), lambda i, j: (i * nq_t + j, 0)),
        out_shape=jax.ShapeDtypeStruct((b * t, 384), bf16),
        compiler_params=_cp(("parallel", "parallel")),
        name="nsa_attn",
    )(nq, kcm, nkvb, winb, misc)


def _pool_project(d, wp_ref, ps_ref):
    return (jnp.dot(d.astype(bf16), wp_ref[...], preferred_element_type=f32) * ps_ref[...]).astype(bf16)


def _out_tail(x, ofox, onsa, opool, wo_ref, g_ref, gate):
    o = (jnp.dot(ofox, wo_ref[0:384, :], preferred_element_type=f32)
         + jnp.dot(onsa, wo_ref[384:768, :], preferred_element_type=f32)
         + jnp.dot(opool, wo_ref[768:1024, :], preferred_element_type=f32))
    return x + gate * _rms(o, g_ref[...])


def _outproj_kernel(x_ref, ofox_ref, onsa_ref, u_ref, halo_ref, wo_ref, wp_ref, ps_ref, g_ref, gate_ref,
                    o_ref, ext_ref, *, tm, t):
    i = pl.program_id(0)
    pos0 = (i * tm) % t
    u = u_ref[...]
    ext_ref[0:16, :] = jnp.where(pos0 > 0, halo_ref[...], 0.0)
    ext_ref[16:, :] = u
    pos = lax.broadcasted_iota(i32, (tm, LANES), 0) + pos0
    lo = lax.broadcasted_iota(i32, (tm, LANES), 1) < HALF

    def shifted(k, c):
        return ext_ref[16 - k:16 - k + tm, c * LANES:(c + 1) * LANES]

    def cnt(w):
        return jnp.minimum(w, pos + 1).astype(f32)

    ds = []
    for c, (wa, wb) in enumerate(((POOL_WINDOWS[0], POOL_WINDOWS[1]), (POOL_WINDOWS[2], POOL_WINDOWS[3]))):
        run = shifted(0, c)
        sums = {}
        for k in range(1, wb):
            run = run + shifted(k, c)
            if k + 1 in (wa, wb):
                sums[k + 1] = run
        ds.append(jnp.where(lo, sums[wa] / cnt(wa), sums[wb] / cnt(wb)) - u[:, c * LANES:(c + 1) * LANES])
    opool = _pool_project(jnp.concatenate(ds, axis=1), wp_ref, ps_ref)
    o_ref[...] = _out_tail(x_ref[...], ofox_ref[...], onsa_ref[...], opool, wo_ref, g_ref, gate_ref[0])


def _outproj(x2, ofox, onsa, u, wo, wp, ps, g1, gate, *, tm, t):
    m = x2.shape[0]
    row = lambda w_: pl.BlockSpec((tm, w_), lambda i: (i, 0))
    const = lambda s: pl.BlockSpec(s, lambda i: (0, 0))
    return pl.pallas_call(
        functools.partial(_outproj_kernel, tm=tm, t=t),
        grid=(m // tm,),
        in_specs=[row(D_MODEL), row(384), row(384), row(256),
                  pl.BlockSpec((16, 256), lambda i: (jnp.maximum(i * (tm // 16) - 1, 0), 0)),
                  const((D_MODEL, D_MODEL)), const((256, 256)), const((1, 256)), const((1, D_MODEL)),
                  pl.BlockSpec((1, 1, D_MODEL), lambda i: ((i * tm) // t, 0, 0))],
        out_specs=row(D_MODEL),
        out_shape=jax.ShapeDtypeStruct((m, D_MODEL), f32),
        scratch_shapes=[pltpu.VMEM((tm + 16, 256), f32)],
        compiler_params=_cp(("parallel",)),
        name="out_proj",
    )(x2, ofox, onsa, u, u, wo, wp, ps, g1, gate)


def _outproj_dec_kernel(x_ref, ofox_ref, onsa_ref, ext_ref, wo_ref, wp_ref, ps_ref, g_ref, gate_ref, o_ref, *, past):
    ext = ext_ref[...]
    n = ext.shape[1]
    r = lax.broadcasted_iota(i32, ext.shape, 1)
    lane = lax.broadcasted_iota(i32, (ext.shape[0], C_POOL), 1)
    u_new = jnp.sum(jnp.where(r == n - 1, ext, 0.0), axis=1)
    d = jnp.zeros(u_new.shape, f32)
    for g, w in enumerate(POOL_WINDOWS):
        sw = jnp.sum(jnp.where(r >= n - w, ext, 0.0), axis=1)
        d = jnp.where(lane // POOL_GW == g, sw / float(min(w, past + 1)), d)
    opool = _pool_project(d - u_new, wp_ref, ps_ref)
    o_ref[...] = _out_tail(x_ref[...], ofox_ref[...], onsa_ref[...], opool, wo_ref, g_ref, gate_ref[...])


def _outproj_dec(x2, ofox, onsa, ext, wo, wp, ps, g1, gate, past):
    m = x2.shape[0]
    full = lambda a: pl.BlockSpec(a.shape, lambda i: (0,) * a.ndim)
    args = (x2, ofox, onsa, ext, wo, wp, ps, g1, gate)
    return pl.pallas_call(
        functools.partial(_outproj_dec_kernel, past=past),
        grid=(1,),
        in_specs=[full(a) for a in args],
        out_specs=pl.BlockSpec((m, D_MODEL), lambda i: (0, 0)),
        out_shape=jax.ShapeDtypeStruct((m, D_MODEL), f32),
        compiler_params=_cp(("arbitrary",)),
        name="out_proj_dec",
    )(*args)


def _mlp_kernel(x_ref, g2_ref, sc_ref, sh_ref, w1_ref, w2_ref, g3_ref, gate_ref, o_ref, *, tf):
    x = x_ref[...]
    h = (_rms(x, g2_ref[...]) * (1.0 + sc_ref[0]) + sh_ref[0]).astype(bf16)
    acc = jnp.zeros(x.shape, f32)
    for c in range(D_FF // tf):
        a = jnp.maximum(jnp.dot(h, w1_ref[:, c * tf:(c + 1) * tf], preferred_element_type=f32), 0.0)
        acc = acc + jnp.dot((a * a).astype(bf16), w2_ref[c * tf:(c + 1) * tf, :], preferred_element_type=f32)
    o_ref[...] = x + gate_ref[0] * _rms(acc, g3_ref[...])


def _mlp(x2, g2, sc, sh, w1, w2, g3, gate, *, tm, rows_per_mod):
    m = x2.shape[0]
    mod_rows = sc.shape[1]
    modmap = lambda i: ((i * tm) // rows_per_mod, 0, 0)
    const = lambda s: pl.BlockSpec(s, lambda i: (0, 0))
    mod = pl.BlockSpec((1, mod_rows, D_MODEL), modmap)
    return pl.pallas_call(
        functools.partial(_mlp_kernel, tf=512),
        grid=(m // tm,),
        in_specs=[pl.BlockSpec((tm, D_MODEL), lambda i: (i, 0)), const((1, D_MODEL)), mod, mod,
                  const((D_MODEL, D_FF)), const((D_FF, D_MODEL)), const((1, D_MODEL)), mod],
        out_specs=pl.BlockSpec((tm, D_MODEL), lambda i: (i, 0)),
        out_shape=jax.ShapeDtypeStruct((m, D_MODEL), f32),
        compiler_params=_cp(("parallel",)),
        name="mlp",
    )(x2, g2, sc, sh, w1, w2, g3, gate)


def _pad8(a):
    return jnp.concatenate([a, jnp.zeros((8 - a.shape[0], a.shape[1]), a.dtype)], axis=0)


def _by_group(rows):
    grp = lax.broadcasted_iota(i32, (8, rows.shape[1]), 0) // G_NSA
    return jnp.where(grp == 0, jnp.broadcast_to(rows[0:1], (8, rows.shape[1])),
                     jnp.broadcast_to(rows[1:2], (8, rows.shape[1])))


def _lane_to_sublane(row, offset, stride):
    sub = lax.broadcasted_iota(i32, (8, LANES), 0)
    lane = lax.broadcasted_iota(i32, (8, LANES), 1)
    return jnp.sum(jnp.where(lane == offset + stride * sub, jnp.broadcast_to(row, (8, LANES)), 0.0),
                   axis=1, keepdims=True)


def _fox_dec_kernel(pt_ref, fq_ref, misc_ref, bias_ref, knew_ref, *rest, pps):
    lf_refs = rest[:pps]
    kv_refs = rest[pps:2 * pps]
    o_ref, lfo_ref, m_scr, l_scr, acc_scr, c_scr = rest[2 * pps:]
    g = pl.program_id(1)
    n = pps * PAGE_SIZE
    q6f = _pad8(fq_ref[0])
    q6 = q6f.astype(bf16)
    lane = lax.broadcasted_iota(i32, (8, LANES), 1)

    @pl.when(g == 0)
    def _():
        lfrow = jnp.where(lane[0:1] < H_FOX, _log_sigmoid(misc_ref[0] + bias_ref[...]), 0.0)
        lfo_ref[0] = lfrow
        m_scr[...] = jnp.sum(q6f * _pad8(knew_ref[0, 0]), axis=1, keepdims=True)
        l_scr[...] = jnp.ones((8, 1), f32)
        acc_scr[...] = _pad8(knew_ref[0, 1])
        c_scr[...] = _lane_to_sublane(lfrow, 0, 1)

    carry = c_scr[...]
    biases = []
    for i in range(pps):
        lf = lf_refs[i][0, 0]
        suf = lf
        for sft in (1, 2, 4, 8, 16, 32, 64):
            suf = suf + jnp.where(lane + sft < LANES, pltpu.roll(suf, LANES - sft, 1), 0.0)
        biases.append(carry + (suf - lf))
        carry = carry + suf[:, 0:1]
    c_scr[...] = carry
    s = jnp.concatenate(biases, axis=1)
    sub = lax.broadcasted_iota(i32, (8, n), 0)
    for h in range(H_FOX):
        kht = jnp.concatenate([kv_refs[i][0, 0, 0, h] for i in range(pps)], axis=1).astype(bf16)
        s = jnp.where(sub == h, s + jnp.dot(q6, kht, preferred_element_type=f32), s)
    m = m_scr[...]
    m_new = jnp.maximum(m, jnp.max(s, axis=1, keepdims=True))
    alpha = jnp.exp(m - m_new)
    p = jnp.exp(s - m_new)
    l_scr[...] = alpha * l_scr[...] + jnp.sum(p, axis=1, keepdims=True)
    m_scr[...] = m_new
    pb = p.astype(bf16)
    sub_d = lax.broadcasted_iota(i32, (8, HEAD_DIM), 0)
    pv = jnp.zeros((8, HEAD_DIM), f32)
    for h in range(H_FOX):
        vht = jnp.concatenate([kv_refs[i][0, 0, 1, h] for i in range(pps)], axis=1).astype(bf16)
        pv = jnp.where(sub_d == h, _nt(pb, vht), pv)
    acc_scr[...] = alpha * acc_scr[...] + pv

    @pl.when(g == pl.num_programs(1) - 1)
    def _():
        o_ref[0] = acc_scr[...] / l_scr[...]


def _fox_decode(layer, page_table, fq, misc, bias_row, fkv_new, lf_t, cache_kv):
    b, n_pages = page_table.shape
    pps = FOX_PPS
    ng = n_pages // pps
    row3 = lambda w_: pl.BlockSpec((1, 1, w_), lambda i, g, pt: (i, 0, 0))

    def page(i, shape):
        zeros = (0,) * len(shape)
        return pl.BlockSpec((1, 1) + shape,
                            lambda bi, g, pt, i=i: (layer, pt[jnp.minimum(bi, b - 1),
                                                               n_pages - 1 - (jnp.minimum(g, ng - 1) * pps + i)]) + zeros)

    grid_spec = pltpu.PrefetchScalarGridSpec(
        num_scalar_prefetch=1,
        grid=(b, ng),
        in_specs=[pl.BlockSpec((1, H_FOX, HEAD_DIM), lambda i, g, pt: (i, 0, 0)), row3(LANES),
                  pl.BlockSpec((1, LANES), lambda i, g, pt: (0, 0)),
                  pl.BlockSpec((1, 2, H_FOX, HEAD_DIM), lambda i, g, pt: (i, 0, 0, 0))]
                 + [page(i, (8, PAGE_SIZE)) for i in range(pps)]
                 + [page(i, (2, H_FOX, HEAD_DIM, PAGE_SIZE)) for i in range(pps)],
        out_specs=[pl.BlockSpec((1, 8, HEAD_DIM), lambda i, g, pt: (i, 0, 0)), row3(LANES)],
        scratch_shapes=[pltpu.VMEM((8, 1), f32), pltpu.VMEM((8, 1), f32), pltpu.VMEM((8, HEAD_DIM), f32),
                        pltpu.VMEM((8, 1), f32)],
    )
    return pl.pallas_call(
        functools.partial(_fox_dec_kernel, pps=pps),
        grid_spec=grid_spec,
        out_shape=[jax.ShapeDtypeStruct((b, 8, HEAD_DIM), f32), jax.ShapeDtypeStruct((b, 1, LANES), f32)],
        compiler_params=_cp(("parallel", "arbitrary")),
        name="fox_decode",
    )(page_table, fq, misc, bias_row, fkv_new, *([lf_t] * pps), *([cache_kv] * pps))


def _nsa_cmp_dec_kernel(pt_ref, q_ref, *rest, pps, past):
    pg_refs = rest[:pps]
    p_ref, o_ref, mean_scr = rest[pps:]
    g = pl.program_id(1)
    per_page = PAGE_SIZE // CMP_BLOCK
    rows = 2 * KV_NSA * HEAD_DIM
    x = jnp.concatenate([pg_refs[i][0, 0].reshape(rows, PAGE_SIZE) for i in range(pps)], axis=1)
    r = lax.broadcasted_iota(i32, (pps * PAGE_SIZE, LANES), 0)
    c = lax.broadcasted_iota(i32, (pps * PAGE_SIZE, LANES), 1)
    avg = jnp.where(r // CMP_BLOCK == c, 1.0 / CMP_BLOCK, 0.0).astype(bf16)
    hi = x.astype(bf16)
    lo = (x - hi.astype(f32)).astype(bf16)
    mean_scr[g] = jnp.dot(hi, avg, preferred_element_type=f32) + jnp.dot(lo, avg, preferred_element_type=f32)

    @pl.when(g == pl.num_programs(1) - 1)
    def _():
        ng = mean_scr.shape[0]
        nbc = ng * pps * per_page
        mean = mean_scr[0]
        for gg in range(1, ng):
            mean = mean + pltpu.roll(mean_scr[gg], gg * pps * per_page, 1)
        mean = mean[:, :nbc].astype(bf16)
        part = lambda feat, kv: mean[(feat * KV_NSA + kv) * HEAD_DIM:(feat * KV_NSA + kv + 1) * HEAD_DIM]
        q6 = _pad8(q_ref[0]).astype(bf16)
        grp0 = lax.broadcasted_iota(i32, (8, 1), 0) // G_NSA == 0
        s = jnp.where(grp0, jnp.dot(q6, part(0, 0), preferred_element_type=f32),
                      jnp.dot(q6, part(0, 1), preferred_element_type=f32))
        j = lax.broadcasted_iota(i32, (8, nbc), 1)
        mask = ((j + 1) * CMP_BLOCK - 1) <= past
        s = jnp.where(mask, s, NEG)
        pe = jnp.where(mask, jnp.exp(s - jnp.max(s, axis=1, keepdims=True)), 0.0)
        den = jnp.sum(pe, axis=1, keepdims=True)
        p = pe / jnp.where(den > 0.0, den, 1.0)
        p_ref[0] = p
        pb = p.astype(bf16)
        o_ref[0] = jnp.where(grp0, _nt(pb, part(1, 0)), _nt(pb, part(1, 1)))


def _nsa_cmp_decode(layer, page_table, nq, cache_nsa, past):
    b, n_pages = page_table.shape
    pps = CMP_PPS
    nbc = past // CMP_BLOCK
    grid_spec = pltpu.PrefetchScalarGridSpec(
        num_scalar_prefetch=1,
        grid=(b, n_pages // pps),
        in_specs=[pl.BlockSpec((1, H_NSA, HEAD_DIM), lambda i, g, pt: (i, 0, 0))]
                 + [pl.BlockSpec((1, 1, 2, KV_NSA, HEAD_DIM, PAGE_SIZE),
                                 lambda bi, g, pt, i=i: (layer, pt[jnp.minimum(bi, b - 1),
                                                                   jnp.minimum(g, n_pages // pps - 1) * pps + i],
                                                         0, 0, 0, 0))
                    for i in range(pps)],
        out_specs=[pl.BlockSpec((1, 8, nbc), lambda i, g, pt: (i, 0, 0)),
                   pl.BlockSpec((1, 8, HEAD_DIM), lambda i, g, pt: (i, 0, 0))],
        scratch_shapes=[pltpu.VMEM((n_pages // pps, 2 * KV_NSA * HEAD_DIM, LANES), f32)],
    )
    return pl.pallas_call(
        functools.partial(_nsa_cmp_dec_kernel, pps=pps, past=past),
        grid_spec=grid_spec,
        out_shape=[jax.ShapeDtypeStruct((b, 8, nbc), f32), jax.ShapeDtypeStruct((b, 8, HEAD_DIM), f32)],
        compiler_params=_cp(("parallel", "arbitrary")),
        name="nsa_cmp_decode",
    )(page_table, nq, *([cache_nsa] * pps))


def _nsa_topk_dec_kernel(p_ref, idx_ref, *, past, width):
    b = p_ref.shape[0]
    nbc = p_ref.shape[2]
    nbs = -(-(past + 1) // SEL_BLOCK)
    imps = []
    for kv in range(KV_NSA):
        imps.append(p_ref[:, kv * G_NSA, :] + p_ref[:, kv * G_NSA + 1, :] + p_ref[:, kv * G_NSA + 2, :])
    imp = jnp.concatenate(imps, axis=0)
    imp = jnp.concatenate([imp, jnp.zeros((2 * b, width - nbc), f32)], axis=1)
    j = lax.broadcasted_iota(i32, (2 * b, width), 1)
    cur = past // SEL_BLOCK
    forced = (j == 0) | (j == cur) | (j == cur - 1)
    score = jnp.where(forced, FORCE, imp)
    score = jnp.where(j * SEL_BLOCK <= past, score, -1.0)
    score = jnp.where(j < nbs, score, -2.0)
    _, firsts = _select_topk(score, min(TOPK_BLOCKS, nbs))
    lane = lax.broadcasted_iota(i32, (2 * b, LANES), 1)
    out = jnp.zeros((2 * b, LANES), i32)
    for k, first in enumerate(firsts):
        out = jnp.where(lane == k, first.astype(i32), out)
    idx_ref[...] = out


def _nsa_topk_decode(p8, past):
    b, _, nbc = p8.shape
    width = 2 * nbc
    return pl.pallas_call(
        functools.partial(_nsa_topk_dec_kernel, past=past, width=width),
        grid=(1,),
        in_specs=[pl.BlockSpec(p8.shape, lambda i: (0, 0, 0))],
        out_specs=pl.BlockSpec((2 * b, LANES), lambda i: (0, 0)),
        out_shape=jax.ShapeDtypeStruct((2 * b, LANES), i32),
        compiler_params=_cp(("arbitrary",)),
        name="nsa_topk_decode",
    )(p8)


def _nsa_sel_dec_kernel(pt_ref, idx_ref, q_ref, misc_ref, ocmp_ref, nkv_ref, wnew_ref, win_ref, *rest, past, nsel):
    blk_refs = rest[:KV_NSA * nsel]
    o_ref = rest[KV_NSA * nsel]
    b = pl.program_id(0)
    n_past_blk = past // SEL_BLOCK
    q8f = _pad8(q_ref[0])
    q8 = q8f.astype(bf16)
    sub = lax.broadcasted_iota(i32, (8, 1), 0)
    grp0 = sub // G_NSA == 0
    per_page = PAGE_SIZE // SEL_BLOCK
    lane_blk = lax.broadcasted_iota(i32, (8, PAGE_SIZE), 1) // SEL_BLOCK
    m = jnp.full((8, 1), NEG, f32)
    l = jnp.zeros((8, 1), f32)
    acc = jnp.zeros((8, HEAD_DIM), f32)
    s_new = jnp.sum(q8f * _by_group(nkv_ref[0, 2]), axis=1, keepdims=True)
    v_new = _by_group(nkv_ref[0, 3])
    for kv in range(KV_NSA):
        in_group = (sub // G_NSA) == kv
        has_new = jnp.zeros((), jnp.bool_)
        for k in range(nsel):
            j = idx_ref[kv * pl.num_programs(0) + b, k]
            has_new = has_new | (j == n_past_blk)
            blk = blk_refs[kv * nsel + k]
            mask = in_group & (j < n_past_blk) & (lane_blk == j % per_page)
            s = jnp.where(mask, jnp.dot(q8, blk[0, 0, 0, 0].astype(bf16), preferred_element_type=f32), NEG)
            m_new = jnp.maximum(m, jnp.max(s, axis=1, keepdims=True))
            alpha = jnp.exp(m - m_new)
            pe = jnp.where(mask, jnp.exp(s - m_new), 0.0)
            l = alpha * l + jnp.sum(pe, axis=1, keepdims=True)
            acc = alpha * acc + _nt(pe.astype(bf16), blk[0, 0, 1, 0].astype(bf16))
            m = m_new
        new_ok = in_group & has_new
        m_new = jnp.maximum(m, jnp.where(new_ok, s_new, NEG))
        alpha = jnp.exp(m - m_new)
        p_new = jnp.where(new_ok, jnp.exp(s_new - m_new), 0.0)
        l = alpha * l + p_new
        acc = alpha * acc + p_new * v_new
        m = m_new
    o_sel = acc / jnp.where(l > 0.0, l, 1.0)
    wb = win_ref.shape[5]
    kpos = past - wb + lax.broadcasted_iota(i32, (8, wb), 1)
    rel = past - kpos
    wmask = (rel >= 0) & (rel < WINDOW) & (kpos >= 0)
    s = jnp.where(grp0, jnp.dot(q8, win_ref[0, 0, 0, 0].astype(bf16), preferred_element_type=f32),
                  jnp.dot(q8, win_ref[0, 0, 0, 1].astype(bf16), preferred_element_type=f32))
    s = jnp.where(wmask, s, NEG)
    s_w = jnp.sum(q8f * _by_group(wnew_ref[0, 0]), axis=1, keepdims=True)
    m = jnp.maximum(jnp.max(s, axis=1, keepdims=True), s_w)
    p = jnp.where(wmask, jnp.exp(s - m), 0.0)
    p_w = jnp.exp(s_w - m)
    pb = p.astype(bf16)
    pv = jnp.where(grp0, _nt(pb, win_ref[0, 0, 1, 0].astype(bf16)), _nt(pb, win_ref[0, 0, 1, 1].astype(bf16)))
    o_win = (pv + p_w * _by_group(wnew_ref[0, 1])) / (jnp.sum(p, axis=1, keepdims=True) + p_w)
    misc = misc_ref[0]
    gates = [jax.nn.sigmoid(_lane_to_sublane(misc, GATE0 + r, 3)) for r in range(3)]
    o_ref[0] = gates[0] * ocmp_ref[0] + gates[1] * o_sel + gates[2] * o_win


def _nsa_sel_decode(layer, page_table, idx, nq, misc, ocmp, nkv_new, win_new, cache_win, cache_nsa, past):
    b = page_table.shape[0]
    nsel = idx.shape[1]
    n_past_blk = past // SEL_BLOCK
    per_page = PAGE_SIZE // SEL_BLOCK
    wb = cache_win.shape[5]

    def blk_spec(kv, k):
        def imap(i, pt, ix):
            ii = jnp.minimum(i, b - 1)
            jp = jnp.clip(ix[kv * b + ii, k], 0, n_past_blk - 1)
            return (layer, pt[ii, jp // per_page], 1, kv, 0, 0)
        return pl.BlockSpec((1, 1, 2, 1, HEAD_DIM, PAGE_SIZE), imap)

    grid_spec = pltpu.PrefetchScalarGridSpec(
        num_scalar_prefetch=2,
        grid=(b,),
        in_specs=[pl.BlockSpec((1, H_NSA, HEAD_DIM), lambda i, pt, ix: (i, 0, 0)),
                  pl.BlockSpec((1, 1, LANES), lambda i, pt, ix: (i, 0, 0)),
                  pl.BlockSpec((1, 8, HEAD_DIM), lambda i, pt, ix: (i, 0, 0)),
                  pl.BlockSpec((1, 4, KV_NSA, HEAD_DIM), lambda i, pt, ix: (i, 0, 0, 0)),
                  pl.BlockSpec((1, 2, KV_NSA, HEAD_DIM), lambda i, pt, ix: (i, 0, 0, 0)),
                  pl.BlockSpec((1, 1, 2, KV_NSA, HEAD_DIM, wb), lambda i, pt, ix: (layer, i, 0, 0, 0, 0))]
                 + [blk_spec(kv, k) for kv in range(KV_NSA) for k in range(nsel)],
        out_specs=pl.BlockSpec((1, 8, HEAD_DIM), lambda i, pt, ix: (i, 0, 0)),
    )
    return pl.pallas_call(
        functools.partial(_nsa_sel_dec_kernel, past=past, nsel=nsel),
        grid_spec=grid_spec,
        out_shape=jax.ShapeDtypeStruct((b, 8, HEAD_DIM), f32),
        compiler_params=_cp(("parallel",)),
        name="nsa_sel_decode",
    )(page_table, idx, nq, misc, ocmp, nkv_new, win_new, cache_win, *([cache_nsa] * (KV_NSA * nsel)))


def _proj_columns():
    off = np.cumsum([0, 384, 384, 384, 6, 384, 128, 128, 128, 128, 128, 128, 18, 256])
    fq, fk, fv, ff, nq, nkc, nvc, nks, nvs, nkw, nvw, ng, u = [int(o) for o in off[:13]]
    cols = list(range(fq, fq + 384))
    for h in NSA_PAIR_ORDER:
        cols += list(range(nq + h * HEAD_DIM, nq + (h + 1) * HEAD_DIM))
    cols += list(range(fk, fk + 768))
    cols += list(range(nkc, nkc + 512))
    cols += list(range(nkw, nkw + 256))
    cols += list(range(u, u + 256))
    cols += list(range(ff, ff + 6)) + list(range(ng, ng + 18)) + [N_IN] * (LANES - 24)
    return np.asarray(cols, np.int32)


def _out_rows():
    rows = list(range(0, 384))
    for h in NSA_PAIR_ORDER:
        rows += list(range(384 + h * HEAD_DIM, 384 + (h + 1) * HEAD_DIM))
    rows += list(range(768, 1024))
    return np.asarray(rows, np.int32)


def _rope_tables(pos):
    half = ROT_DIM // 2
    inv = ROPE_THETA ** (-jnp.arange(0, ROT_DIM, 2, dtype=f32) / ROT_DIM)
    ang = pos.astype(f32)[:, None] * inv[None, :]
    cos, sin = jnp.cos(ang), jnp.sin(ang)
    n = pos.shape[0]
    one = jnp.ones((n, HEAD_DIM - ROT_DIM), f32)
    zero8 = jnp.zeros((n, half), f32)
    zrest = jnp.zeros((n, HEAD_DIM - ROT_DIM), f32)
    c = jnp.concatenate([cos, cos, one], axis=1)
    sa = jnp.concatenate([-sin, zero8, zrest], axis=1)
    sb = jnp.concatenate([zero8, sin, zrest], axis=1)
    return tuple(jnp.concatenate([a, a], axis=1) for a in (c, sa, sb))


def kernel(x_prompt, x_sample, cache_fox_kv, cache_fox_logf, cache_nsa_kv, cache_nsa_win, state_pool, page_table,
           c_prompt, c_sample, w_ada, b_ada, norm_g, w_in, b_fox_f, w_out, w_pool, pool_scale, w_ff1, w_ff2):
    depth = w_in.shape[0]
    bp, t, _ = x_prompt.shape
    bs = x_sample.shape[0]
    past = page_table.shape[1] * PAGE_SIZE
    mp = bp * t

    cols = _proj_columns()
    w_in_p = jnp.concatenate([w_in, jnp.zeros((depth, D_MODEL, 1), f32)], axis=2)[:, :, cols].astype(bf16)
    w_out_p = w_out[:, _out_rows(), :].astype(bf16)
    w_pool_bd = jnp.zeros((depth, C_POOL, C_POOL), f32)
    for g in range(len(POOL_WINDOWS)):
        w_pool_bd = w_pool_bd.at[:, g * POOL_GW:(g + 1) * POOL_GW, g * POOL_GW:(g + 1) * POOL_GW].set(w_pool[:, g])
    w_pool_bd = w_pool_bd.astype(bf16)
    w1 = w_ff1.astype(bf16)
    w2 = w_ff2.astype(bf16)
    bias_rows = jnp.pad(b_fox_f, ((0, 0), (0, LANES - H_FOX))).reshape(depth, 1, LANES)

    rope_p = _rope_tables(jnp.arange(t))
    rope_s = _rope_tables(jnp.full((bs,), past, i32))

    to_last = (0, 1, 3, 4, 5, 2)
    fox_t = jnp.transpose(cache_fox_kv, to_last)
    nsa_t = jnp.transpose(cache_nsa_kv, to_last)
    win_t = jnp.transpose(cache_nsa_win, to_last)
    lf_t = jnp.pad(jnp.swapaxes(cache_fox_logf, 2, 3), ((0, 0), (0, 0), (0, 8 - H_FOX), (0, 0)))
    nat_from_pair = np.argsort(np.asarray(NSA_PAIR_ORDER))

    mod = _ada(jnp.concatenate([c_prompt, c_sample], axis=0), w_ada, b_ada)
    mod = mod.reshape(depth, bp + bs, 6, D_MODEL)

    yp = x_prompt.reshape(mp, D_MODEL)
    ys = x_sample.reshape(bs, D_MODEL)
    sp, ss = [], []
    for l in range(depth):
        g = norm_g[l].reshape(4, 1, D_MODEL)
        modp = [mod[l, :bp, k].reshape(bp, 1, D_MODEL) for k in range(6)]
        mods = [mod[l, bp:, k].reshape(1, bs, D_MODEL) for k in range(6)]

        (fq, nq, fkv, nkv, win, u, misc, fkvb, nkvb, winb, kcm) = _inproj(
            yp, g[0], modp[1], modp[0], w_in_p[l], *rope_p, tm=512, rows_per_mod=t, rope_rows=t, with_means=True)
        logf, ccol, crow = _foxprep(misc, bias_rows[l], bp, t)
        o_fox = _fox_attn(fq, fkvb, ccol, crow, bp, t)
        o_nsa = _nsa_attn(nq, kcm, nkvb, winb, misc, bp, t)
        y1 = _outproj(yp, o_fox, o_nsa, u, w_out_p[l], w_pool_bd[l], pool_scale[l].reshape(1, C_POOL), g[1],
                      modp[2], tm=512, t=t)
        yp = _mlp(y1, g[2], modp[4], modp[3], w1[l], w2[l], g[3], modp[5], tm=512, rows_per_mod=t)
        wl = min(WINDOW, t)
        sp.append((fkv.reshape(bp, t, 2, H_FOX, HEAD_DIM), logf.reshape(bp, t, H_FOX),
                   nkv.reshape(bp, t, 4, KV_NSA, HEAD_DIM),
                   win.reshape(bp, t, 2, KV_NSA, HEAD_DIM)[:, t - wl:],
                   u.reshape(bp, t, C_POOL)[:, t - POOL_STATE:]))

        (fq_s, nq_s, fkv_s, nkv_s, win_s, u_s, misc_s, _, _, _, _) = _inproj(
            ys, g[0], mods[1], mods[0], w_in_p[l], *rope_s, tm=bs, rows_per_mod=bs, rope_rows=bs, with_means=False)
        r3 = lambda a: a.reshape(bs, 1, a.shape[-1])
        fq6 = fq_s.astype(f32).reshape(bs, H_FOX, HEAD_DIM)
        nq6 = nq_s.astype(f32).reshape(bs, H_NSA, HEAD_DIM)[:, nat_from_pair]
        nkv4 = nkv_s.reshape(bs, 4, KV_NSA, HEAD_DIM)
        win2 = win_s.reshape(bs, 2, KV_NSA, HEAD_DIM)
        o_fox_s, logf_s = _fox_decode(l, page_table, fq6, r3(misc_s), bias_rows[l],
                                      fkv_s.reshape(bs, 2, H_FOX, HEAD_DIM), lf_t, fox_t)
        p8, o_cmp = _nsa_cmp_decode(l, page_table, nq6, nsa_t, past)
        idx = _nsa_topk_decode(p8, past)[:, :min(TOPK_BLOCKS, -(-(past + 1) // SEL_BLOCK))]
        o_nsa_s = _nsa_sel_decode(l, page_table, idx, nq6, r3(misc_s), o_cmp, nkv4, win2,
                                  win_t, nsa_t, past)
        o_fox_s = o_fox_s[:, :H_FOX].reshape(bs, 384).astype(bf16)
        o_nsa_s = o_nsa_s[:, np.asarray(NSA_PAIR_ORDER)].reshape(bs, 384).astype(bf16)
        u_ext = jnp.concatenate([state_pool[l], u_s.reshape(bs, 1, C_POOL)], axis=1)
        y1s = _outproj_dec(ys, o_fox_s, o_nsa_s, u_ext, w_out_p[l],
                           w_pool_bd[l], pool_scale[l].reshape(1, C_POOL), g[1], mods[2][0], past)
        ys = _mlp(y1s, g[2], mods[4], mods[3], w1[l], w2[l], g[3], mods[5], tm=bs, rows_per_mod=bs)
        win_all = jnp.concatenate([cache_nsa_win[l], win2[:, None]], axis=1)
        ss.append((fkv_s.reshape(bs, 1, 2, H_FOX, HEAD_DIM), logf_s[:, :, :H_FOX], nkv4[:, None],
                   win_all[:, 1:], u_ext[:, 1:]))

    stk = lambda lst, i: jnp.stack([s[i] for s in lst], axis=0)
    return (yp.reshape(bp, t, D_MODEL), ys.reshape(bs, 1, D_MODEL),
            stk(sp, 0), stk(sp, 1), stk(sp, 2), stk(sp, 3), stk(sp, 4),
            stk(ss, 0), stk(ss, 1), stk(ss, 2), stk(ss, 3), stk(ss, 4))
```

```python
import functools

import numpy as np
import jax
import jax.numpy as jnp
from jax import lax
from jax.experimental import pallas as pl
from jax.experimental.pallas import tpu as pltpu

f32 = jnp.float32
bf16 = jnp.bfloat16
i32 = jnp.int32

D_MODEL = 1024
HEAD_DIM = 64
H_FOX = 6
H_NSA = 6
KV_NSA = 2
G_NSA = H_NSA // KV_NSA
POOL_WINDOWS = (2, 4, 8, 16)
C_POOL = 256
POOL_GW = 64
POOL_STATE = 15
ROT_DIM = 16
ROPE_THETA = 500000.0
CMP_BLOCK = 64
SEL_BLOCK = 64
TOPK_BLOCKS = 8
WINDOW = 512
PAGE_SIZE = 128
D_FF = 4 * D_MODEL
EPS = 1e-6
NEG = -1e30
FORCE = 1e4
SCALE = HEAD_DIM ** -0.5
N_IN = 2584

LANES = 128
HALF = LANES // 2
VMEM_LIMIT = 56 * 1024 * 1024

C_FQ = 0
C_NQ = 384
C_FKV = 768
C_NKV = 1536
C_WIN = 2048
C_U = 2304
C_MISC = 2560
N_PROJ = 2688
GATE0 = H_FOX
NSA_PAIR_ORDER = (0, 3, 1, 4, 2, 5)

FOX_TQ = 512
FOX_TK = 512
NSA_TQ = 256
NSA_TK = 512
FOX_PPS = 16
CMP_PPS = 16


def _cp(sem):
    return pltpu.CompilerParams(dimension_semantics=sem, vmem_limit_bytes=VMEM_LIMIT)


def _nt(a, b):
    return lax.dot_general(a, b, (((1,), (1,)), ((), ())), preferred_element_type=f32)


def _rms(x, g):
    return x * lax.rsqrt(jnp.mean(x * x, axis=-1, keepdims=True) + EPS) * g


def _log_sigmoid(x):
    return jnp.minimum(x, 0.0) - jnp.log1p(jnp.exp(-jnp.abs(x)))


def _ada_kernel(c_ref, w_ref, b_ref, o_ref):
    c = c_ref[...]
    a = (c * jax.nn.sigmoid(c)).astype(bf16)
    o_ref[0] = jnp.dot(a, w_ref[0].astype(bf16), preferred_element_type=f32) + b_ref[0]


def _ada(c_all, w_ada, b_ada):
    depth = w_ada.shape[0]
    n = c_all.shape[0]
    tn = 1024
    return pl.pallas_call(
        _ada_kernel,
        grid=(depth, 6 * D_MODEL // tn),
        in_specs=[pl.BlockSpec((n, D_MODEL), lambda l, j: (0, 0)),
                  pl.BlockSpec((1, D_MODEL, tn), lambda l, j: (l, 0, j)),
                  pl.BlockSpec((1, 1, tn), lambda l, j: (l, 0, j))],
        out_specs=pl.BlockSpec((1, n, tn), lambda l, j: (l, 0, j)),
        out_shape=jax.ShapeDtypeStruct((depth, n, 6 * D_MODEL), f32),
        compiler_params=_cp(("parallel", "parallel")),
        name="ada_mod",
    )(c_all, w_ada, b_ada.reshape(depth, 1, 6 * D_MODEL))


def _inproj_kernel(x_ref, g_ref, sc_ref, sh_ref, w_ref, cos_ref, sa_ref, sb_ref,
                   fq_ref, nq_ref, fkv_ref, nkv_ref, win_ref, u_ref, misc_ref,
                   fkvb_ref, nkvb_ref, winb_ref, kcm_ref, *, tm, with_means):
    x = x_ref[...]
    h = (_rms(x, g_ref[...]) * (1.0 + sc_ref[0]) + sh_ref[0]).astype(bf16)
    cos = cos_ref[...]
    sa = sa_ref[...]
    sb = sb_ref[...]

    def rope(z):
        return z * cos + pltpu.roll(z, LANES - ROT_DIM // 2, 1) * sa + pltpu.roll(z, ROT_DIM // 2, 1) * sb

    def means(z):
        return jnp.sum(z.reshape(tm // CMP_BLOCK, CMP_BLOCK, LANES), axis=1) * (1.0 / CMP_BLOCK)

    def emit(blk, z):
        c = blk * LANES
        if c < C_NQ:
            fq_ref[:, c - C_FQ:c - C_FQ + LANES] = (z * SCALE).astype(bf16)
        elif c < C_FKV:
            nq_ref[:, c - C_NQ:c - C_NQ + LANES] = (rope(z) * SCALE).astype(bf16)
        elif c < C_NKV:
            fkv_ref[:, c - C_FKV:c - C_FKV + LANES] = z
            fkvb_ref[:, c - C_FKV:c - C_FKV + LANES] = z.astype(bf16)
        elif c < C_WIN:
            o = c - C_NKV
            if o in (0, 2 * LANES):
                z = rope(z)
            nkv_ref[:, o:o + LANES] = z
            nkvb_ref[:, o:o + LANES] = z.astype(bf16)
            if with_means and o < 2 * LANES:
                kcm_ref[:, o:o + LANES] = means(z)
        elif c < C_U:
            o = c - C_WIN
            if o == 0:
                z = rope(z)
            win_ref[:, o:o + LANES] = z
            winb_ref[:, o:o + LANES] = z.astype(bf16)
        elif c < C_MISC:
            u_ref[:, c - C_U:c - C_U + LANES] = z
        else:
            misc_ref[...] = z

    nblk = N_PROJ // LANES
    for b0 in range(0, nblk, 2):
        nb = min(2, nblk - b0)
        z2 = jnp.dot(h, w_ref[:, b0 * LANES:(b0 + nb) * LANES], preferred_element_type=f32)
        for k in range(nb):
            emit(b0 + k, z2[:, k * LANES:(k + 1) * LANES])
    if not with_means:
        kcm_ref[...] = jnp.zeros(kcm_ref.shape, f32)


def _inproj(x2, g0, sc, sh, w, cos, sa, sb, *, tm, rows_per_mod, rope_rows, with_means):
    m = x2.shape[0]
    nt = m // tm
    mod_rows = sc.shape[1]
    rope_blocks = rope_rows // tm
    kc_rows = max(tm // CMP_BLOCK, 8)
    modmap = lambda i: ((i * tm) // rows_per_mod, 0, 0)
    ropemap = lambda i: (i % rope_blocks, 0)
    row = lambda w_: pl.BlockSpec((tm, w_), lambda i: (i, 0))
    outs = [((m, 384), bf16), ((m, 384), bf16), ((m, 768), f32), ((m, 512), f32), ((m, 256), f32),
            ((m, 256), f32), ((m, LANES), f32), ((m, 768), bf16), ((m, 512), bf16), ((m, 256), bf16)]
    out_shape = [jax.ShapeDtypeStruct(s, d) for s, d in outs] + [jax.ShapeDtypeStruct((nt * kc_rows, 256), f32)]
    out_specs = [row(s[1]) for s, _ in outs] + [pl.BlockSpec((kc_rows, 256), lambda i: (i, 0))]
    return pl.pallas_call(
        functools.partial(_inproj_kernel, tm=tm, with_means=with_means),
        grid=(nt,),
        in_specs=[row(D_MODEL),
                  pl.BlockSpec((1, D_MODEL), lambda i: (0, 0)),
                  pl.BlockSpec((1, mod_rows, D_MODEL), modmap),
                  pl.BlockSpec((1, mod_rows, D_MODEL), modmap),
                  pl.BlockSpec((D_MODEL, N_PROJ), lambda i: (0, 0)),
                  pl.BlockSpec((tm, LANES), ropemap),
                  pl.BlockSpec((tm, LANES), ropemap),
                  pl.BlockSpec((tm, LANES), ropemap)],
        out_specs=out_specs,
        out_shape=out_shape,
        compiler_params=_cp(("parallel",)),
        name="in_proj",
    )(x2, g0, sc, sh, w, cos, sa, sb)


def _foxprep_kernel(misc_ref, bias_ref, lf_ref, ccol_ref, crow_ref, *, t):
    blk = 256
    lane = lax.broadcasted_iota(i32, (t, LANES), 1)
    lf = jnp.where(lane < H_FOX, _log_sigmoid(misc_ref[...] + bias_ref[...]), 0.0)
    lf_ref[...] = lf[:, :H_FOX]
    r = lax.broadcasted_iota(i32, (blk, blk), 0)
    c = lax.broadcasted_iota(i32, (blk, blk), 1)
    ltri = (r >= c).astype(f32)
    carry = jnp.zeros((1, LANES), f32)
    for b in range(t // blk):
        cb = jnp.dot(ltri, lf[b * blk:(b + 1) * blk], preferred_element_type=f32,
                     precision=lax.Precision.HIGHEST) + carry
        ccol_ref[b * blk:(b + 1) * blk, :] = cb
        carry = cb[blk - 1:blk, :]
    er = lax.broadcasted_iota(i32, (8, LANES), 0)
    ec = lax.broadcasted_iota(i32, (8, LANES), 1)
    eye = (er == ec).astype(f32)
    crow_ref[0] = lax.dot_general(eye, ccol_ref[...], (((1,), (1,)), ((), ())),
                                  preferred_element_type=f32, precision=lax.Precision.HIGHEST)


def _foxprep(misc, bias_row, b, t):
    return pl.pallas_call(
        functools.partial(_foxprep_kernel, t=t),
        grid=(b,),
        in_specs=[pl.BlockSpec((t, LANES), lambda i: (i, 0)),
                  pl.BlockSpec((1, LANES), lambda i: (0, 0))],
        out_specs=[pl.BlockSpec((t, H_FOX), lambda i: (i, 0)),
                   pl.BlockSpec((t, LANES), lambda i: (i, 0)),
                   pl.BlockSpec((1, 8, t), lambda i: (i, 0, 0))],
        out_shape=[jax.ShapeDtypeStruct((b * t, H_FOX), f32),
                   jax.ShapeDtypeStruct((b * t, LANES), f32),
                   jax.ShapeDtypeStruct((b, 8, t), f32)],
        compiler_params=_cp(("parallel",)),
        name="fox_prep",
    )(misc, bias_row)


def _tn(a, b):
    return lax.dot_general(a, b, (((0,), (0,)), ((), ())), preferred_element_type=f32)


def _softmax_step_t(carry, st, mask, vb):
    m, l, acc = carry
    if mask is not None:
        st = jnp.where(mask, st, NEG)
    m_new = jnp.maximum(m, jnp.max(st, axis=0, keepdims=True))
    alpha = jnp.exp(m - m_new)
    p = jnp.exp(st - m_new)
    if mask is not None:
        p = jnp.where(mask, p, 0.0)
    l = alpha * l + jnp.sum(p, axis=0, keepdims=True)
    acc = alpha * acc + _tn(vb, p.astype(bf16))
    return m_new, l, acc


def _fox_kernel(q_ref, kv_ref, ccol_ref, crow_ref, o_ref, *, tq, tk):
    qi = pl.program_id(1)
    q0 = pl.multiple_of(qi * tq, tq)
    lo = lax.broadcasted_iota(i32, (tq, LANES), 1) < HALF
    top = lax.broadcasted_iota(i32, (LANES, tq), 0) < HALF
    kk = lax.broadcasted_iota(i32, (tk, tq), 0)
    qq = lax.broadcasted_iota(i32, (tk, tq), 1)
    diag = kk <= qq
    diag2 = jnp.concatenate([diag, diag], axis=1)
    zero = jnp.zeros((), bf16)
    for p in range(H_FOX // 2):
        q = q_ref[:, p * LANES:(p + 1) * LANES]
        qst = jnp.concatenate([jnp.where(lo, q, zero), jnp.where(lo, zero, q)], axis=0)
        cq0 = crow_ref[0, 2 * p:2 * p + 1, pl.ds(q0, tq)]
        cq1 = crow_ref[0, 2 * p + 1:2 * p + 2, pl.ds(q0, tq)]

        def chunk(c, carry, mask, p=p, qst=qst, cq0=cq0, cq1=cq1):
            k0 = pl.multiple_of(c * tk, tk)
            kb = kv_ref[pl.ds(k0, tk), p * LANES:(p + 1) * LANES]
            vb = kv_ref[pl.ds(k0, tk), 384 + p * LANES:384 + (p + 1) * LANES]
            ck0 = ccol_ref[pl.ds(k0, tk), 2 * p:2 * p + 1]
            ck1 = ccol_ref[pl.ds(k0, tk), 2 * p + 1:2 * p + 2]
            st = _nt(kb, qst) + jnp.concatenate([cq0 - ck0, cq1 - ck1], axis=1)
            return _softmax_step_t(carry, st, mask, vb)

        init = (jnp.full((1, 2 * tq), NEG, f32), jnp.zeros((1, 2 * tq), f32), jnp.zeros((LANES, 2 * tq), f32))
        carry = lax.fori_loop(0, qi, lambda c, cr: chunk(c, cr, None), init)
        _, l, acc = chunk(qi, carry, diag2)
        ot = acc / l
        o_ref[:, p * LANES:(p + 1) * LANES] = jnp.where(top, ot[:, :tq], ot[:, tq:]).T.astype(bf16)


def _fox_attn(fq, fkvb, ccol, crow, b, t):
    tq, tk = FOX_TQ, FOX_TK
    assert tq == tk
    nq = t // tq
    return pl.pallas_call(
        functools.partial(_fox_kernel, tq=tq, tk=tk),
        grid=(b, nq),
        in_specs=[pl.BlockSpec((tq, 384), lambda i, j: (i * nq + j, 0)),
                  pl.BlockSpec((t, 768), lambda i, j: (i, 0)),
                  pl.BlockSpec((t, LANES), lambda i, j: (i, 0)),
                  pl.BlockSpec((1, 8, t), lambda i, j: (i, 0, 0))],
        out_specs=pl.BlockSpec((tq, 384), lambda i, j: (i * nq + j, 0)),
        out_shape=jax.ShapeDtypeStruct((b * t, 384), bf16),
        compiler_params=_cp(("parallel", "parallel")),
        name="fox_attn",
    )(fq, fkvb, ccol, crow)


def _select_topk(score, nsel, axis=1):
    nb = score.shape[axis]
    jf = lax.broadcasted_iota(i32, score.shape, axis).astype(f32)
    sel = jnp.zeros(score.shape, f32)
    firsts = []
    for _ in range(nsel):
        mx = jnp.max(score, axis=axis, keepdims=True)
        first = jnp.min(jnp.where(score == mx, jf, float(nb)), axis=axis, keepdims=True)
        hit = jf == first
        sel = jnp.where(hit, 1.0, sel)
        score = jnp.where(hit, -3.0, score)
        firsts.append(first)
    return sel, firsts


def _nsa_kernel(q_ref, kcm_ref, nkv_ref, win_ref, misc_ref, o_ref, *, tq, tk, t):
    qi = pl.program_id(1)
    s0 = qi * tq
    nbc = t // CMP_BLOCK
    lane = lax.broadcasted_iota(i32, (tq, LANES), 1)
    lo = lane < HALF
    zero = jnp.zeros((), bf16)
    kpos = lax.broadcasted_iota(i32, (tk, tq), 0)
    qpos = lax.broadcasted_iota(i32, (tk, tq), 1) + s0
    qpos_b = lax.broadcasted_iota(i32, (nbc, tq), 1) + s0
    jb = lax.broadcasted_iota(i32, (nbc, tq), 0)
    gates = jax.nn.sigmoid(misc_ref[...].T)

    def tile3(a):
        return jnp.concatenate([a, a, a], axis=1)

    def per_head(ot):
        return [ot[:, g * tq:(g + 1) * tq] for g in range(G_NSA)]

    def attend(qk, kref, klane, vlane, c_lo, c_hi, maskfn):
        def body(c, carry):
            k0 = pl.multiple_of(c * tk, tk)
            kb = kref[pl.ds(k0, tk), klane:klane + LANES]
            vb = kref[pl.ds(k0, tk), vlane:vlane + LANES]
            return _softmax_step_t(carry, _nt(kb, qk), tile3(maskfn(k0)), vb)
        init = (jnp.full((1, 3 * tq), NEG, f32), jnp.zeros((1, 3 * tq), f32), jnp.zeros((LANES, 3 * tq), f32))
        _, l, acc = lax.fori_loop(c_lo, c_hi, body, init)
        return per_head(acc / l)

    kcm = kcm_ref[...]
    kc = kcm[:, :LANES].astype(bf16)
    vc = kcm[:, LANES:].astype(bf16)
    outs = []
    for kv in range(KV_NSA):
        keep = lo if kv == 0 else jnp.logical_not(lo)
        qk = jnp.concatenate([jnp.where(keep, q_ref[:, p * LANES:(p + 1) * LANES], zero)
                              for p in range(G_NSA)], axis=0)
        cmask3 = tile3(((jb + 1) * CMP_BLOCK - 1) <= qpos_b)
        sc = jnp.where(cmask3, _nt(kc, qk), NEG)
        mx = jnp.max(sc, axis=0, keepdims=True)
        pe = jnp.where(cmask3, jnp.exp(sc - mx), 0.0)
        den = jnp.sum(pe, axis=0, keepdims=True)
        pc = pe / jnp.where(den > 0.0, den, 1.0)
        o_cmp = per_head(_tn(vc, pc.astype(bf16)))
        imp = pc[:, :tq] + pc[:, tq:2 * tq] + pc[:, 2 * tq:]
        cur = qpos_b // SEL_BLOCK
        forced = (jb == 0) | (jb == cur) | (jb == cur - 1)
        score = jnp.where(forced, FORCE, imp)
        score = jnp.where(jb * SEL_BLOCK <= qpos_b, score, -1.0)
        sel, _ = _select_topk(score, min(TOPK_BLOCKS, nbc), axis=0)
        selt = jnp.concatenate([sel, jnp.zeros((LANES - nbc, tq), f32)], axis=0).astype(bf16)

        def sel_mask(k0, selt=selt):
            er = (lax.broadcasted_iota(i32, (tk, LANES), 0) + k0) // SEL_BLOCK
            ec = lax.broadcasted_iota(i32, (tk, LANES), 1)
            expand = jnp.where(er == ec, 1.0, 0.0).astype(bf16)
            picked = jnp.dot(expand, selt, preferred_element_type=f32) > 0.5
            return picked & ((kpos + k0) <= qpos)

        def win_mask(k0):
            rel = qpos - (kpos + k0)
            return (rel >= 0) & (rel < WINDOW)

        o_sel = attend(qk, nkv_ref, 2 * LANES, 3 * LANES, 0, (s0 + tq + tk - 1) // tk, sel_mask)
        w_lo = jnp.maximum(s0 - WINDOW + 1, 0) // tk
        o_win = attend(qk, win_ref, 0, LANES, w_lo, (s0 + tq + tk - 1) // tk, win_mask)
        for g in range(G_NSA):
            r = GATE0 + 3 * (kv * G_NSA + g)
            outs.append(gates[r:r + 1] * o_cmp[g] + gates[r + 1:r + 2] * o_sel[g] + gates[r + 2:r + 3] * o_win[g])
    top = lax.broadcasted_iota(i32, (LANES, tq), 0) < HALF
    for p in range(G_NSA):
        o_ref[:, p * LANES:(p + 1) * LANES] = jnp.where(top, outs[p], outs[G_NSA + p]).T.astype(bf16)


def _nsa_attn(nq, kcm, nkvb, winb, misc, b, t):
    tq, tk = NSA_TQ, NSA_TK
    nq_t = t // tq
    nbc = t // CMP_BLOCK
    return pl.pallas_call(
        functools.partial(_nsa_kernel, tq=tq, tk=tk, t=t),
        grid=(b, nq_t),
        in_specs=[pl.BlockSpec((tq, 384), lambda i, j: (i * nq_t + j, 0)),
                  pl.BlockSpec((nbc, 256), lambda i, j: (i, 0)),
                  pl.BlockSpec((t, 512), lambda i, j: (i, 0)),
                  pl.BlockSpec((t, 256), lambda i, j: (i, 0)),
                  pl.BlockSpec((tq, LANES), lambda i, j: (i * nq_t + j, 0))],
        out_specs=pl.BlockSpec((tq, 384), lambda i, j: (i * nq_t + j, 0)),
        out_shape=jax.ShapeDtypeStruct((b * t, 384), bf16),
        compiler_params=_cp(("parallel", "parallel")),
        name="nsa_attn",
    )(nq, kcm, nkvb, winb, misc)


def _pool_project(d, wp_ref, ps_ref):
    return (jnp.dot(d.astype(bf16), wp_ref[...], preferred_element_type=f32) * ps_ref[...]).astype(bf16)


def _out_tail(x, ofox, onsa, opool, wo_ref, g_ref, gate):
    o = (jnp.dot(ofox, wo_ref[0:384, :], preferred_element_type=f32)
         + jnp.dot(onsa, wo_ref[384:768, :], preferred_element_type=f32)
         + jnp.dot(opool, wo_ref[768:1024, :], preferred_element_type=f32))
    return x + gate * _rms(o, g_ref[...])


def _outproj_kernel(x_ref, ofox_ref, onsa_ref, u_ref, halo_ref, wo_ref, wp_ref, ps_ref, g_ref, gate_ref,
                    o_ref, ext_ref, *, tm, t):
    i = pl.program_id(0)
    pos0 = (i * tm) % t
    u = u_ref[...]
    ext_ref[0:16, :] = jnp.where(pos0 > 0, halo_ref[...], 0.0)
    ext_ref[16:, :] = u
    pos = lax.broadcasted_iota(i32, (tm, LANES), 0) + pos0
    lo = lax.broadcasted_iota(i32, (tm, LANES), 1) < HALF

    def shifted(k, c):
        return ext_ref[16 - k:16 - k + tm, c * LANES:(c + 1) * LANES]

    def cnt(w):
        return jnp.minimum(w, pos + 1).astype(f32)

    ds = []
    for c, (wa, wb) in enumerate(((POOL_WINDOWS[0], POOL_WINDOWS[1]), (POOL_WINDOWS[2], POOL_WINDOWS[3]))):
        run = shifted(0, c)
        sums = {}
        for k in range(1, wb):
            run = run + shifted(k, c)
            if k + 1 in (wa, wb):
                sums[k + 1] = run
        ds.append(jnp.where(lo, sums[wa] / cnt(wa), sums[wb] / cnt(wb)) - u[:, c * LANES:(c + 1) * LANES])
    opool = _pool_project(jnp.concatenate(ds, axis=1), wp_ref, ps_ref)
    o_ref[...] = _out_tail(x_ref[...], ofox_ref[...], onsa_ref[...], opool, wo_ref, g_ref, gate_ref[0])


def _outproj(x2, ofox, onsa, u, wo, wp, ps, g1, gate, *, tm, t):
    m = x2.shape[0]
    row = lambda w_: pl.BlockSpec((tm, w_), lambda i: (i, 0))
    const = lambda s: pl.BlockSpec(s, lambda i: (0, 0))
    return pl.pallas_call(
        functools.partial(_outproj_kernel, tm=tm, t=t),
        grid=(m // tm,),
        in_specs=[row(D_MODEL), row(384), row(384), row(256),
                  pl.BlockSpec((16, 256), lambda i: (jnp.maximum(i * (tm // 16) - 1, 0), 0)),
                  const((D_MODEL, D_MODEL)), const((256, 256)), const((1, 256)), const((1, D_MODEL)),
                  pl.BlockSpec((1, 1, D_MODEL), lambda i: ((i * tm) // t, 0, 0))],
        out_specs=row(D_MODEL),
        out_shape=jax.ShapeDtypeStruct((m, D_MODEL), f32),
        scratch_shapes=[pltpu.VMEM((tm + 16, 256), f32)],
        compiler_params=_cp(("parallel",)),
        name="out_proj",
    )(x2, ofox, onsa, u, u, wo, wp, ps, g1, gate)


def _outproj_dec_kernel(x_ref, ofox_ref, onsa_ref, ext_ref, wo_ref, wp_ref, ps_ref, g_ref, gate_ref, o_ref, *, past):
    ext = ext_ref[...]
    n = ext.shape[1]
    r = lax.broadcasted_iota(i32, ext.shape, 1)
    lane = lax.broadcasted_iota(i32, (ext.shape[0], C_POOL), 1)
    u_new = jnp.sum(jnp.where(r == n - 1, ext, 0.0), axis=1)
    d = jnp.zeros(u_new.shape, f32)
    for g, w in enumerate(POOL_WINDOWS):
        sw = jnp.sum(jnp.where(r >= n - w, ext, 0.0), axis=1)
        d = jnp.where(lane // POOL_GW == g, sw / float(min(w, past + 1)), d)
    opool = _pool_project(d - u_new, wp_ref, ps_ref)
    o_ref[...] = _out_tail(x_ref[...], ofox_ref[...], onsa_ref[...], opool, wo_ref, g_ref, gate_ref[...])


def _outproj_dec(x2, ofox, onsa, ext, wo, wp, ps, g1, gate, past):
    m = x2.shape[0]
    full = lambda a: pl.BlockSpec(a.shape, lambda i: (0,) * a.ndim)
    args = (x2, ofox, onsa, ext, wo, wp, ps, g1, gate)
    return pl.pallas_call(
        functools.partial(_outproj_dec_kernel, past=past),
        grid=(1,),
        in_specs=[full(a) for a in args],
        out_specs=pl.BlockSpec((m, D_MODEL), lambda i: (0, 0)),
        out_shape=jax.ShapeDtypeStruct((m, D_MODEL), f32),
        compiler_params=_cp(("arbitrary",)),
        name="out_proj_dec",
    )(*args)


def _mlp_kernel(x_ref, g2_ref, sc_ref, sh_ref, w1_ref, w2_ref, g3_ref, gate_ref, o_ref, *, tf):
    x = x_ref[...]
    h = (_rms(x, g2_ref[...]) * (1.0 + sc_ref[0]) + sh_ref[0]).astype(bf16)
    acc = jnp.zeros(x.shape, f32)
    for c in range(D_FF // tf):
        a = jnp.maximum(jnp.dot(h, w1_ref[:, c * tf:(c + 1) * tf], preferred_element_type=f32), 0.0)
        acc = acc + jnp.dot((a * a).astype(bf16), w2_ref[c * tf:(c + 1) * tf, :], preferred_element_type=f32)
    o_ref[...] = x + gate_ref[0] * _rms(acc, g3_ref[...])


def _mlp(x2, g2, sc, sh, w1, w2, g3, gate, *, tm, rows_per_mod):
    m = x2.shape[0]
    mod_rows = sc.shape[1]
    modmap = lambda i: ((i * tm) // rows_per_mod, 0, 0)
    const = lambda s: pl.BlockSpec(s, lambda i: (0, 0))
    mod = pl.BlockSpec((1, mod_rows, D_MODEL), modmap)
    return pl.pallas_call(
        functools.partial(_mlp_kernel, tf=512),
        grid=(m // tm,),
        in_specs=[pl.BlockSpec((tm, D_MODEL), lambda i: (i, 0)), const((1, D_MODEL)), mod, mod,
                  const((D_MODEL, D_FF)), const((D_FF, D_MODEL)), const((1, D_MODEL)), mod],
        out_specs=pl.BlockSpec((tm, D_MODEL), lambda i: (i, 0)),
        out_shape=jax.ShapeDtypeStruct((m, D_MODEL), f32),
        compiler_params=_cp(("parallel",)),
        name="mlp",
    )(x2, g2, sc, sh, w1, w2, g3, gate)


def _pad8(a):
    return jnp.concatenate([a, jnp.zeros((8 - a.shape[0], a.shape[1]), a.dtype)], axis=0)


def _by_group(rows):
    grp = lax.broadcasted_iota(i32, (8, rows.shape[1]), 0) // G_NSA
    return jnp.where(grp == 0, jnp.broadcast_to(rows[0:1], (8, rows.shape[1])),
                     jnp.broadcast_to(rows[1:2], (8, rows.shape[1])))


def _lane_to_sublane(row, offset, stride):
    sub = lax.broadcasted_iota(i32, (8, LANES), 0)
    lane = lax.broadcasted_iota(i32, (8, LANES), 1)
    return jnp.sum(jnp.where(lane == offset + stride * sub, jnp.broadcast_to(row, (8, LANES)), 0.0),
                   axis=1, keepdims=True)


def _fox_dec_kernel(pt_ref, fq_ref, misc_ref, bias_ref, knew_ref, *rest, pps):
    lf_refs = rest[:pps]
    kv_refs = rest[pps:2 * pps]
    o_ref, lfo_ref, m_scr, l_scr, acc_scr, c_scr = rest[2 * pps:]
    g = pl.program_id(1)
    n = pps * PAGE_SIZE
    q6f = _pad8(fq_ref[0])
    q6 = q6f.astype(bf16)
    lane = lax.broadcasted_iota(i32, (8, LANES), 1)

    @pl.when(g == 0)
    def _():
        lfrow = jnp.where(lane[0:1] < H_FOX, _log_sigmoid(misc_ref[0] + bias_ref[...]), 0.0)
        lfo_ref[0] = lfrow
        m_scr[...] = jnp.sum(q6f * _pad8(knew_ref[0, 0]), axis=1, keepdims=True)
        l_scr[...] = jnp.ones((8, 1), f32)
        acc_scr[...] = _pad8(knew_ref[0, 1])
        c_scr[...] = _lane_to_sublane(lfrow, 0, 1)

    carry = c_scr[...]
    biases = []
    for i in range(pps):
        lf = lf_refs[i][0, 0]
        suf = lf
        for sft in (1, 2, 4, 8, 16, 32, 64):
            suf = suf + jnp.where(lane + sft < LANES, pltpu.roll(suf, LANES - sft, 1), 0.0)
        biases.append(carry + (suf - lf))
        carry = carry + suf[:, 0:1]
    c_scr[...] = carry
    s = jnp.concatenate(biases, axis=1)
    sub = lax.broadcasted_iota(i32, (8, n), 0)
    for h in range(H_FOX):
        kht = jnp.concatenate([kv_refs[i][0, 0, 0, h] for i in range(pps)], axis=1).astype(bf16)
        s = jnp.where(sub == h, s + jnp.dot(q6, kht, preferred_element_type=f32), s)
    m = m_scr[...]
    m_new = jnp.maximum(m, jnp.max(s, axis=1, keepdims=True))
    alpha = jnp.exp(m - m_new)
    p = jnp.exp(s - m_new)
    l_scr[...] = alpha * l_scr[...] + jnp.sum(p, axis=1, keepdims=True)
    m_scr[...] = m_new
    pb = p.astype(bf16)
    sub_d = lax.broadcasted_iota(i32, (8, HEAD_DIM), 0)
    pv = jnp.zeros((8, HEAD_DIM), f32)
    for h in range(H_FOX):
        vht = jnp.concatenate([kv_refs[i][0, 0, 1, h] for i in range(pps)], axis=1).astype(bf16)
        pv = jnp.where(sub_d == h, _nt(pb, vht), pv)
    acc_scr[...] = alpha * acc_scr[...] + pv

    @pl.when(g == pl.num_programs(1) - 1)
    def _():
        o_ref[0] = acc_scr[...] / l_scr[...]


def _fox_decode(layer, page_table, fq, misc, bias_row, fkv_new, lf_t, cache_kv):
    b, n_pages = page_table.shape
    pps = FOX_PPS
    ng = n_pages // pps
    row3 = lambda w_: pl.BlockSpec((1, 1, w_), lambda i, g, pt: (i, 0, 0))

    def page(i, shape):
        zeros = (0,) * len(shape)
        return pl.BlockSpec((1, 1) + shape,
                            lambda bi, g, pt, i=i: (layer, pt[jnp.minimum(bi, b - 1),
                                                               n_pages - 1 - (jnp.minimum(g, ng - 1) * pps + i)]) + zeros)

    grid_spec = pltpu.PrefetchScalarGridSpec(
        num_scalar_prefetch=1,
        grid=(b, ng),
        in_specs=[pl.BlockSpec((1, H_FOX, HEAD_DIM), lambda i, g, pt: (i, 0, 0)), row3(LANES),
                  pl.BlockSpec((1, LANES), lambda i, g, pt: (0, 0)),
                  pl.BlockSpec((1, 2, H_FOX, HEAD_DIM), lambda i, g, pt: (i, 0, 0, 0))]
                 + [page(i, (8, PAGE_SIZE)) for i in range(pps)]
                 + [page(i, (2, H_FOX, HEAD_DIM, PAGE_SIZE)) for i in range(pps)],
        out_specs=[pl.BlockSpec((1, 8, HEAD_DIM), lambda i, g, pt: (i, 0, 0)), row3(LANES)],
        scratch_shapes=[pltpu.VMEM((8, 1), f32), pltpu.VMEM((8, 1), f32), pltpu.VMEM((8, HEAD_DIM), f32),
                        pltpu.VMEM((8, 1), f32)],
    )
    return pl.pallas_call(
        functools.partial(_fox_dec_kernel, pps=pps),
        grid_spec=grid_spec,
        out_shape=[jax.ShapeDtypeStruct((b, 8, HEAD_DIM), f32), jax.ShapeDtypeStruct((b, 1, LANES), f32)],
        compiler_params=_cp(("parallel", "arbitrary")),
        name="fox_decode",
    )(page_table, fq, misc, bias_row, fkv_new, *([lf_t] * pps), *([cache_kv] * pps))


def _nsa_cmp_dec_kernel(pt_ref, q_ref, *rest, pps, past):
    pg_refs = rest[:pps]
    p_ref, o_ref, mean_scr = rest[pps:]
    g = pl.program_id(1)
    per_page = PAGE_SIZE // CMP_BLOCK
    rows = 2 * KV_NSA * HEAD_DIM
    x = jnp.concatenate([pg_refs[i][0, 0].reshape(rows, PAGE_SIZE) for i in range(pps)], axis=1)
    r = lax.broadcasted_iota(i32, (pps * PAGE_SIZE, LANES), 0)
    c = lax.broadcasted_iota(i32, (pps * PAGE_SIZE, LANES), 1)
    avg = jnp.where(r // CMP_BLOCK == c, 1.0 / CMP_BLOCK, 0.0).astype(bf16)
    hi = x.astype(bf16)
    lo = (x - hi.astype(f32)).astype(bf16)
    mean_scr[g] = jnp.dot(hi, avg, preferred_element_type=f32) + jnp.dot(lo, avg, preferred_element_type=f32)

    @pl.when(g == pl.num_programs(1) - 1)
    def _():
        ng = mean_scr.shape[0]
        nbc = ng * pps * per_page
        mean = mean_scr[0]
        for gg in range(1, ng):
            mean = mean + pltpu.roll(mean_scr[gg], gg * pps * per_page, 1)
        mean = mean[:, :nbc].astype(bf16)
        part = lambda feat, kv: mean[(feat * KV_NSA + kv) * HEAD_DIM:(feat * KV_NSA + kv + 1) * HEAD_DIM]
        q6 = _pad8(q_ref[0]).astype(bf16)
        grp0 = lax.broadcasted_iota(i32, (8, 1), 0) // G_NSA == 0
        s = jnp.where(grp0, jnp.dot(q6, part(0, 0), preferred_element_type=f32),
                      jnp.dot(q6, part(0, 1), preferred_element_type=f32))
        j = lax.broadcasted_iota(i32, (8, nbc), 1)
        mask = ((j + 1) * CMP_BLOCK - 1) <= past
        s = jnp.where(mask, s, NEG)
        pe = jnp.where(mask, jnp.exp(s - jnp.max(s, axis=1, keepdims=True)), 0.0)
        den = jnp.sum(pe, axis=1, keepdims=True)
        p = pe / jnp.where(den > 0.0, den, 1.0)
        p_ref[0] = p
        pb = p.astype(bf16)
        o_ref[0] = jnp.where(grp0, _nt(pb, part(1, 0)), _nt(pb, part(1, 1)))


def _nsa_cmp_decode(layer, page_table, nq, cache_nsa, past):
    b, n_pages = page_table.shape
    pps = CMP_PPS
    nbc = past // CMP_BLOCK
    grid_spec = pltpu.PrefetchScalarGridSpec(
        num_scalar_prefetch=1,
        grid=(b, n_pages // pps),
        in_specs=[pl.BlockSpec((1, H_NSA, HEAD_DIM), lambda i, g, pt: (i, 0, 0))]
                 + [pl.BlockSpec((1, 1, 2, KV_NSA, HEAD_DIM, PAGE_SIZE),
                                 lambda bi, g, pt, i=i: (layer, pt[jnp.minimum(bi, b - 1),
                                                                   jnp.minimum(g, n_pages // pps - 1) * pps + i],
                                                         0, 0, 0, 0))
                    for i in range(pps)],
        out_specs=[pl.BlockSpec((1, 8, nbc), lambda i, g, pt: (i, 0, 0)),
                   pl.BlockSpec((1, 8, HEAD_DIM), lambda i, g, pt: (i, 0, 0))],
        scratch_shapes=[pltpu.VMEM((n_pages // pps, 2 * KV_NSA * HEAD_DIM, LANES), f32)],
    )
    return pl.pallas_call(
        functools.partial(_nsa_cmp_dec_kernel, pps=pps, past=past),
        grid_spec=grid_spec,
        out_shape=[jax.ShapeDtypeStruct((b, 8, nbc), f32), jax.ShapeDtypeStruct((b, 8, HEAD_DIM), f32)],
        compiler_params=_cp(("parallel", "arbitrary")),
        name="nsa_cmp_decode",
    )(page_table, nq, *([cache_nsa] * pps))


def _nsa_topk_dec_kernel(p_ref, idx_ref, *, past, width):
    b = p_ref.shape[0]
    nbc = p_ref.shape[2]
    nbs = -(-(past + 1) // SEL_BLOCK)
    imps = []
    for kv in range(KV_NSA):
        imps.append(p_ref[:, kv * G_NSA, :] + p_ref[:, kv * G_NSA + 1, :] + p_ref[:, kv * G_NSA + 2, :])
    imp = jnp.concatenate(imps, axis=0)
    imp = jnp.concatenate([imp, jnp.zeros((2 * b, width - nbc), f32)], axis=1)
    j = lax.broadcasted_iota(i32, (2 * b, width), 1)
    cur = past // SEL_BLOCK
    forced = (j == 0) | (j == cur) | (j == cur - 1)
    score = jnp.where(forced, FORCE, imp)
    score = jnp.where(j * SEL_BLOCK <= past, score, -1.0)
    score = jnp.where(j < nbs, score, -2.0)
    _, firsts = _select_topk(score, min(TOPK_BLOCKS, nbs))
    lane = lax.broadcasted_iota(i32, (2 * b, LANES), 1)
    out = jnp.zeros((2 * b, LANES), i32)
    for k, first in enumerate(firsts):
        out = jnp.where(lane == k, first.astype(i32), out)
    idx_ref[...] = out


def _nsa_topk_decode(p8, past):
    b, _, nbc = p8.shape
    width = 2 * nbc
    return pl.pallas_call(
        functools.partial(_nsa_topk_dec_kernel, past=past, width=width),
        grid=(1,),
        in_specs=[pl.BlockSpec(p8.shape, lambda i: (0, 0, 0))],
        out_specs=pl.BlockSpec((2 * b, LANES), lambda i: (0, 0)),
        out_shape=jax.ShapeDtypeStruct((2 * b, LANES), i32),
        compiler_params=_cp(("arbitrary",)),
        name="nsa_topk_decode",
    )(p8)


def _nsa_sel_dec_kernel(pt_ref, idx_ref, q_ref, misc_ref, ocmp_ref, nkv_ref, wnew_ref, win_ref, *rest, past, nsel):
    blk_refs = rest[:KV_NSA * nsel]
    o_ref = rest[KV_NSA * nsel]
    b = pl.program_id(0)
    n_past_blk = past // SEL_BLOCK
    q8f = _pad8(q_ref[0])
    q8 = q8f.astype(bf16)
    sub = lax.broadcasted_iota(i32, (8, 1), 0)
    grp0 = sub // G_NSA == 0
    per_page = PAGE_SIZE // SEL_BLOCK
    lane_blk = lax.broadcasted_iota(i32, (8, PAGE_SIZE), 1) // SEL_BLOCK
    m = jnp.full((8, 1), NEG, f32)
    l = jnp.zeros((8, 1), f32)
    acc = jnp.zeros((8, HEAD_DIM), f32)
    s_new = jnp.sum(q8f * _by_group(nkv_ref[0, 2]), axis=1, keepdims=True)
    v_new = _by_group(nkv_ref[0, 3])
    for kv in range(KV_NSA):
        in_group = (sub // G_NSA) == kv
        has_new = jnp.zeros((), jnp.bool_)
        for k in range(nsel):
            j = idx_ref[kv * pl.num_programs(0) + b, k]
            has_new = has_new | (j == n_past_blk)
            blk = blk_refs[kv * nsel + k]
            mask = in_group & (j < n_past_blk) & (lane_blk == j % per_page)
            s = jnp.where(mask, jnp.dot(q8, blk[0, 0, 0, 0].astype(bf16), preferred_element_type=f32), NEG)
            m_new = jnp.maximum(m, jnp.max(s, axis=1, keepdims=True))
            alpha = jnp.exp(m - m_new)
            pe = jnp.where(mask, jnp.exp(s - m_new), 0.0)
            l = alpha * l + jnp.sum(pe, axis=1, keepdims=True)
            acc = alpha * acc + _nt(pe.astype(bf16), blk[0, 0, 1, 0].astype(bf16))
            m = m_new
        new_ok = in_group & has_new
        m_new = jnp.maximum(m, jnp.where(new_ok, s_new, NEG))
        alpha = jnp.exp(m - m_new)
        p_new = jnp.where(new_ok, jnp.exp(s_new - m_new), 0.0)
        l = alpha * l + p_new
        acc = alpha * acc + p_new * v_new
        m = m_new
    o_sel = acc / jnp.where(l > 0.0, l, 1.0)
    wb = win_ref.shape[5]
    kpos = past - wb + lax.broadcasted_iota(i32, (8, wb), 1)
    rel = past - kpos
    wmask = (rel >= 0) & (rel < WINDOW) & (kpos >= 0)
    s = jnp.where(grp0, jnp.dot(q8, win_ref[0, 0, 0, 0].astype(bf16), preferred_element_type=f32),
                  jnp.dot(q8, win_ref[0, 0, 0, 1].astype(bf16), preferred_element_type=f32))
    s = jnp.where(wmask, s, NEG)
    s_w = jnp.sum(q8f * _by_group(wnew_ref[0, 0]), axis=1, keepdims=True)
    m = jnp.maximum(jnp.max(s, axis=1, keepdims=True), s_w)
    p = jnp.where(wmask, jnp.exp(s - m), 0.0)
    p_w = jnp.exp(s_w - m)
    pb = p.astype(bf16)
    pv = jnp.where(grp0, _nt(pb, win_ref[0, 0, 1, 0].astype(bf16)), _nt(pb, win_ref[0, 0, 1, 1].astype(bf16)))
    o_win = (pv + p_w * _by_group(wnew_ref[0, 1])) / (jnp.sum(p, axis=1, keepdims=True) + p_w)
    misc = misc_ref[0]
    gates = [jax.nn.sigmoid(_lane_to_sublane(misc, GATE0 + r, 3)) for r in range(3)]
    o_ref[0] = gates[0] * ocmp_ref[0] + gates[1] * o_sel + gates[2] * o_win


def _nsa_sel_decode(layer, page_table, idx, nq, misc, ocmp, nkv_new, win_new, cache_win, cache_nsa, past):
    b = page_table.shape[0]
    nsel = idx.shape[1]
    n_past_blk = past // SEL_BLOCK
    per_page = PAGE_SIZE // SEL_BLOCK
    wb = cache_win.shape[5]

    def blk_spec(kv, k):
        def imap(i, pt, ix):
            ii = jnp.minimum(i, b - 1)
            jp = jnp.clip(ix[kv * b + ii, k], 0, n_past_blk - 1)
            return (layer, pt[ii, jp // per_page], 1, kv, 0, 0)
        return pl.BlockSpec((1, 1, 2, 1, HEAD_DIM, PAGE_SIZE), imap)

    grid_spec = pltpu.PrefetchScalarGridSpec(
        num_scalar_prefetch=2,
        grid=(b,),
        in_specs=[pl.BlockSpec((1, H_NSA, HEAD_DIM), lambda i, pt, ix: (i, 0, 0)),
                  pl.BlockSpec((1, 1, LANES), lambda i, pt, ix: (i, 0, 0)),
                  pl.BlockSpec((1, 8, HEAD_DIM), lambda i, pt, ix: (i, 0, 0)),
                  pl.BlockSpec((1, 4, KV_NSA, HEAD_DIM), lambda i, pt, ix: (i, 0, 0, 0)),
                  pl.BlockSpec((1, 2, KV_NSA, HEAD_DIM), lambda i, pt, ix: (i, 0, 0, 0)),
                  pl.BlockSpec((1, 1, 2, KV_NSA, HEAD_DIM, wb), lambda i, pt, ix: (layer, i, 0, 0, 0, 0))]
                 + [blk_spec(kv, k) for kv in range(KV_NSA) for k in range(nsel)],
        out_specs=pl.BlockSpec((1, 8, HEAD_DIM), lambda i, pt, ix: (i, 0, 0)),
    )
    return pl.pallas_call(
        functools.partial(_nsa_sel_dec_kernel, past=past, nsel=nsel),
        grid_spec=grid_spec,
        out_shape=jax.ShapeDtypeStruct((b, 8, HEAD_DIM), f32),
        compiler_params=_cp(("parallel",)),
        name="nsa_sel_decode",
    )(page_table, idx, nq, misc, ocmp, nkv_new, win_new, cache_win, *([cache_nsa] * (KV_NSA * nsel)))


def _proj_columns():
    off = np.cumsum([0, 384, 384, 384, 6, 384, 128, 128, 128, 128, 128, 128, 18, 256])
    fq, fk, fv, ff, nq, nkc, nvc, nks, nvs, nkw, nvw, ng, u = [int(o) for o in off[:13]]
    cols = list(range(fq, fq + 384))
    for h in NSA_PAIR_ORDER:
        cols += list(range(nq + h * HEAD_DIM, nq + (h + 1) * HEAD_DIM))
    cols += list(range(fk, fk + 768))
    cols += list(range(nkc, nkc + 512))
    cols += list(range(nkw, nkw + 256))
    cols += list(range(u, u + 256))
    cols += list(range(ff, ff + 6)) + list(range(ng, ng + 18)) + [N_IN] * (LANES - 24)
    return np.asarray(cols, np.int32)


def _out_rows():
    rows = list(range(0, 384))
    for h in NSA_PAIR_ORDER:
        rows += list(range(384 + h * HEAD_DIM, 384 + (h + 1) * HEAD_DIM))
    rows += list(range(768, 1024))
    return np.asarray(rows, np.int32)


def _rope_tables(pos):
    half = ROT_DIM // 2
    inv = ROPE_THETA ** (-jnp.arange(0, ROT_DIM, 2, dtype=f32) / ROT_DIM)
    ang = pos.astype(f32)[:, None] * inv[None, :]
    cos, sin = jnp.cos(ang), jnp.sin(ang)
    n = pos.shape[0]
    one = jnp.ones((n, HEAD_DIM - ROT_DIM), f32)
    zero8 = jnp.zeros((n, half), f32)
    zrest = jnp.zeros((n, HEAD_DIM - ROT_DIM), f32)
    c = jnp.concatenate([cos, cos, one], axis=1)
    sa = jnp.concatenate([-sin, zero8, zrest], axis=1)
    sb = jnp.concatenate([zero8, sin, zrest], axis=1)
    return tuple(jnp.concatenate([a, a], axis=1) for a in (c, sa, sb))


def kernel(x_prompt, x_sample, cache_fox_kv, cache_fox_logf, cache_nsa_kv, cache_nsa_win, state_pool, page_table,
           c_prompt, c_sample, w_ada, b_ada, norm_g, w_in, b_fox_f, w_out, w_pool, pool_scale, w_ff1, w_ff2):
    depth = w_in.shape[0]
    bp, t, _ = x_prompt.shape
    bs = x_sample.shape[0]
    past = page_table.shape[1] * PAGE_SIZE
    mp = bp * t

    cols = _proj_columns()
    w_in_p = jnp.concatenate([w_in, jnp.zeros((depth, D_MODEL, 1), f32)], axis=2)[:, :, cols].astype(bf16)
    w_out_p = w_out[:, _out_rows(), :].astype(bf16)
    w_pool_bd = jnp.zeros((depth, C_POOL, C_POOL), f32)
    for g in range(len(POOL_WINDOWS)):
        w_pool_bd = w_pool_bd.at[:, g * POOL_GW:(g + 1) * POOL_GW, g * POOL_GW:(g + 1) * POOL_GW].set(w_pool[:, g])
    w_pool_bd = w_pool_bd.astype(bf16)
    w1 = w_ff1.astype(bf16)
    w2 = w_ff2.astype(bf16)
    bias_rows = jnp.pad(b_fox_f, ((0, 0), (0, LANES - H_FOX))).reshape(depth, 1, LANES)

    rope_p = _rope_tables(jnp.arange(t))
    rope_s = _rope_tables(jnp.full((bs,), past, i32))

    to_last = (0, 1, 3, 4, 5, 2)
    fox_t = jnp.transpose(cache_fox_kv, to_last)
    nsa_t = jnp.transpose(cache_nsa_kv, to_last)
    win_t = jnp.transpose(cache_nsa_win, to_last)
    lf_t = jnp.pad(jnp.swapaxes(cache_fox_logf, 2, 3), ((0, 0), (0, 0), (0, 8 - H_FOX), (0, 0)))
    nat_from_pair = np.argsort(np.asarray(NSA_PAIR_ORDER))

    mod = _ada(jnp.concatenate([c_prompt, c_sample], axis=0), w_ada, b_ada)
    mod = mod.reshape(depth, bp + bs, 6, D_MODEL)

    yp = x_prompt.reshape(mp, D_MODEL)
    ys = x_sample.reshape(bs, D_MODEL)
    sp, ss = [], []
    for l in range(depth):
        g = norm_g[l].reshape(4, 1, D_MODEL)
        modp = [mod[l, :bp, k].reshape(bp, 1, D_MODEL) for k in range(6)]
        mods = [mod[l, bp:, k].reshape(1, bs, D_MODEL) for k in range(6)]

        (fq, nq, fkv, nkv, win, u, misc, fkvb, nkvb, winb, kcm) = _inproj(
            yp, g[0], modp[1], modp[0], w_in_p[l], *rope_p, tm=512, rows_per_mod=t, rope_rows=t, with_means=True)
        logf, ccol, crow = _foxprep(misc, bias_rows[l], bp, t)
        o_fox = _fox_attn(fq, fkvb, ccol, crow, bp, t)
        o_nsa = _nsa_attn(nq, kcm, nkvb, winb, misc, bp, t)
        y1 = _outproj(yp, o_fox, o_nsa, u, w_out_p[l], w_pool_bd[l], pool_scale[l].reshape(1, C_POOL), g[1],
                      modp[2], tm=512, t=t)
        yp = _mlp(y1, g[2], modp[4], modp[3], w1[l], w2[l], g[3], modp[5], tm=512, rows_per_mod=t)
        wl = min(WINDOW, t)
        sp.append((fkv.reshape(bp, t, 2, H_FOX, HEAD_DIM), logf.reshape(bp, t, H_FOX),
                   nkv.reshape(bp, t, 4, KV_NSA, HEAD_DIM),
                   win.reshape(bp, t, 2, KV_NSA, HEAD_DIM)[:, t - wl:],
                   u.reshape(bp, t, C_POOL)[:, t - POOL_STATE:]))

        (fq_s, nq_s, fkv_s, nkv_s, win_s, u_s, misc_s, _, _, _, _) = _inproj(
            ys, g[0], mods[1], mods[0], w_in_p[l], *rope_s, tm=bs, rows_per_mod=bs, rope_rows=bs, with_means=False)
        r3 = lambda a: a.reshape(bs, 1, a.shape[-1])
        fq6 = fq_s.astype(f32).reshape(bs, H_FOX, HEAD_DIM)
        nq6 = nq_s.astype(f32).reshape(bs, H_NSA, HEAD_DIM)[:, nat_from_pair]
        nkv4 = nkv_s.reshape(bs, 4, KV_NSA, HEAD_DIM)
        win2 = win_s.reshape(bs, 2, KV_NSA, HEAD_DIM)
        o_fox_s, logf_s = _fox_decode(l, page_table, fq6, r3(misc_s), bias_rows[l],
                                      fkv_s.reshape(bs, 2, H_FOX, HEAD_DIM), lf_t, fox_t)
        p8, o_cmp = _nsa_cmp_decode(l, page_table, nq6, nsa_t, past)
        idx = _nsa_topk_decode(p8, past)[:, :min(TOPK_BLOCKS, -(-(past + 1) // SEL_BLOCK))]
        o_nsa_s = _nsa_sel_decode(l, page_table, idx, nq6, r3(misc_s), o_cmp, nkv4, win2,
                                  win_t, nsa_t, past)
        o_fox_s = o_fox_s[:, :H_FOX].reshape(bs, 384).astype(bf16)
        o_nsa_s = o_nsa_s[:, np.asarray(NSA_PAIR_ORDER)].reshape(bs, 384).astype(bf16)
        u_ext = jnp.concatenate([state_pool[l], u_s.reshape(bs, 1, C_POOL)], axis=1)
        y1s = _outproj_dec(ys, o_fox_s, o_nsa_s, u_ext, w_out_p[l],
                           w_pool_bd[l], pool_scale[l].reshape(1, C_POOL), g[1], mods[2][0], past)
        ys = _mlp(y1s, g[2], mods[4], mods[3], w1[l], w2[l], g[3], mods[5], tm=bs, rows_per_mod=bs)
        win_all = jnp.concatenate([cache_nsa_win[l], win2[:, None]], axis=1)
        ss.append((fkv_s.reshape(bs, 1, 2, H_FOX, HEAD_DIM), logf_s[:, :, :H_FOX], nkv4[:, None],
                   win_all[:, 1:], u_ext[:, 1:]))

    stk = lambda lst, i: jnp.stack([s[i] for s in lst], axis=0)
    return (yp.reshape(bp, t, D_MODEL), ys.reshape(bs, 1, D_MODEL),
            stk(sp, 0), stk(sp, 1), stk(sp, 2), stk(sp, 3), stk(sp, 4),
            stk(ss, 0), stk(ss, 1), stk(ss, 2), stk(ss, 3), stk(ss, 4))
```

```python
import functools

import numpy as np
import jax
import jax.numpy as jnp
from jax import lax
from jax.experimental import pallas as pl
from jax.experimental.pallas import tpu as pltpu

f32 = jnp.float32
bf16 = jnp.bfloat16
i32 = jnp.int32

D_MODEL = 1024
HEAD_DIM = 64
H_FOX = 6
H_NSA = 6
KV_NSA = 2
G_NSA = H_NSA // KV_NSA
POOL_WINDOWS = (2, 4, 8, 16)
C_POOL = 256
POOL_GW = 64
POOL_STATE = 15
ROT_DIM = 16
ROPE_THETA = 500000.0
CMP_BLOCK = 64
SEL_BLOCK = 64
TOPK_BLOCKS = 8
WINDOW = 512
PAGE_SIZE = 128
D_FF = 4 * D_MODEL
EPS = 1e-6
NEG = -1e30
FORCE = 1e4
SCALE = HEAD_DIM ** -0.5
N_IN = 2584

LANES = 128
HALF = LANES // 2
VMEM_LIMIT = 56 * 1024 * 1024

C_FQ = 0
C_NQ = 384
C_FKV = 768
C_NKV = 1536
C_WIN = 2048
C_U = 2304
C_MISC = 2560
N_PROJ = 2688
GATE0 = H_FOX
NSA_PAIR_ORDER = (0, 3, 1, 4, 2, 5)

FOX_TQ = 512
FOX_TK = 512
NSA_TQ = 256
NSA_TK = 512
FOX_PPS = 16
CMP_PPS = 16


def _cp(sem):
    return pltpu.CompilerParams(dimension_semantics=sem, vmem_limit_bytes=VMEM_LIMIT)


def _nt(a, b):
    return lax.dot_general(a, b, (((1,), (1,)), ((), ())), preferred_element_type=f32)


def _rms(x, g):
    return x * lax.rsqrt(jnp.mean(x * x, axis=-1, keepdims=True) + EPS) * g


def _log_sigmoid(x):
    return jnp.minimum(x, 0.0) - jnp.log1p(jnp.exp(-jnp.abs(x)))


def _ada_kernel(c_ref, w_ref, b_ref, o_ref):
    c = c_ref[...]
    a = (c * jax.nn.sigmoid(c)).astype(bf16)
    o_ref[0] = jnp.dot(a, w_ref[0].astype(bf16), preferred_element_type=f32) + b_ref[0]


def _ada(c_all, w_ada, b_ada):
    depth = w_ada.shape[0]
    n = c_all.shape[0]
    tn = 1024
    return pl.pallas_call(
        _ada_kernel,
        grid=(depth, 6 * D_MODEL // tn),
        in_specs=[pl.BlockSpec((n, D_MODEL), lambda l, j: (0, 0)),
                  pl.BlockSpec((1, D_MODEL, tn), lambda l, j: (l, 0, j)),
                  pl.BlockSpec((1, 1, tn), lambda l, j: (l, 0, j))],
        out_specs=pl.BlockSpec((1, n, tn), lambda l, j: (l, 0, j)),
        out_shape=jax.ShapeDtypeStruct((depth, n, 6 * D_MODEL), f32),
        compiler_params=_cp(("parallel", "parallel")),
        name="ada_mod",
    )(c_all, w_ada, b_ada.reshape(depth, 1, 6 * D_MODEL))


def _inproj_kernel(x_ref, g_ref, sc_ref, sh_ref, w_ref, cos_ref, sa_ref, sb_ref,
                   fq_ref, nq_ref, fkv_ref, nkv_ref, win_ref, u_ref, misc_ref,
                   fkvb_ref, nkvb_ref, winb_ref, kcm_ref, *, tm, with_means):
    x = x_ref[...]
    h = (_rms(x, g_ref[...]) * (1.0 + sc_ref[0]) + sh_ref[0]).astype(bf16)
    cos = cos_ref[...]
    sa = sa_ref[...]
    sb = sb_ref[...]

    def rope(z):
        return z * cos + pltpu.roll(z, LANES - ROT_DIM // 2, 1) * sa + pltpu.roll(z, ROT_DIM // 2, 1) * sb

    def means(z):
        return jnp.sum(z.reshape(tm // CMP_BLOCK, CMP_BLOCK, LANES), axis=1) * (1.0 / CMP_BLOCK)

    def emit(blk, z):
        c = blk * LANES
        if c < C_NQ:
            fq_ref[:, c - C_FQ:c - C_FQ + LANES] = (z * SCALE).astype(bf16)
        elif c < C_FKV:
            nq_ref[:, c - C_NQ:c - C_NQ + LANES] = (rope(z) * SCALE).astype(bf16)
        elif c < C_NKV:
            fkv_ref[:, c - C_FKV:c - C_FKV + LANES] = z
            fkvb_ref[:, c - C_FKV:c - C_FKV + LANES] = z.astype(bf16)
        elif c < C_WIN:
            o = c - C_NKV
            if o in (0, 2 * LANES):
                z = rope(z)
            nkv_ref[:, o:o + LANES] = z
            nkvb_ref[:, o:o + LANES] = z.astype(bf16)
            if with_means and o < 2 * LANES:
                kcm_ref[:, o:o + LANES] = means(z)
        elif c < C_U:
            o = c - C_WIN
            if o == 0:
                z = rope(z)
            win_ref[:, o:o + LANES] = z
            winb_ref[:, o:o + LANES] = z.astype(bf16)
        elif c < C_MISC:
            u_ref[:, c - C_U:c - C_U + LANES] = z
        else:
            misc_ref[...] = z

    nblk = N_PROJ // LANES
    for b0 in range(0, nblk, 2):
        nb = min(2, nblk - b0)
        z2 = jnp.dot(h, w_ref[:, b0 * LANES:(b0 + nb) * LANES], preferred_element_type=f32)
        for k in range(nb):
            emit(b0 + k, z2[:, k * LANES:(k + 1) * LANES])
    if not with_means:
        kcm_ref[...] = jnp.zeros(kcm_ref.shape, f32)


def _inproj(x2, g0, sc, sh, w, cos, sa, sb, *, tm, rows_per_mod, rope_rows, with_means):
    m = x2.shape[0]
    nt = m // tm
    mod_rows = sc.shape[1]
    rope_blocks = rope_rows // tm
    kc_rows = max(tm // CMP_BLOCK, 8)
    modmap = lambda i: ((i * tm) // rows_per_mod, 0, 0)
    ropemap = lambda i: (i % rope_blocks, 0)
    row = lambda w_: pl.BlockSpec((tm, w_), lambda i: (i, 0))
    outs = [((m, 384), bf16), ((m, 384), bf16), ((m, 768), f32), ((m, 512), f32), ((m, 256), f32),
            ((m, 256), f32), ((m, LANES), f32), ((m, 768), bf16), ((m, 512), bf16), ((m, 256), bf16)]
    out_shape = [jax.ShapeDtypeStruct(s, d) for s, d in outs] + [jax.ShapeDtypeStruct((nt * kc_rows, 256), f32)]
    out_specs = [row(s[1]) for s, _ in outs] + [pl.BlockSpec((kc_rows, 256), lambda i: (i, 0))]
    return pl.pallas_call(
        functools.partial(_inproj_kernel, tm=tm, with_means=with_means),
        grid=(nt,),
        in_specs=[row(D_MODEL),
                  pl.BlockSpec((1, D_MODEL), lambda i: (0, 0)),
                  pl.BlockSpec((1, mod_rows, D_MODEL), modmap),
                  pl.BlockSpec((1, mod_rows, D_MODEL), modmap),
                  pl.BlockSpec((D_MODEL, N_PROJ), lambda i: (0, 0)),
                  pl.BlockSpec((tm, LANES), ropemap),
                  pl.BlockSpec((tm, LANES), ropemap),
                  pl.BlockSpec((tm, LANES), ropemap)],
        out_specs=out_specs,
        out_shape=out_shape,
        compiler_params=_cp(("parallel",)),
        name="in_proj",
    )(x2, g0, sc, sh, w, cos, sa, sb)


def _foxprep_kernel(misc_ref, bias_ref, lf_ref, ccol_ref, crow_ref, *, t):
    blk = 256
    lane = lax.broadcasted_iota(i32, (t, LANES), 1)
    lf = jnp.where(lane < H_FOX, _log_sigmoid(misc_ref[...] + bias_ref[...]), 0.0)
    lf_ref[...] = lf[:, :H_FOX]
    r = lax.broadcasted_iota(i32, (blk, blk), 0)
    c = lax.broadcasted_iota(i32, (blk, blk), 1)
    ltri = (r >= c).astype(f32)
    carry = jnp.zeros((1, LANES), f32)
    for b in range(t // blk):
        cb = jnp.dot(ltri, lf[b * blk:(b + 1) * blk], preferred_element_type=f32,
                     precision=lax.Precision.HIGHEST) + carry
        ccol_ref[b * blk:(b + 1) * blk, :] = cb
        carry = cb[blk - 1:blk, :]
    er = lax.broadcasted_iota(i32, (8, LANES), 0)
    ec = lax.broadcasted_iota(i32, (8, LANES), 1)
    eye = (er == ec).astype(f32)
    crow_ref[0] = lax.dot_general(eye, ccol_ref[...], (((1,), (1,)), ((), ())),
                                  preferred_element_type=f32, precision=lax.Precision.HIGHEST)


def _foxprep(misc, bias_row, b, t):
    return pl.pallas_call(
        functools.partial(_foxprep_kernel, t=t),
        grid=(b,),
        in_specs=[pl.BlockSpec((t, LANES), lambda i: (i, 0)),
                  pl.BlockSpec((1, LANES), lambda i: (0, 0))],
        out_specs=[pl.BlockSpec((t, H_FOX), lambda i: (i, 0)),
                   pl.BlockSpec((t, LANES), lambda i: (i, 0)),
                   pl.BlockSpec((1, 8, t), lambda i: (i, 0, 0))],
        out_shape=[jax.ShapeDtypeStruct((b * t, H_FOX), f32),
                   jax.ShapeDtypeStruct((b * t, LANES), f32),
                   jax.ShapeDtypeStruct((b, 8, t), f32)],
        compiler_params=_cp(("parallel",)),
        name="fox_prep",
    )(misc, bias_row)


def _tn(a, b):
    return lax.dot_general(a, b, (((0,), (0,)), ((), ())), preferred_element_type=f32)


def _softmax_step_t(carry, st, vb):
    m, l, acc = carry
    m_new = jnp.maximum(m, jnp.max(st, axis=0, keepdims=True))
    alpha = jnp.exp(m - m_new)
    p = jnp.exp(st - m_new)
    l = alpha * l + jnp.sum(p, axis=0, keepdims=True)
    acc = alpha * acc + _tn(vb, p.astype(bf16))
    return m_new, l, acc


def _fox_kernel(q_ref, kv_ref, ccol_ref, crow_ref, o_ref, *, tq, tk):
    qi = pl.program_id(1)
    q0 = pl.multiple_of(qi * tq, tq)
    lo = lax.broadcasted_iota(i32, (tq, LANES), 1) < HALF
    top = lax.broadcasted_iota(i32, (LANES, tq), 0) < HALF
    kk = lax.broadcasted_iota(i32, (tk, tq), 0)
    qq = lax.broadcasted_iota(i32, (tk, tq), 1)
    diag = kk <= qq
    diag2 = jnp.concatenate([diag, diag], axis=1)
    zero = jnp.zeros((), bf16)
    for p in range(H_FOX // 2):
        q = q_ref[:, p * LANES:(p + 1) * LANES]
        qst = jnp.concatenate([jnp.where(lo, q, zero), jnp.where(lo, zero, q)], axis=0)
        cq0 = crow_ref[0, 2 * p:2 * p + 1, pl.ds(q0, tq)]
        cq1 = crow_ref[0, 2 * p + 1:2 * p + 2, pl.ds(q0, tq)]

        def chunk(c, carry, mask, p=p, qst=qst, cq0=cq0, cq1=cq1):
            k0 = pl.multiple_of(c * tk, tk)
            kb = kv_ref[pl.ds(k0, tk), p * LANES:(p + 1) * LANES]
            vb = kv_ref[pl.ds(k0, tk), 384 + p * LANES:384 + (p + 1) * LANES]
            ck0 = ccol_ref[pl.ds(k0, tk), 2 * p:2 * p + 1]
            ck1 = ccol_ref[pl.ds(k0, tk), 2 * p + 1:2 * p + 2]
            st = _nt(kb, qst) + jnp.concatenate([cq0 - ck0, cq1 - ck1], axis=1)
            if mask is not None:
                st = jnp.where(mask, st, NEG)
            return _softmax_step_t(carry, st, vb)

        init = (jnp.full((1, 2 * tq), NEG, f32), jnp.zeros((1, 2 * tq), f32), jnp.zeros((LANES, 2 * tq), f32))
        carry = lax.fori_loop(0, qi, lambda c, cr: chunk(c, cr, None), init)
        _, l, acc = chunk(qi, carry, diag2)
        ot = acc / l
        o_ref[:, p * LANES:(p + 1) * LANES] = jnp.where(top, ot[:, :tq], ot[:, tq:]).T.astype(bf16)


def _fox_attn(fq, fkvb, ccol, crow, b, t):
    tq, tk = FOX_TQ, FOX_TK
    assert tq == tk
    nq = t // tq
    return pl.pallas_call(
        functools.partial(_fox_kernel, tq=tq, tk=tk),
        grid=(b, nq),
        in_specs=[pl.BlockSpec((tq, 384), lambda i, j: (i * nq + j, 0)),
                  pl.BlockSpec((t, 768), lambda i, j: (i, 0)),
                  pl.BlockSpec((t, LANES), lambda i, j: (i, 0)),
                  pl.BlockSpec((1, 8, t), lambda i, j: (i, 0, 0))],
        out_specs=pl.BlockSpec((tq, 384), lambda i, j: (i * nq + j, 0)),
        out_shape=jax.ShapeDtypeStruct((b * t, 384), bf16),
        compiler_params=_cp(("parallel", "parallel")),
        name="fox_attn",
    )(fq, fkvb, ccol, crow)


def _select_topk(score, nsel, axis=1):
    nb = score.shape[axis]
    jf = lax.broadcasted_iota(i32, score.shape, axis).astype(f32)
    sel = jnp.zeros(score.shape, f32)
    firsts = []
    for _ in range(nsel):
        mx = jnp.max(score, axis=axis, keepdims=True)
        first = jnp.min(jnp.where(score == mx, jf, float(nb)), axis=axis, keepdims=True)
        hit = jf == first
        sel = jnp.where(hit, 1.0, sel)
        score = jnp.where(hit, -3.0, score)
        firsts.append(first)
    return sel, firsts


def _nsa_kernel(q_ref, kcm_ref, nkv_ref, win_ref, misc_ref, o_ref, *, tq, tk, t):
    qi = pl.program_id(1)
    s0 = qi * tq
    nbc = t // CMP_BLOCK
    lane = lax.broadcasted_iota(i32, (tq, LANES), 1)
    lo = lane < HALF
    zero = jnp.zeros((), bf16)
    kpos = lax.broadcasted_iota(i32, (tk, tq), 0)
    qpos = lax.broadcasted_iota(i32, (tk, tq), 1) + s0
    qpos_b = lax.broadcasted_iota(i32, (nbc, tq), 1) + s0
    jb = lax.broadcasted_iota(i32, (nbc, tq), 0)
    gates = jax.nn.sigmoid(misc_ref[...].T)

    def tile3(a):
        return jnp.concatenate([a, a, a], axis=1)

    def per_head(ot):
        return [ot[:, g * tq:(g + 1) * tq] for g in range(G_NSA)]

    def attend(qk, kref, klane, vlane, c_lo, c_hi, biasfn):
        def body(c, carry, last):
            k0 = pl.multiple_of(c * tk, tk)
            kb = kref[pl.ds(k0, tk), klane:klane + LANES]
            vb = kref[pl.ds(k0, tk), vlane:vlane + LANES]
            return _softmax_step_t(carry, _nt(kb, qk) + tile3(biasfn(k0, last)), vb)
        init = (jnp.full((1, 3 * tq), NEG, f32), jnp.zeros((1, 3 * tq), f32), jnp.zeros((LANES, 3 * tq), f32))
        carry = lax.fori_loop(c_lo, c_hi - 1, lambda c, cr: body(c, cr, False), init)
        _, l, acc = body(c_hi - 1, carry, True)
        return per_head(acc / l)

    kcm = kcm_ref[...]
    kc = kcm[:, :LANES].astype(bf16)
    vc = kcm[:, LANES:].astype(bf16)
    outs = []
    for kv in range(KV_NSA):
        keep = lo if kv == 0 else jnp.logical_not(lo)
        qk = jnp.concatenate([jnp.where(keep, q_ref[:, p * LANES:(p + 1) * LANES], zero)
                              for p in range(G_NSA)], axis=0)
        cmask3 = tile3(((jb + 1) * CMP_BLOCK - 1) <= qpos_b)
        sc = jnp.where(cmask3, _nt(kc, qk), NEG)
        mx = jnp.max(sc, axis=0, keepdims=True)
        pe = jnp.where(cmask3, jnp.exp(sc - mx), 0.0)
        den = jnp.sum(pe, axis=0, keepdims=True)
        pc = pe / jnp.where(den > 0.0, den, 1.0)
        o_cmp = per_head(_tn(vc, pc.astype(bf16)))
        imp = pc[:, :tq] + pc[:, tq:2 * tq] + pc[:, 2 * tq:]
        cur = qpos_b // SEL_BLOCK
        forced = (jb == 0) | (jb == cur) | (jb == cur - 1)
        score = jnp.where(forced, FORCE, imp)
        score = jnp.where(jb * SEL_BLOCK <= qpos_b, score, -1.0)
        sel, _ = _select_topk(score, min(TOPK_BLOCKS, nbc), axis=0)
        selt = jnp.concatenate([sel, jnp.zeros((LANES - nbc, tq), f32)], axis=0).astype(bf16)

        def sel_bias(k0, last, selt=selt):
            er = (lax.broadcasted_iota(i32, (tk, LANES), 0) + k0) // SEL_BLOCK
            ec = lax.broadcasted_iota(i32, (tk, LANES), 1)
            expand = jnp.where(er == ec, 1.0, 0.0).astype(bf16)
            picked = jnp.dot(expand, selt, preferred_element_type=f32)
            bias = (picked - 1.0) * (-NEG)
            return jnp.where((kpos + k0) <= qpos, bias, NEG) if last else bias

        def win_bias(k0, last):
            rel = qpos - (kpos + k0)
            return jnp.where(rel >= 0, jnp.where(rel < WINDOW, 0.0, NEG), NEG)

        o_sel = attend(qk, nkv_ref, 2 * LANES, 3 * LANES, 0, (s0 + tq + tk - 1) // tk, sel_bias)
        w_lo = jnp.maximum(s0 - WINDOW + 1, 0) // tk
        o_win = attend(qk, win_ref, 0, LANES, w_lo, (s0 + tq + tk - 1) // tk, win_bias)
        for g in range(G_NSA):
            r = GATE0 + 3 * (kv * G_NSA + g)
            outs.append(gates[r:r + 1] * o_cmp[g] + gates[r + 1:r + 2] * o_sel[g] + gates[r + 2:r + 3] * o_win[g])
    top = lax.broadcasted_iota(i32, (LANES, tq), 0) < HALF
    for p in range(G_NSA):
        o_ref[:, p * LANES:(p + 1) * LANES] = jnp.where(top, outs[p], outs[G_NSA + p]).T.astype(bf16)


def _nsa_attn(nq, kcm, nkvb, winb, misc, b, t):
    tq, tk = NSA_TQ, NSA_TK
    assert tk % tq == 0
    nq_t = t // tq
    nbc = t // CMP_BLOCK
    return pl.pallas_call(
        functools.partial(_nsa_kernel, tq=tq, tk=tk, t=t),
        grid=(b, nq_t),
        in_specs=[pl.BlockSpec((tq, 384), lambda i, j: (i * nq_t + j, 0)),
                  pl.BlockSpec((nbc, 256), lambda i, j: (i, 0)),
                  pl.BlockSpec((t, 512), lambda i, j: (i, 0)),
                  pl.BlockSpec((t, 256), lambda i, j: (i, 0)),
                  pl.BlockSpec((tq, LANES), lambda i, j: (i * nq_t + j, 0))],
        out_specs=pl.BlockSpec((tq, 384), lambda i, j: (i * nq_t + j, 0)),
        out_shape=jax.ShapeDtypeStruct((b * t, 384), bf16),
        compiler_params=_cp(("parallel", "parallel")),
        name="nsa_attn",
    )(nq, kcm, nkvb, winb, misc)


def _pool_project(d, wp_ref, ps_ref):
    return (jnp.dot(d.astype(bf16), wp_ref[...], preferred_element_type=f32) * ps_ref[...]).astype(bf16)


def _out_tail(x, ofox, onsa, opool, wo_ref, g_ref, gate):
    o = (jnp.dot(ofox, wo_ref[0:384, :], preferred_element_type=f32)
         + jnp.dot(onsa, wo_ref[384:768, :], preferred_element_type=f32)
         + jnp.dot(opool, wo_ref[768:1024, :], preferred_element_type=f32))
    return x + gate * _rms(o, g_ref[...])


def _outproj_kernel(x_ref, ofox_ref, onsa_ref, u_ref, halo_ref, wo_ref, wp_ref, ps_ref, g_ref, gate_ref,
                    o_ref, ext_ref, *, tm, t):
    i = pl.program_id(0)
    pos0 = (i * tm) % t
    u = u_ref[...]
    ext_ref[0:16, :] = jnp.where(pos0 > 0, halo_ref[...], 0.0)
    ext_ref[16:, :] = u
    pos = lax.broadcasted_iota(i32, (tm, LANES), 0) + pos0
    lo = lax.broadcasted_iota(i32, (tm, LANES), 1) < HALF

    def shifted(k, c):
        return ext_ref[16 - k:16 - k + tm, c * LANES:(c + 1) * LANES]

    def cnt(w):
        return jnp.minimum(w, pos + 1).astype(f32)

    ds = []
    for c, (wa, wb) in enumerate(((POOL_WINDOWS[0], POOL_WINDOWS[1]), (POOL_WINDOWS[2], POOL_WINDOWS[3]))):
        run = shifted(0, c)
        sums = {}
        for k in range(1, wb):
            run = run + shifted(k, c)
            if k + 1 in (wa, wb):
                sums[k + 1] = run
        ds.append(jnp.where(lo, sums[wa] / cnt(wa), sums[wb] / cnt(wb)) - u[:, c * LANES:(c + 1) * LANES])
    opool = _pool_project(jnp.concatenate(ds, axis=1), wp_ref, ps_ref)
    o_ref[...] = _out_tail(x_ref[...], ofox_ref[...], onsa_ref[...], opool, wo_ref, g_ref, gate_ref[0])


def _outproj(x2, ofox, onsa, u, wo, wp, ps, g1, gate, *, tm, t):
    m = x2.shape[0]
    row = lambda w_: pl.BlockSpec((tm, w_), lambda i: (i, 0))
    const = lambda s: pl.BlockSpec(s, lambda i: (0, 0))
    return pl.pallas_call(
        functools.partial(_outproj_kernel, tm=tm, t=t),
        grid=(m // tm,),
        in_specs=[row(D_MODEL), row(384), row(384), row(256),
                  pl.BlockSpec((16, 256), lambda i: (jnp.maximum(i * (tm // 16) - 1, 0), 0)),
                  const((D_MODEL, D_MODEL)), const((256, 256)), const((1, 256)), const((1, D_MODEL)),
                  pl.BlockSpec((1, 1, D_MODEL), lambda i: ((i * tm) // t, 0, 0))],
        out_specs=row(D_MODEL),
        out_shape=jax.ShapeDtypeStruct((m, D_MODEL), f32),
        scratch_shapes=[pltpu.VMEM((tm + 16, 256), f32)],
        compiler_params=_cp(("parallel",)),
        name="out_proj",
    )(x2, ofox, onsa, u, u, wo, wp, ps, g1, gate)


def _outproj_dec_kernel(x_ref, ofox_ref, onsa_ref, ext_ref, wo_ref, wp_ref, ps_ref, g_ref, gate_ref, o_ref, *, past):
    ext = ext_ref[...]
    n = ext.shape[1]
    r = lax.broadcasted_iota(i32, ext.shape, 1)
    lane = lax.broadcasted_iota(i32, (ext.shape[0], C_POOL), 1)
    u_new = jnp.sum(jnp.where(r == n - 1, ext, 0.0), axis=1)
    d = jnp.zeros(u_new.shape, f32)
    for g, w in enumerate(POOL_WINDOWS):
        sw = jnp.sum(jnp.where(r >= n - w, ext, 0.0), axis=1)
        d = jnp.where(lane // POOL_GW == g, sw / float(min(w, past + 1)), d)
    opool = _pool_project(d - u_new, wp_ref, ps_ref)
    o_ref[...] = _out_tail(x_ref[...], ofox_ref[...], onsa_ref[...], opool, wo_ref, g_ref, gate_ref[...])


def _outproj_dec(x2, ofox, onsa, ext, wo, wp, ps, g1, gate, past):
    m = x2.shape[0]
    full = lambda a: pl.BlockSpec(a.shape, lambda i: (0,) * a.ndim)
    args = (x2, ofox, onsa, ext, wo, wp, ps, g1, gate)
    return pl.pallas_call(
        functools.partial(_outproj_dec_kernel, past=past),
        grid=(1,),
        in_specs=[full(a) for a in args],
        out_specs=pl.BlockSpec((m, D_MODEL), lambda i: (0, 0)),
        out_shape=jax.ShapeDtypeStruct((m, D_MODEL), f32),
        compiler_params=_cp(("arbitrary",)),
        name="out_proj_dec",
    )(*args)


def _mlp_kernel(x_ref, g2_ref, sc_ref, sh_ref, w1_ref, w2_ref, g3_ref, gate_ref, o_ref, *, tf):
    x = x_ref[...]
    h = (_rms(x, g2_ref[...]) * (1.0 + sc_ref[0]) + sh_ref[0]).astype(bf16)
    acc = jnp.zeros(x.shape, f32)
    for c in range(D_FF // tf):
        a = jnp.maximum(jnp.dot(h, w1_ref[:, c * tf:(c + 1) * tf], preferred_element_type=f32), 0.0)
        acc = acc + jnp.dot((a * a).astype(bf16), w2_ref[c * tf:(c + 1) * tf, :], preferred_element_type=f32)
    o_ref[...] = x + gate_ref[0] * _rms(acc, g3_ref[...])


def _mlp(x2, g2, sc, sh, w1, w2, g3, gate, *, tm, rows_per_mod):
    m = x2.shape[0]
    mod_rows = sc.shape[1]
    modmap = lambda i: ((i * tm) // rows_per_mod, 0, 0)
    const = lambda s: pl.BlockSpec(s, lambda i: (0, 0))
    mod = pl.BlockSpec((1, mod_rows, D_MODEL), modmap)
    return pl.pallas_call(
        functools.partial(_mlp_kernel, tf=512),
        grid=(m // tm,),
        in_specs=[pl.BlockSpec((tm, D_MODEL), lambda i: (i, 0)), const((1, D_MODEL)), mod, mod,
                  const((D_MODEL, D_FF)), const((D_FF, D_MODEL)), const((1, D_MODEL)), mod],
        out_specs=pl.BlockSpec((tm, D_MODEL), lambda i: (i, 0)),
        out_shape=jax.ShapeDtypeStruct((m, D_MODEL), f32),
        compiler_params=_cp(("parallel",)),
        name="mlp",
    )(x2, g2, sc, sh, w1, w2, g3, gate)


def _pad8(a):
    return jnp.concatenate([a, jnp.zeros((8 - a.shape[0], a.shape[1]), a.dtype)], axis=0)


def _by_group(rows):
    grp = lax.broadcasted_iota(i32, (8, rows.shape[1]), 0) // G_NSA
    return jnp.where(grp == 0, jnp.broadcast_to(rows[0:1], (8, rows.shape[1])),
                     jnp.broadcast_to(rows[1:2], (8, rows.shape[1])))


def _lane_to_sublane(row, offset, stride):
    sub = lax.broadcasted_iota(i32, (8, LANES), 0)
    lane = lax.broadcasted_iota(i32, (8, LANES), 1)
    return jnp.sum(jnp.where(lane == offset + stride * sub, jnp.broadcast_to(row, (8, LANES)), 0.0),
                   axis=1, keepdims=True)


def _fox_dec_kernel(pt_ref, fq_ref, misc_ref, bias_ref, knew_ref, *rest, pps):
    lf_refs = rest[:pps]
    kv_refs = rest[pps:2 * pps]
    o_ref, lfo_ref, m_scr, l_scr, acc_scr, c_scr = rest[2 * pps:]
    g = pl.program_id(1)
    n = pps * PAGE_SIZE
    q6f = _pad8(fq_ref[0])
    q6 = q6f.astype(bf16)
    lane = lax.broadcasted_iota(i32, (8, LANES), 1)

    @pl.when(g == 0)
    def _():
        lfrow = jnp.where(lane[0:1] < H_FOX, _log_sigmoid(misc_ref[0] + bias_ref[...]), 0.0)
        lfo_ref[0] = lfrow
        m_scr[...] = jnp.sum(q6f * _pad8(knew_ref[0, 0]), axis=1, keepdims=True)
        l_scr[...] = jnp.ones((8, 1), f32)
        acc_scr[...] = _pad8(knew_ref[0, 1])
        c_scr[...] = _lane_to_sublane(lfrow, 0, 1)

    carry = c_scr[...]
    biases = []
    n_pages = pl.num_programs(1) * pps
    for i in range(pps):
        row = pt_ref[pl.program_id(0), n_pages - 1 - (g * pps + i)] % 8
        lf = _pad8(jnp.concatenate([lf_refs[i][0, h, 0, pl.ds(row, 1), :] for h in range(H_FOX)], axis=0))
        suf = lf
        for sft in (1, 2, 4, 8, 16, 32, 64):
            suf = suf + jnp.where(lane + sft < LANES, pltpu.roll(suf, LANES - sft, 1), 0.0)
        biases.append(carry + (suf - lf))
        carry = carry + suf[:, 0:1]
    c_scr[...] = carry
    s = jnp.concatenate(biases, axis=1)
    sub = lax.broadcasted_iota(i32, (8, n), 0)
    for h in range(H_FOX):
        kht = jnp.concatenate([kv_refs[i][0, 0, 0, h] for i in range(pps)], axis=1).astype(bf16)
        s = jnp.where(sub == h, s + jnp.dot(q6, kht, preferred_element_type=f32), s)
    m = m_scr[...]
    m_new = jnp.maximum(m, jnp.max(s, axis=1, keepdims=True))
    alpha = jnp.exp(m - m_new)
    p = jnp.exp(s - m_new)
    l_scr[...] = alpha * l_scr[...] + jnp.sum(p, axis=1, keepdims=True)
    m_scr[...] = m_new
    pb = p.astype(bf16)
    sub_d = lax.broadcasted_iota(i32, (8, HEAD_DIM), 0)
    pv = jnp.zeros((8, HEAD_DIM), f32)
    for h in range(H_FOX):
        vht = jnp.concatenate([kv_refs[i][0, 0, 1, h] for i in range(pps)], axis=1).astype(bf16)
        pv = jnp.where(sub_d == h, _nt(pb, vht), pv)
    acc_scr[...] = alpha * acc_scr[...] + pv

    @pl.when(g == pl.num_programs(1) - 1)
    def _():
        o_ref[0] = acc_scr[...] / l_scr[...]


def _fox_decode(layer, page_table, fq, misc, bias_row, fkv_new, lf_t, cache_kv):
    b, n_pages = page_table.shape
    pps = FOX_PPS
    ng = n_pages // pps
    row3 = lambda w_: pl.BlockSpec((1, 1, w_), lambda i, g, pt: (i, 0, 0))

    def page_id(bi, g, pt, i):
        return pt[jnp.minimum(bi, b - 1), n_pages - 1 - (jnp.minimum(g, ng - 1) * pps + i)]

    def page(i, shape):
        zeros = (0,) * len(shape)
        return pl.BlockSpec((1, 1) + shape, lambda bi, g, pt, i=i: (layer, page_id(bi, g, pt, i)) + zeros)

    def lf_group(i):
        return pl.BlockSpec((1, H_FOX, 1, 8, PAGE_SIZE),
                            lambda bi, g, pt, i=i: (layer, 0, page_id(bi, g, pt, i) // 8, 0, 0))

    grid_spec = pltpu.PrefetchScalarGridSpec(
        num_scalar_prefetch=1,
        grid=(b, ng),
        in_specs=[pl.BlockSpec((1, H_FOX, HEAD_DIM), lambda i, g, pt: (i, 0, 0)), row3(LANES),
                  pl.BlockSpec((1, LANES), lambda i, g, pt: (0, 0)),
                  pl.BlockSpec((1, 2, H_FOX, HEAD_DIM), lambda i, g, pt: (i, 0, 0, 0))]
                 + [lf_group(i) for i in range(pps)]
                 + [page(i, (2, H_FOX, HEAD_DIM, PAGE_SIZE)) for i in range(pps)],
        out_specs=[pl.BlockSpec((1, 8, HEAD_DIM), lambda i, g, pt: (i, 0, 0)), row3(LANES)],
        scratch_shapes=[pltpu.VMEM((8, 1), f32), pltpu.VMEM((8, 1), f32), pltpu.VMEM((8, HEAD_DIM), f32),
                        pltpu.VMEM((8, 1), f32)],
    )
    return pl.pallas_call(
        functools.partial(_fox_dec_kernel, pps=pps),
        grid_spec=grid_spec,
        out_shape=[jax.ShapeDtypeStruct((b, 8, HEAD_DIM), f32), jax.ShapeDtypeStruct((b, 1, LANES), f32)],
        compiler_params=_cp(("parallel", "arbitrary")),
        name="fox_decode",
    )(page_table, fq, misc, bias_row, fkv_new, *([lf_t] * pps), *([cache_kv] * pps))


def _nsa_cmp_dec_kernel(pt_ref, q_ref, *rest, pps, past):
    pg_refs = rest[:pps]
    p_ref, o_ref, mean_scr = rest[pps:]
    g = pl.program_id(1)
    per_page = PAGE_SIZE // CMP_BLOCK
    rows = 2 * KV_NSA * HEAD_DIM
    x = jnp.concatenate([pg_refs[i][0, 0].reshape(rows, PAGE_SIZE) for i in range(pps)], axis=1)
    r = lax.broadcasted_iota(i32, (pps * PAGE_SIZE, LANES), 0)
    c = lax.broadcasted_iota(i32, (pps * PAGE_SIZE, LANES), 1)
    avg = jnp.where(r // CMP_BLOCK == c, 1.0 / CMP_BLOCK, 0.0).astype(bf16)
    hi = x.astype(bf16)
    lo = (x - hi.astype(f32)).astype(bf16)
    mean_scr[g] = jnp.dot(hi, avg, preferred_element_type=f32) + jnp.dot(lo, avg, preferred_element_type=f32)

    @pl.when(g == pl.num_programs(1) - 1)
    def _():
        ng = mean_scr.shape[0]
        nbc = ng * pps * per_page
        mean = mean_scr[0]
        for gg in range(1, ng):
            mean = mean + pltpu.roll(mean_scr[gg], gg * pps * per_page, 1)
        mean = mean[:, :nbc].astype(bf16)
        part = lambda feat, kv: mean[(feat * KV_NSA + kv) * HEAD_DIM:(feat * KV_NSA + kv + 1) * HEAD_DIM]
        q6 = _pad8(q_ref[0]).astype(bf16)
        grp0 = lax.broadcasted_iota(i32, (8, 1), 0) // G_NSA == 0
        s = jnp.where(grp0, jnp.dot(q6, part(0, 0), preferred_element_type=f32),
                      jnp.dot(q6, part(0, 1), preferred_element_type=f32))
        j = lax.broadcasted_iota(i32, (8, nbc), 1)
        mask = ((j + 1) * CMP_BLOCK - 1) <= past
        s = jnp.where(mask, s, NEG)
        pe = jnp.where(mask, jnp.exp(s - jnp.max(s, axis=1, keepdims=True)), 0.0)
        den = jnp.sum(pe, axis=1, keepdims=True)
        p = pe / jnp.where(den > 0.0, den, 1.0)
        p_ref[0] = p
        pb = p.astype(bf16)
        o_ref[0] = jnp.where(grp0, _nt(pb, part(1, 0)), _nt(pb, part(1, 1)))


def _nsa_cmp_decode(layer, page_table, nq, cache_nsa, past):
    b, n_pages = page_table.shape
    pps = CMP_PPS
    nbc = past // CMP_BLOCK
    grid_spec = pltpu.PrefetchScalarGridSpec(
        num_scalar_prefetch=1,
        grid=(b, n_pages // pps),
        in_specs=[pl.BlockSpec((1, H_NSA, HEAD_DIM), lambda i, g, pt: (i, 0, 0))]
                 + [pl.BlockSpec((1, 1, 2, KV_NSA, HEAD_DIM, PAGE_SIZE),
                                 lambda bi, g, pt, i=i: (layer, pt[jnp.minimum(bi, b - 1),
                                                                   jnp.minimum(g, n_pages // pps - 1) * pps + i],
                                                         0, 0, 0, 0))
                    for i in range(pps)],
        out_specs=[pl.BlockSpec((1, 8, nbc), lambda i, g, pt: (i, 0, 0)),
                   pl.BlockSpec((1, 8, HEAD_DIM), lambda i, g, pt: (i, 0, 0))],
        scratch_shapes=[pltpu.VMEM((n_pages // pps, 2 * KV_NSA * HEAD_DIM, LANES), f32)],
    )
    return pl.pallas_call(
        functools.partial(_nsa_cmp_dec_kernel, pps=pps, past=past),
        grid_spec=grid_spec,
        out_shape=[jax.ShapeDtypeStruct((b, 8, nbc), f32), jax.ShapeDtypeStruct((b, 8, HEAD_DIM), f32)],
        compiler_params=_cp(("parallel", "arbitrary")),
        name="nsa_cmp_decode",
    )(page_table, nq, *([cache_nsa] * pps))


def _nsa_topk_dec_kernel(p_ref, idx_ref, *, past, width):
    b = p_ref.shape[0]
    nbc = p_ref.shape[2]
    nbs = -(-(past + 1) // SEL_BLOCK)
    imps = []
    for kv in range(KV_NSA):
        imps.append(p_ref[:, kv * G_NSA, :] + p_ref[:, kv * G_NSA + 1, :] + p_ref[:, kv * G_NSA + 2, :])
    imp = jnp.concatenate(imps, axis=0)
    imp = jnp.concatenate([imp, jnp.zeros((2 * b, width - nbc), f32)], axis=1)
    j = lax.broadcasted_iota(i32, (2 * b, width), 1)
    cur = past // SEL_BLOCK
    forced = (j == 0) | (j == cur) | (j == cur - 1)
    score = jnp.where(forced, FORCE, imp)
    score = jnp.where(j * SEL_BLOCK <= past, score, -1.0)
    score = jnp.where(j < nbs, score, -2.0)
    _, firsts = _select_topk(score, min(TOPK_BLOCKS, nbs))
    lane = lax.broadcasted_iota(i32, (2 * b, LANES), 1)
    out = jnp.zeros((2 * b, LANES), i32)
    for k, first in enumerate(firsts):
        out = jnp.where(lane == k, first.astype(i32), out)
    idx_ref[...] = out


def _nsa_topk_decode(p8, past):
    b, _, nbc = p8.shape
    width = 2 * nbc
    return pl.pallas_call(
        functools.partial(_nsa_topk_dec_kernel, past=past, width=width),
        grid=(1,),
        in_specs=[pl.BlockSpec(p8.shape, lambda i: (0, 0, 0))],
        out_specs=pl.BlockSpec((2 * b, LANES), lambda i: (0, 0)),
        out_shape=jax.ShapeDtypeStruct((2 * b, LANES), i32),
        compiler_params=_cp(("arbitrary",)),
        name="nsa_topk_decode",
    )(p8)


def _nsa_sel_dec_kernel(pt_ref, idx_ref, q_ref, misc_ref, ocmp_ref, nkv_ref, wnew_ref, win_ref, *rest, past, nsel):
    blk_refs = rest[:KV_NSA * nsel]
    o_ref = rest[KV_NSA * nsel]
    b = pl.program_id(0)
    n_past_blk = past // SEL_BLOCK
    q8f = _pad8(q_ref[0])
    q8 = q8f.astype(bf16)
    sub = lax.broadcasted_iota(i32, (8, 1), 0)
    grp0 = sub // G_NSA == 0
    per_page = PAGE_SIZE // SEL_BLOCK
    lane_blk = lax.broadcasted_iota(i32, (8, PAGE_SIZE), 1) // SEL_BLOCK
    m = jnp.full((8, 1), NEG, f32)
    l = jnp.zeros((8, 1), f32)
    acc = jnp.zeros((8, HEAD_DIM), f32)
    s_new = jnp.sum(q8f * _by_group(nkv_ref[0, 2]), axis=1, keepdims=True)
    v_new = _by_group(nkv_ref[0, 3])
    for kv in range(KV_NSA):
        in_group = (sub // G_NSA) == kv
        has_new = jnp.zeros((), jnp.bool_)
        for k in range(nsel):
            j = idx_ref[kv * pl.num_programs(0) + b, k]
            has_new = has_new | (j == n_past_blk)
            blk = blk_refs[kv * nsel + k]
            mask = in_group & (j < n_past_blk) & (lane_blk == j % per_page)
            s = jnp.where(mask, jnp.dot(q8, blk[0, 0, 0, 0].astype(bf16), preferred_element_type=f32), NEG)
            m_new = jnp.maximum(m, jnp.max(s, axis=1, keepdims=True))
            alpha = jnp.exp(m - m_new)
            pe = jnp.where(mask, jnp.exp(s - m_new), 0.0)
            l = alpha * l + jnp.sum(pe, axis=1, keepdims=True)
            acc = alpha * acc + _nt(pe.astype(bf16), blk[0, 0, 1, 0].astype(bf16))
            m = m_new
        new_ok = in_group & has_new
        m_new = jnp.maximum(m, jnp.where(new_ok, s_new, NEG))
        alpha = jnp.exp(m - m_new)
        p_new = jnp.where(new_ok, jnp.exp(s_new - m_new), 0.0)
        l = alpha * l + p_new
        acc = alpha * acc + p_new * v_new
        m = m_new
    o_sel = acc / jnp.where(l > 0.0, l, 1.0)
    wb = win_ref.shape[5]
    kpos = past - wb + lax.broadcasted_iota(i32, (8, wb), 1)
    rel = past - kpos
    wmask = (rel >= 0) & (rel < WINDOW) & (kpos >= 0)
    s = jnp.where(grp0, jnp.dot(q8, win_ref[0, 0, 0, 0].astype(bf16), preferred_element_type=f32),
                  jnp.dot(q8, win_ref[0, 0, 0, 1].astype(bf16), preferred_element_type=f32))
    s = jnp.where(wmask, s, NEG)
    s_w = jnp.sum(q8f * _by_group(wnew_ref[0, 0]), axis=1, keepdims=True)
    m = jnp.maximum(jnp.max(s, axis=1, keepdims=True), s_w)
    p = jnp.where(wmask, jnp.exp(s - m), 0.0)
    p_w = jnp.exp(s_w - m)
    pb = p.astype(bf16)
    pv = jnp.where(grp0, _nt(pb, win_ref[0, 0, 1, 0].astype(bf16)), _nt(pb, win_ref[0, 0, 1, 1].astype(bf16)))
    o_win = (pv + p_w * _by_group(wnew_ref[0, 1])) / (jnp.sum(p, axis=1, keepdims=True) + p_w)
    misc = misc_ref[0]
    gates = [jax.nn.sigmoid(_lane_to_sublane(misc, GATE0 + r, 3)) for r in range(3)]
    o_ref[0] = gates[0] * ocmp_ref[0] + gates[1] * o_sel + gates[2] * o_win


def _nsa_sel_decode(layer, page_table, idx, nq, misc, ocmp, nkv_new, win_new, cache_win, cache_nsa, past):
    b = page_table.shape[0]
    nsel = idx.shape[1]
    n_past_blk = past // SEL_BLOCK
    per_page = PAGE_SIZE // SEL_BLOCK
    wb = cache_win.shape[5]

    def blk_spec(kv, k):
        def imap(i, pt, ix):
            ii = jnp.minimum(i, b - 1)
            jp = jnp.clip(ix[kv * b + ii, k], 0, n_past_blk - 1)
            return (layer, pt[ii, jp // per_page], 1, kv, 0, 0)
        return pl.BlockSpec((1, 1, 2, 1, HEAD_DIM, PAGE_SIZE), imap)

    grid_spec = pltpu.PrefetchScalarGridSpec(
        num_scalar_prefetch=2,
        grid=(b,),
        in_specs=[pl.BlockSpec((1, H_NSA, HEAD_DIM), lambda i, pt, ix: (i, 0, 0)),
                  pl.BlockSpec((1, 1, LANES), lambda i, pt, ix: (i, 0, 0)),
                  pl.BlockSpec((1, 8, HEAD_DIM), lambda i, pt, ix: (i, 0, 0)),
                  pl.BlockSpec((1, 4, KV_NSA, HEAD_DIM), lambda i, pt, ix: (i, 0, 0, 0)),
                  pl.BlockSpec((1, 2, KV_NSA, HEAD_DIM), lambda i, pt, ix: (i, 0, 0, 0)),
                  pl.BlockSpec((1, 1, 2, KV_NSA, HEAD_DIM, wb), lambda i, pt, ix: (layer, i, 0, 0, 0, 0))]
                 + [blk_spec(kv, k) for kv in range(KV_NSA) for k in range(nsel)],
        out_specs=pl.BlockSpec((1, 8, HEAD_DIM), lambda i, pt, ix: (i, 0, 0)),
    )
    return pl.pallas_call(
        functools.partial(_nsa_sel_dec_kernel, past=past, nsel=nsel),
        grid_spec=grid_spec,
        out_shape=jax.ShapeDtypeStruct((b, 8, HEAD_DIM), f32),
        compiler_params=_cp(("parallel",)),
        name="nsa_sel_decode",
    )(page_table, idx, nq, misc, ocmp, nkv_new, win_new, cache_win, *([cache_nsa] * (KV_NSA * nsel)))


def _proj_columns():
    off = np.cumsum([0, 384, 384, 384, 6, 384, 128, 128, 128, 128, 128, 128, 18, 256])
    fq, fk, fv, ff, nq, nkc, nvc, nks, nvs, nkw, nvw, ng, u = [int(o) for o in off[:13]]
    cols = list(range(fq, fq + 384))
    for h in NSA_PAIR_ORDER:
        cols += list(range(nq + h * HEAD_DIM, nq + (h + 1) * HEAD_DIM))
    cols += list(range(fk, fk + 768))
    cols += list(range(nkc, nkc + 512))
    cols += list(range(nkw, nkw + 256))
    cols += list(range(u, u + 256))
    cols += list(range(ff, ff + 6)) + list(range(ng, ng + 18)) + [N_IN] * (LANES - 24)
    return np.asarray(cols, np.int32)


def _out_rows():
    rows = list(range(0, 384))
    for h in NSA_PAIR_ORDER:
        rows += list(range(384 + h * HEAD_DIM, 384 + (h + 1) * HEAD_DIM))
    rows += list(range(768, 1024))
    return np.asarray(rows, np.int32)


def _rope_tables(pos):
    half = ROT_DIM // 2
    inv = ROPE_THETA ** (-jnp.arange(0, ROT_DIM, 2, dtype=f32) / ROT_DIM)
    ang = pos.astype(f32)[:, None] * inv[None, :]
    cos, sin = jnp.cos(ang), jnp.sin(ang)
    n = pos.shape[0]
    one = jnp.ones((n, HEAD_DIM - ROT_DIM), f32)
    zero8 = jnp.zeros((n, half), f32)
    zrest = jnp.zeros((n, HEAD_DIM - ROT_DIM), f32)
    c = jnp.concatenate([cos, cos, one], axis=1)
    sa = jnp.concatenate([-sin, zero8, zrest], axis=1)
    sb = jnp.concatenate([zero8, sin, zrest], axis=1)
    return tuple(jnp.concatenate([a, a], axis=1) for a in (c, sa, sb))


def kernel(x_prompt, x_sample, cache_fox_kv, cache_fox_logf, cache_nsa_kv, cache_nsa_win, state_pool, page_table,
           c_prompt, c_sample, w_ada, b_ada, norm_g, w_in, b_fox_f, w_out, w_pool, pool_scale, w_ff1, w_ff2):
    depth = w_in.shape[0]
    bp, t, _ = x_prompt.shape
    bs = x_sample.shape[0]
    past = page_table.shape[1] * PAGE_SIZE
    mp = bp * t

    cols = _proj_columns()
    w_in_p = jnp.concatenate([w_in, jnp.zeros((depth, D_MODEL, 1), f32)], axis=2)[:, :, cols].astype(bf16)
    w_out_p = w_out[:, _out_rows(), :].astype(bf16)
    w_pool_bd = jnp.zeros((depth, C_POOL, C_POOL), f32)
    for g in range(len(POOL_WINDOWS)):
        w_pool_bd = w_pool_bd.at[:, g * POOL_GW:(g + 1) * POOL_GW, g * POOL_GW:(g + 1) * POOL_GW].set(w_pool[:, g])
    w_pool_bd = w_pool_bd.astype(bf16)
    w1 = w_ff1.astype(bf16)
    w2 = w_ff2.astype(bf16)
    bias_rows = jnp.pad(b_fox_f, ((0, 0), (0, LANES - H_FOX))).reshape(depth, 1, LANES)

    rope_p = _rope_tables(jnp.arange(t))
    rope_s = _rope_tables(jnp.full((bs,), past, i32))

    to_last = (0, 1, 3, 4, 5, 2)
    fox_t = jnp.transpose(cache_fox_kv, to_last)
    nsa_t = jnp.transpose(cache_nsa_kv, to_last)
    win_t = jnp.transpose(cache_nsa_win, to_last)
    n_pool = cache_fox_logf.shape[1]
    assert n_pool % 8 == 0
    lf_t = jnp.transpose(cache_fox_logf, (0, 3, 1, 2)).reshape(depth, H_FOX, n_pool // 8, 8, PAGE_SIZE)
    nat_from_pair = np.argsort(np.asarray(NSA_PAIR_ORDER))

    mod = _ada(jnp.concatenate([c_prompt, c_sample], axis=0), w_ada, b_ada)
    mod = mod.reshape(depth, bp + bs, 6, D_MODEL)

    yp = x_prompt.reshape(mp, D_MODEL)
    ys = x_sample.reshape(bs, D_MODEL)
    sp, ss = [], []
    for l in range(depth):
        g = norm_g[l].reshape(4, 1, D_MODEL)
        modp = [mod[l, :bp, k].reshape(bp, 1, D_MODEL) for k in range(6)]
        mods = [mod[l, bp:, k].reshape(1, bs, D_MODEL) for k in range(6)]

        (fq, nq, fkv, nkv, win, u, misc, fkvb, nkvb, winb, kcm) = _inproj(
            yp, g[0], modp[1], modp[0], w_in_p[l], *rope_p, tm=512, rows_per_mod=t, rope_rows=t, with_means=True)
        logf, ccol, crow = _foxprep(misc, bias_rows[l], bp, t)
        o_fox = _fox_attn(fq, fkvb, ccol, crow, bp, t)
        o_nsa = _nsa_attn(nq, kcm, nkvb, winb, misc, bp, t)
        y1 = _outproj(yp, o_fox, o_nsa, u, w_out_p[l], w_pool_bd[l], pool_scale[l].reshape(1, C_POOL), g[1],
                      modp[2], tm=512, t=t)
        yp = _mlp(y1, g[2], modp[4], modp[3], w1[l], w2[l], g[3], modp[5], tm=512, rows_per_mod=t)
        wl = min(WINDOW, t)
        sp.append((fkv.reshape(bp, t, 2, H_FOX, HEAD_DIM), logf.reshape(bp, t, H_FOX),
                   nkv.reshape(bp, t, 4, KV_NSA, HEAD_DIM),
                   win.reshape(bp, t, 2, KV_NSA, HEAD_DIM)[:, t - wl:],
                   u.reshape(bp, t, C_POOL)[:, t - POOL_STATE:]))

        (fq_s, nq_s, fkv_s, nkv_s, win_s, u_s, misc_s, _, _, _, _) = _inproj(
            ys, g[0], mods[1], mods[0], w_in_p[l], *rope_s, tm=bs, rows_per_mod=bs, rope_rows=bs, with_means=False)
        r3 = lambda a: a.reshape(bs, 1, a.shape[-1])
        fq6 = fq_s.astype(f32).reshape(bs, H_FOX, HEAD_DIM)
        nq6 = nq_s.astype(f32).reshape(bs, H_NSA, HEAD_DIM)[:, nat_from_pair]
        nkv4 = nkv_s.reshape(bs, 4, KV_NSA, HEAD_DIM)
        win2 = win_s.reshape(bs, 2, KV_NSA, HEAD_DIM)
        o_fox_s, logf_s = _fox_decode(l, page_table, fq6, r3(misc_s), bias_rows[l],
                                      fkv_s.reshape(bs, 2, H_FOX, HEAD_DIM), lf_t, fox_t)
        p8, o_cmp = _nsa_cmp_decode(l, page_table, nq6, nsa_t, past)
        idx = _nsa_topk_decode(p8, past)[:, :min(TOPK_BLOCKS, -(-(past + 1) // SEL_BLOCK))]
        o_nsa_s = _nsa_sel_decode(l, page_table, idx, nq6, r3(misc_s), o_cmp, nkv4, win2,
                                  win_t, nsa_t, past)
        o_fox_s = o_fox_s[:, :H_FOX].reshape(bs, 384).astype(bf16)
        o_nsa_s = o_nsa_s[:, np.asarray(NSA_PAIR_ORDER)].reshape(bs, 384).astype(bf16)
        u_ext = jnp.concatenate([state_pool[l], u_s.reshape(bs, 1, C_POOL)], axis=1)
        y1s = _outproj_dec(ys, o_fox_s, o_nsa_s, u_ext, w_out_p[l],
                           w_pool_bd[l], pool_scale[l].reshape(1, C_POOL), g[1], mods[2][0], past)
        ys = _mlp(y1s, g[2], mods[4], mods[3], w1[l], w2[l], g[3], mods[5], tm=bs, rows_per_mod=bs)
        win_all = jnp.concatenate([cache_nsa_win[l], win2[:, None]], axis=1)
        ss.append((fkv_s.reshape(bs, 1, 2, H_FOX, HEAD_DIM), logf_s[:, :, :H_FOX], nkv4[:, None],
                   win_all[:, 1:], u_ext[:, 1:]))

    stk = lambda lst, i: jnp.stack([s[i] for s in lst], axis=0)
    return (yp.reshape(bp, t, D_MODEL), ys.reshape(bs, 1, D_MODEL),
            stk(sp, 0), stk(sp, 1), stk(sp, 2), stk(sp, 3), stk(sp, 4),
            stk(ss, 0), stk(ss, 1), stk(ss, 2), stk(ss, 3), stk(ss, 4))
```

```python
import functools

import numpy as np
import jax
import jax.numpy as jnp
from jax import lax
from jax.experimental import pallas as pl
from jax.experimental.pallas import tpu as pltpu

f32 = jnp.float32
bf16 = jnp.bfloat16
i32 = jnp.int32

D_MODEL = 1024
HEAD_DIM = 64
H_FOX = 6
H_NSA = 6
KV_NSA = 2
G_NSA = H_NSA // KV_NSA
POOL_WINDOWS = (2, 4, 8, 16)
C_POOL = 256
POOL_GW = 64
POOL_STATE = 15
ROT_DIM = 16
ROPE_THETA = 500000.0
CMP_BLOCK = 64
SEL_BLOCK = 64
TOPK_BLOCKS = 8
WINDOW = 512
PAGE_SIZE = 128
D_FF = 4 * D_MODEL
EPS = 1e-6
NEG = -1e30
FORCE = 1e4
SCALE = HEAD_DIM ** -0.5
N_IN = 2584

LANES = 128
HALF = LANES // 2
VMEM_LIMIT = 56 * 1024 * 1024

C_FQ = 0
C_NQ = 384
C_FKV = 768
C_NKV = 1536
C_WIN = 2048
C_U = 2304
C_MISC = 2560
N_PROJ = 2688
GATE0 = H_FOX
NSA_PAIR_ORDER = (0, 3, 1, 4, 2, 5)

FOX_TQ = 512
FOX_TK = 512
NSA_TQ = 256
NSA_TK = 512
FOX_PPS = 16
CMP_PPS = 16


def _cp(sem):
    return pltpu.CompilerParams(dimension_semantics=sem, vmem_limit_bytes=VMEM_LIMIT)


def _nt(a, b):
    return lax.dot_general(a, b, (((1,), (1,)), ((), ())), preferred_element_type=f32)


def _rms(x, g):
    return x * lax.rsqrt(jnp.mean(x * x, axis=-1, keepdims=True) + EPS) * g


def _log_sigmoid(x):
    return jnp.minimum(x, 0.0) - jnp.log1p(jnp.exp(-jnp.abs(x)))


def _ada_kernel(c_ref, w_ref, b_ref, o_ref):
    c = c_ref[...]
    a = (c * jax.nn.sigmoid(c)).astype(bf16)
    o_ref[0] = jnp.dot(a, w_ref[0].astype(bf16), preferred_element_type=f32) + b_ref[0]


def _ada(c_all, w_ada, b_ada):
    depth = w_ada.shape[0]
    n = c_all.shape[0]
    tn = 1024
    return pl.pallas_call(
        _ada_kernel,
        grid=(depth, 6 * D_MODEL // tn),
        in_specs=[pl.BlockSpec((n, D_MODEL), lambda l, j: (0, 0)),
                  pl.BlockSpec((1, D_MODEL, tn), lambda l, j: (l, 0, j)),
                  pl.BlockSpec((1, 1, tn), lambda l, j: (l, 0, j))],
        out_specs=pl.BlockSpec((1, n, tn), lambda l, j: (l, 0, j)),
        out_shape=jax.ShapeDtypeStruct((depth, n, 6 * D_MODEL), f32),
        compiler_params=_cp(("parallel", "parallel")),
        name="ada_mod",
    )(c_all, w_ada, b_ada.reshape(depth, 1, 6 * D_MODEL))


def _inproj_kernel(x_ref, g_ref, sc_ref, sh_ref, w_ref, cos_ref, sa_ref, sb_ref, *rest, tm, with_means):
    (fq_ref, nq_ref, fkv_ref, nkv_ref, win_ref, u_ref, misc_ref, fkvb_ref, nkvb_ref, winb_ref, kcm_ref) = rest[-11:]
    x = x_ref[...]
    h = (_rms(x, g_ref[...]) * (1.0 + sc_ref[0]) + sh_ref[0]).astype(bf16)
    cos = cos_ref[...]
    sa = sa_ref[...]
    sb = sb_ref[...]

    def rope(z):
        return z * cos + pltpu.roll(z, LANES - ROT_DIM // 2, 1) * sa + pltpu.roll(z, ROT_DIM // 2, 1) * sb

    def means(z):
        return jnp.sum(z.reshape(tm // CMP_BLOCK, CMP_BLOCK, LANES), axis=1) * (1.0 / CMP_BLOCK)

    def emit(blk, z):
        c = blk * LANES
        if c < C_NQ:
            fq_ref[:, c - C_FQ:c - C_FQ + LANES] = (z * SCALE).astype(bf16)
        elif c < C_FKV:
            nq_ref[:, c - C_NQ:c - C_NQ + LANES] = (rope(z) * SCALE).astype(bf16)
        elif c < C_NKV:
            fkv_ref[0, :, c - C_FKV:c - C_FKV + LANES] = z
            fkvb_ref[:, c - C_FKV:c - C_FKV + LANES] = z.astype(bf16)
        elif c < C_WIN:
            o = c - C_NKV
            if o in (0, 2 * LANES):
                z = rope(z)
            nkv_ref[0, :, o:o + LANES] = z
            nkvb_ref[:, o:o + LANES] = z.astype(bf16)
            if with_means and o < 2 * LANES:
                kcm_ref[:, o:o + LANES] = means(z)
        elif c < C_U:
            o = c - C_WIN
            if o == 0:
                z = rope(z)
            win_ref[:, o:o + LANES] = z
            winb_ref[:, o:o + LANES] = z.astype(bf16)
        elif c < C_MISC:
            u_ref[:, c - C_U:c - C_U + LANES] = z
        else:
            misc_ref[...] = z

    nblk = N_PROJ // LANES
    for b0 in range(0, nblk, 2):
        nb = min(2, nblk - b0)
        z2 = jnp.dot(h, w_ref[:, b0 * LANES:(b0 + nb) * LANES], preferred_element_type=f32)
        for k in range(nb):
            emit(b0 + k, z2[:, k * LANES:(k + 1) * LANES])
    if not with_means:
        kcm_ref[...] = jnp.zeros(kcm_ref.shape, f32)


def _inproj(x2, g0, sc, sh, w, cos, sa, sb, *, tm, rows_per_mod, rope_rows, with_means, layer=0, depth=1,
            stacked=None):
    m = x2.shape[0]
    nt = m // tm
    mod_rows = sc.shape[1]
    rope_blocks = rope_rows // tm
    kc_rows = max(tm // CMP_BLOCK, 8)
    modmap = lambda i: ((i * tm) // rows_per_mod, 0, 0)
    ropemap = lambda i: (i % rope_blocks, 0)
    row = lambda w_: pl.BlockSpec((tm, w_), lambda i: (i, 0))
    slab = lambda w_: pl.BlockSpec((1, tm, w_), lambda i: (layer, i, 0))
    outs = [((m, 384), bf16), ((m, 384), bf16), ((depth, m, 768), f32), ((depth, m, 512), f32), ((m, 256), f32),
            ((m, 256), f32), ((m, LANES), f32), ((m, 768), bf16), ((m, 512), bf16), ((m, 256), bf16)]
    out_shape = [jax.ShapeDtypeStruct(s, d) for s, d in outs] + [jax.ShapeDtypeStruct((nt * kc_rows, 256), f32)]
    out_specs = ([slab(s[-1]) if len(s) == 3 else row(s[-1]) for s, _ in outs]
                 + [pl.BlockSpec((kc_rows, 256), lambda i: (i, 0))])
    in_specs = [row(D_MODEL),
                pl.BlockSpec((1, D_MODEL), lambda i: (0, 0)),
                pl.BlockSpec((1, mod_rows, D_MODEL), modmap),
                pl.BlockSpec((1, mod_rows, D_MODEL), modmap),
                pl.BlockSpec((D_MODEL, N_PROJ), lambda i: (0, 0)),
                pl.BlockSpec((tm, LANES), ropemap),
                pl.BlockSpec((tm, LANES), ropemap),
                pl.BlockSpec((tm, LANES), ropemap)]
    args = [x2, g0, sc, sh, w, cos, sa, sb]
    aliases = {}
    if stacked is not None:
        aliases = {len(args): 2, len(args) + 1: 3}
        in_specs += [pl.BlockSpec(memory_space=pl.ANY)] * 2
        args += list(stacked)
    return pl.pallas_call(
        functools.partial(_inproj_kernel, tm=tm, with_means=with_means),
        grid=(nt,),
        in_specs=in_specs,
        out_specs=out_specs,
        out_shape=out_shape,
        input_output_aliases=aliases,
        compiler_params=_cp(("parallel",)),
        name="in_proj",
    )(*args)


def _foxprep_kernel(misc_ref, bias_ref, lf_ref, ccol_ref, crow_ref, *, t):
    blk = 256
    lane = lax.broadcasted_iota(i32, (t, LANES), 1)
    lf = jnp.where(lane < H_FOX, _log_sigmoid(misc_ref[...] + bias_ref[...]), 0.0)
    lf_ref[...] = lf[:, :H_FOX]
    r = lax.broadcasted_iota(i32, (blk, blk), 0)
    c = lax.broadcasted_iota(i32, (blk, blk), 1)
    ltri = (r >= c).astype(f32)
    carry = jnp.zeros((1, LANES), f32)
    for b in range(t // blk):
        cb = jnp.dot(ltri, lf[b * blk:(b + 1) * blk], preferred_element_type=f32,
                     precision=lax.Precision.HIGHEST) + carry
        ccol_ref[b * blk:(b + 1) * blk, :] = cb
        carry = cb[blk - 1:blk, :]
    er = lax.broadcasted_iota(i32, (8, LANES), 0)
    ec = lax.broadcasted_iota(i32, (8, LANES), 1)
    eye = (er == ec).astype(f32)
    crow_ref[0] = lax.dot_general(eye, ccol_ref[...], (((1,), (1,)), ((), ())),
                                  preferred_element_type=f32, precision=lax.Precision.HIGHEST)


def _foxprep(misc, bias_row, b, t):
    return pl.pallas_call(
        functools.partial(_foxprep_kernel, t=t),
        grid=(b,),
        in_specs=[pl.BlockSpec((t, LANES), lambda i: (i, 0)),
                  pl.BlockSpec((1, LANES), lambda i: (0, 0))],
        out_specs=[pl.BlockSpec((t, H_FOX), lambda i: (i, 0)),
                   pl.BlockSpec((t, LANES), lambda i: (i, 0)),
                   pl.BlockSpec((1, 8, t), lambda i: (i, 0, 0))],
        out_shape=[jax.ShapeDtypeStruct((b * t, H_FOX), f32),
                   jax.ShapeDtypeStruct((b * t, LANES), f32),
                   jax.ShapeDtypeStruct((b, 8, t), f32)],
        compiler_params=_cp(("parallel",)),
        name="fox_prep",
    )(misc, bias_row)


def _tn(a, b):
    return lax.dot_general(a, b, (((0,), (0,)), ((), ())), preferred_element_type=f32)


def _softmax_step_t(carry, st, vb):
    m, l, acc = carry
    m_new = jnp.maximum(m, jnp.max(st, axis=0, keepdims=True))
    alpha = jnp.exp(m - m_new)
    p = jnp.exp(st - m_new)
    l = alpha * l + jnp.sum(p, axis=0, keepdims=True)
    acc = alpha * acc + _tn(vb, p.astype(bf16))
    return m_new, l, acc


def _fox_kernel(q_ref, kv_ref, ccol_ref, crow_ref, o_ref, *, tq, tk):
    qi = pl.program_id(1)
    q0 = pl.multiple_of(qi * tq, tq)
    lo = lax.broadcasted_iota(i32, (tq, LANES), 1) < HALF
    top = lax.broadcasted_iota(i32, (LANES, tq), 0) < HALF
    kk = lax.broadcasted_iota(i32, (tk, tq), 0)
    qq = lax.broadcasted_iota(i32, (tk, tq), 1)
    diag = kk <= qq
    diag2 = jnp.concatenate([diag, diag], axis=1)
    zero = jnp.zeros((), bf16)
    for p in range(H_FOX // 2):
        q = q_ref[:, p * LANES:(p + 1) * LANES]
        qst = jnp.concatenate([jnp.where(lo, q, zero), jnp.where(lo, zero, q)], axis=0)
        cq0 = crow_ref[0, 2 * p:2 * p + 1, pl.ds(q0, tq)]
        cq1 = crow_ref[0, 2 * p + 1:2 * p + 2, pl.ds(q0, tq)]

        def chunk(c, carry, mask, p=p, qst=qst, cq0=cq0, cq1=cq1):
            k0 = pl.multiple_of(c * tk, tk)
            kb = kv_ref[pl.ds(k0, tk), p * LANES:(p + 1) * LANES]
            vb = kv_ref[pl.ds(k0, tk), 384 + p * LANES:384 + (p + 1) * LANES]
            ck0 = ccol_ref[pl.ds(k0, tk), 2 * p:2 * p + 1]
            ck1 = ccol_ref[pl.ds(k0, tk), 2 * p + 1:2 * p + 2]
            st = _nt(kb, qst) + jnp.concatenate([cq0 - ck0, cq1 - ck1], axis=1)
            if mask is not None:
                st = jnp.where(mask, st, NEG)
            return _softmax_step_t(carry, st, vb)

        init = (jnp.full((1, 2 * tq), NEG, f32), jnp.zeros((1, 2 * tq), f32), jnp.zeros((LANES, 2 * tq), f32))
        carry = lax.fori_loop(0, qi, lambda c, cr: chunk(c, cr, None), init)
        _, l, acc = chunk(qi, carry, diag2)
        ot = acc / l
        o_ref[:, p * LANES:(p + 1) * LANES] = jnp.where(top, ot[:, :tq], ot[:, tq:]).T.astype(bf16)


def _fox_attn(fq, fkvb, ccol, crow, b, t):
    tq, tk = FOX_TQ, FOX_TK
    assert tq == tk
    nq = t // tq
    return pl.pallas_call(
        functools.partial(_fox_kernel, tq=tq, tk=tk),
        grid=(b, nq),
        in_specs=[pl.BlockSpec((tq, 384), lambda i, j: (i * nq + j, 0)),
                  pl.BlockSpec((t, 768), lambda i, j: (i, 0)),
                  pl.BlockSpec((t, LANES), lambda i, j: (i, 0)),
                  pl.BlockSpec((1, 8, t), lambda i, j: (i, 0, 0))],
        out_specs=pl.BlockSpec((tq, 384), lambda i, j: (i * nq + j, 0)),
        out_shape=jax.ShapeDtypeStruct((b * t, 384), bf16),
        compiler_params=_cp(("parallel", "parallel")),
        name="fox_attn",
    )(fq, fkvb, ccol, crow)


def _select_topk(score, nsel, axis=1):
    nb = score.shape[axis]
    jf = lax.broadcasted_iota(i32, score.shape, axis).astype(f32)
    sel = jnp.zeros(score.shape, f32)
    firsts = []
    for _ in range(nsel):
        mx = jnp.max(score, axis=axis, keepdims=True)
        first = jnp.min(jnp.where(score == mx, jf, float(nb)), axis=axis, keepdims=True)
        hit = jf == first
        sel = jnp.where(hit, 1.0, sel)
        score = jnp.where(hit, -3.0, score)
        firsts.append(first)
    return sel, firsts


def _nsa_kernel(q_ref, kcm_ref, nkv_ref, win_ref, misc_ref, o_ref, *, tq, tk, t):
    qi = pl.program_id(1)
    s0 = qi * tq
    nbc = t // CMP_BLOCK
    lane = lax.broadcasted_iota(i32, (tq, LANES), 1)
    lo = lane < HALF
    zero = jnp.zeros((), bf16)
    kpos = lax.broadcasted_iota(i32, (tk, tq), 0)
    qpos = lax.broadcasted_iota(i32, (tk, tq), 1) + s0
    qpos_b = lax.broadcasted_iota(i32, (nbc, tq), 1) + s0
    jb = lax.broadcasted_iota(i32, (nbc, tq), 0)
    gates = jax.nn.sigmoid(misc_ref[...].T)

    def tile3(a):
        return jnp.concatenate([a, a, a], axis=1)

    def per_head(ot):
        return [ot[:, g * tq:(g + 1) * tq] for g in range(G_NSA)]

    def attend(qk, kref, klane, vlane, c_lo, c_hi, biasfn):
        def body(c, carry, last):
            k0 = pl.multiple_of(c * tk, tk)
            kb = kref[pl.ds(k0, tk), klane:klane + LANES]
            vb = kref[pl.ds(k0, tk), vlane:vlane + LANES]
            return _softmax_step_t(carry, _nt(kb, qk) + tile3(biasfn(k0, last)), vb)
        init = (jnp.full((1, 3 * tq), NEG, f32), jnp.zeros((1, 3 * tq), f32), jnp.zeros((LANES, 3 * tq), f32))
        carry = lax.fori_loop(c_lo, c_hi - 1, lambda c, cr: body(c, cr, False), init)
        _, l, acc = body(c_hi - 1, carry, True)
        return per_head(acc / l)

    kcm = kcm_ref[...]
    kc = kcm[:, :LANES].astype(bf16)
    vc = kcm[:, LANES:].astype(bf16)
    outs = []
    for kv in range(KV_NSA):
        keep = lo if kv == 0 else jnp.logical_not(lo)
        qk = jnp.concatenate([jnp.where(keep, q_ref[:, p * LANES:(p + 1) * LANES], zero)
                              for p in range(G_NSA)], axis=0)
        cmask3 = tile3(((jb + 1) * CMP_BLOCK - 1) <= qpos_b)
        sc = jnp.where(cmask3, _nt(kc, qk), NEG)
        mx = jnp.max(sc, axis=0, keepdims=True)
        pe = jnp.where(cmask3, jnp.exp(sc - mx), 0.0)
        den = jnp.sum(pe, axis=0, keepdims=True)
        pc = pe / jnp.where(den > 0.0, den, 1.0)
        o_cmp = per_head(_tn(vc, pc.astype(bf16)))
        imp = pc[:, :tq] + pc[:, tq:2 * tq] + pc[:, 2 * tq:]
        cur = qpos_b // SEL_BLOCK
        forced = (jb == 0) | (jb == cur) | (jb == cur - 1)
        score = jnp.where(forced, FORCE, imp)
        score = jnp.where(jb * SEL_BLOCK <= qpos_b, score, -1.0)
        sel, _ = _select_topk(score, min(TOPK_BLOCKS, nbc), axis=0)
        selt = jnp.concatenate([sel, jnp.zeros((LANES - nbc, tq), f32)], axis=0).astype(bf16)

        def sel_bias(k0, last, selt=selt):
            er = (lax.broadcasted_iota(i32, (tk, LANES), 0) + k0) // SEL_BLOCK
            ec = lax.broadcasted_iota(i32, (tk, LANES), 1)
            expand = jnp.where(er == ec, 1.0, 0.0).astype(bf16)
            picked = jnp.dot(expand, selt, preferred_element_type=f32)
            bias = (picked - 1.0) * (-NEG)
            return jnp.where((kpos + k0) <= qpos, bias, NEG) if last else bias

        def win_bias(k0, last):
            rel = qpos - (kpos + k0)
            return jnp.where(rel >= 0, jnp.where(rel < WINDOW, 0.0, NEG), NEG)

        o_sel = attend(qk, nkv_ref, 2 * LANES, 3 * LANES, 0, (s0 + tq + tk - 1) // tk, sel_bias)
        w_lo = jnp.maximum(s0 - WINDOW + 1, 0) // tk
        o_win = attend(qk, win_ref, 0, LANES, w_lo, (s0 + tq + tk - 1) // tk, win_bias)
        for g in range(G_NSA):
            r = GATE0 + 3 * (kv * G_NSA + g)
            outs.append(gates[r:r + 1] * o_cmp[g] + gates[r + 1:r + 2] * o_sel[g] + gates[r + 2:r + 3] * o_win[g])
    top = lax.broadcasted_iota(i32, (LANES, tq), 0) < HALF
    for p in range(G_NSA):
        o_ref[:, p * LANES:(p + 1) * LANES] = jnp.where(top, outs[p], outs[G_NSA + p]).T.astype(bf16)


def _nsa_attn(nq, kcm, nkvb, winb, misc, b, t):
    tq, tk = NSA_TQ, NSA_TK
    assert tk % tq == 0
    nq_t = t // tq
    nbc = t // CMP_BLOCK
    return pl.pallas_call(
        functools.partial(_nsa_kernel, tq=tq, tk=tk, t=t),
        grid=(b, nq_t),
        in_specs=[pl.BlockSpec((tq, 384), lambda i, j: (i * nq_t + j, 0)),
                  pl.BlockSpec((nbc, 256), lambda i, j: (i, 0)),
                  pl.BlockSpec((t, 512), lambda i, j: (i, 0)),
                  pl.BlockSpec((t, 256), lambda i, j: (i, 0)),
                  pl.BlockSpec((tq, LANES), lambda i, j: (i * nq_t + j, 0))],
        out_specs=pl.BlockSpec((tq, 384), lambda i, j: (i * nq_t + j, 0)),
        out_shape=jax.ShapeDtypeStruct((b * t, 384), bf16),
        compiler_params=_cp(("parallel", "parallel")),
        name="nsa_attn",
    )(nq, kcm, nkvb, winb, misc)


def _pool_project(d, wp_ref, ps_ref):
    return (jnp.dot(d.astype(bf16), wp_ref[...], preferred_element_type=f32) * ps_ref[...]).astype(bf16)


def _out_tail(x, ofox, onsa, opool, wo_ref, g_ref, gate):
    o = (jnp.dot(ofox, wo_ref[0:384, :], preferred_element_type=f32)
         + jnp.dot(onsa, wo_ref[384:768, :], preferred_element_type=f32)
         + jnp.dot(opool, wo_ref[768:1024, :], preferred_element_type=f32))
    return x + gate * _rms(o, g_ref[...])


def _outproj_kernel(x_ref, ofox_ref, onsa_ref, u_ref, halo_ref, wo_ref, wp_ref, ps_ref, g_ref, gate_ref,
                    o_ref, ext_ref, *, tm, t):
    i = pl.program_id(0)
    pos0 = (i * tm) % t
    u = u_ref[...]
    ext_ref[0:16, :] = jnp.where(pos0 > 0, halo_ref[...], 0.0)
    ext_ref[16:, :] = u
    pos = lax.broadcasted_iota(i32, (tm, LANES), 0) + pos0
    lo = lax.broadcasted_iota(i32, (tm, LANES), 1) < HALF

    def shifted(k, c):
        return ext_ref[16 - k:16 - k + tm, c * LANES:(c + 1) * LANES]

    def cnt(w):
        return jnp.minimum(w, pos + 1).astype(f32)

    ds = []
    for c, (wa, wb) in enumerate(((POOL_WINDOWS[0], POOL_WINDOWS[1]), (POOL_WINDOWS[2], POOL_WINDOWS[3]))):
        run = shifted(0, c)
        sums = {}
        for k in range(1, wb):
            run = run + shifted(k, c)
            if k + 1 in (wa, wb):
                sums[k + 1] = run
        ds.append(jnp.where(lo, sums[wa] / cnt(wa), sums[wb] / cnt(wb)) - u[:, c * LANES:(c + 1) * LANES])
    opool = _pool_project(jnp.concatenate(ds, axis=1), wp_ref, ps_ref)
    o_ref[...] = _out_tail(x_ref[...], ofox_ref[...], onsa_ref[...], opool, wo_ref, g_ref, gate_ref[0])


def _outproj(x2, ofox, onsa, u, wo, wp, ps, g1, gate, *, tm, t):
    m = x2.shape[0]
    row = lambda w_: pl.BlockSpec((tm, w_), lambda i: (i, 0))
    const = lambda s: pl.BlockSpec(s, lambda i: (0, 0))
    return pl.pallas_call(
        functools.partial(_outproj_kernel, tm=tm, t=t),
        grid=(m // tm,),
        in_specs=[row(D_MODEL), row(384), row(384), row(256),
                  pl.BlockSpec((16, 256), lambda i: (jnp.maximum(i * (tm // 16) - 1, 0), 0)),
                  const((D_MODEL, D_MODEL)), const((256, 256)), const((1, 256)), const((1, D_MODEL)),
                  pl.BlockSpec((1, 1, D_MODEL), lambda i: ((i * tm) // t, 0, 0))],
        out_specs=row(D_MODEL),
        out_shape=jax.ShapeDtypeStruct((m, D_MODEL), f32),
        scratch_shapes=[pltpu.VMEM((tm + 16, 256), f32)],
        compiler_params=_cp(("parallel",)),
        name="out_proj",
    )(x2, ofox, onsa, u, u, wo, wp, ps, g1, gate)


def _outproj_dec_kernel(x_ref, ofox_ref, onsa_ref, ext_ref, wo_ref, wp_ref, ps_ref, g_ref, gate_ref, o_ref, *, past):
    ext = ext_ref[...]
    n = ext.shape[1]
    r = lax.broadcasted_iota(i32, ext.shape, 1)
    lane = lax.broadcasted_iota(i32, (ext.shape[0], C_POOL), 1)
    u_new = jnp.sum(jnp.where(r == n - 1, ext, 0.0), axis=1)
    d = jnp.zeros(u_new.shape, f32)
    for g, w in enumerate(POOL_WINDOWS):
        sw = jnp.sum(jnp.where(r >= n - w, ext, 0.0), axis=1)
        d = jnp.where(lane // POOL_GW == g, sw / float(min(w, past + 1)), d)
    opool = _pool_project(d - u_new, wp_ref, ps_ref)
    o_ref[...] = _out_tail(x_ref[...], ofox_ref[...], onsa_ref[...], opool, wo_ref, g_ref, gate_ref[...])


def _outproj_dec(x2, ofox, onsa, ext, wo, wp, ps, g1, gate, past):
    m = x2.shape[0]
    full = lambda a: pl.BlockSpec(a.shape, lambda i: (0,) * a.ndim)
    args = (x2, ofox, onsa, ext, wo, wp, ps, g1, gate)
    return pl.pallas_call(
        functools.partial(_outproj_dec_kernel, past=past),
        grid=(1,),
        in_specs=[full(a) for a in args],
        out_specs=pl.BlockSpec((m, D_MODEL), lambda i: (0, 0)),
        out_shape=jax.ShapeDtypeStruct((m, D_MODEL), f32),
        compiler_params=_cp(("arbitrary",)),
        name="out_proj_dec",
    )(*args)


def _mlp_kernel(x_ref, g2_ref, sc_ref, sh_ref, w1_ref, w2_ref, g3_ref, gate_ref, o_ref, *, tf):
    x = x_ref[...]
    h = (_rms(x, g2_ref[...]) * (1.0 + sc_ref[0]) + sh_ref[0]).astype(bf16)
    acc = jnp.zeros(x.shape, f32)
    for c in range(D_FF // tf):
        a = jnp.maximum(jnp.dot(h, w1_ref[:, c * tf:(c + 1) * tf], preferred_element_type=f32), 0.0)
        acc = acc + jnp.dot((a * a).astype(bf16), w2_ref[c * tf:(c + 1) * tf, :], preferred_element_type=f32)
    o_ref[...] = x + gate_ref[0] * _rms(acc, g3_ref[...])


def _mlp(x2, g2, sc, sh, w1, w2, g3, gate, *, tm, rows_per_mod):
    m = x2.shape[0]
    mod_rows = sc.shape[1]
    modmap = lambda i: ((i * tm) // rows_per_mod, 0, 0)
    const = lambda s: pl.BlockSpec(s, lambda i: (0, 0))
    mod = pl.BlockSpec((1, mod_rows, D_MODEL), modmap)
    return pl.pallas_call(
        functools.partial(_mlp_kernel, tf=512),
        grid=(m // tm,),
        in_specs=[pl.BlockSpec((tm, D_MODEL), lambda i: (i, 0)), const((1, D_MODEL)), mod, mod,
                  const((D_MODEL, D_FF)), const((D_FF, D_MODEL)), const((1, D_MODEL)), mod],
        out_specs=pl.BlockSpec((tm, D_MODEL), lambda i: (i, 0)),
        out_shape=jax.ShapeDtypeStruct((m, D_MODEL), f32),
        compiler_params=_cp(("parallel",)),
        name="mlp",
    )(x2, g2, sc, sh, w1, w2, g3, gate)


def _pad8(a):
    return jnp.concatenate([a, jnp.zeros((8 - a.shape[0], a.shape[1]), a.dtype)], axis=0)


def _by_group(rows):
    grp = lax.broadcasted_iota(i32, (8, rows.shape[1]), 0) // G_NSA
    return jnp.where(grp == 0, jnp.broadcast_to(rows[0:1], (8, rows.shape[1])),
                     jnp.broadcast_to(rows[1:2], (8, rows.shape[1])))


def _lane_to_sublane(row, offset, stride):
    sub = lax.broadcasted_iota(i32, (8, LANES), 0)
    lane = lax.broadcasted_iota(i32, (8, LANES), 1)
    return jnp.sum(jnp.where(lane == offset + stride * sub, jnp.broadcast_to(row, (8, LANES)), 0.0),
                   axis=1, keepdims=True)


def _fox_dec_kernel(pt_ref, fq_ref, misc_ref, bias_ref, knew_ref, *rest, pps):
    lf_refs = rest[:pps]
    kv_refs = rest[pps:2 * pps]
    o_ref, lfo_ref, m_scr, l_scr, acc_scr, c_scr = rest[2 * pps:]
    g = pl.program_id(1)
    n = pps * PAGE_SIZE
    q6f = _pad8(fq_ref[0])
    q6 = q6f.astype(bf16)
    lane = lax.broadcasted_iota(i32, (8, LANES), 1)

    @pl.when(g == 0)
    def _():
        lfrow = jnp.where(lane[0:1] < H_FOX, _log_sigmoid(misc_ref[0] + bias_ref[...]), 0.0)
        lfo_ref[0] = lfrow
        m_scr[...] = jnp.sum(q6f * _pad8(knew_ref[0, 0]), axis=1, keepdims=True)
        l_scr[...] = jnp.ones((8, 1), f32)
        acc_scr[...] = _pad8(knew_ref[0, 1])
        c_scr[...] = _lane_to_sublane(lfrow, 0, 1)

    carry = c_scr[...]
    biases = []
    n_pages = pl.num_programs(1) * pps
    for i in range(pps):
        row = pt_ref[pl.program_id(0), n_pages - 1 - (g * pps + i)] % 8
        lf = _pad8(jnp.concatenate([lf_refs[i][0, h, 0, pl.ds(row, 1), :] for h in range(H_FOX)], axis=0))
        suf = lf
        for sft in (1, 2, 4, 8, 16, 32, 64):
            suf = suf + jnp.where(lane + sft < LANES, pltpu.roll(suf, LANES - sft, 1), 0.0)
        biases.append(carry + (suf - lf))
        carry = carry + suf[:, 0:1]
    c_scr[...] = carry
    s = jnp.concatenate(biases, axis=1)
    sub = lax.broadcasted_iota(i32, (8, n), 0)
    for h in range(H_FOX):
        kht = jnp.concatenate([kv_refs[i][0, 0, 0, h] for i in range(pps)], axis=1).astype(bf16)
        s = jnp.where(sub == h, s + jnp.dot(q6, kht, preferred_element_type=f32), s)
    m = m_scr[...]
    m_new = jnp.maximum(m, jnp.max(s, axis=1, keepdims=True))
    alpha = jnp.exp(m - m_new)
    p = jnp.exp(s - m_new)
    l_scr[...] = alpha * l_scr[...] + jnp.sum(p, axis=1, keepdims=True)
    m_scr[...] = m_new
    pb = p.astype(bf16)
    sub_d = lax.broadcasted_iota(i32, (8, HEAD_DIM), 0)
    pv = jnp.zeros((8, HEAD_DIM), f32)
    for h in range(H_FOX):
        vht = jnp.concatenate([kv_refs[i][0, 0, 1, h] for i in range(pps)], axis=1).astype(bf16)
        pv = jnp.where(sub_d == h, _nt(pb, vht), pv)
    acc_scr[...] = alpha * acc_scr[...] + pv

    @pl.when(g == pl.num_programs(1) - 1)
    def _():
        o_ref[0] = acc_scr[...] / l_scr[...]


def _fox_decode(layer, page_table, fq, misc, bias_row, fkv_new, lf_t, cache_kv):
    b, n_pages = page_table.shape
    pps = FOX_PPS
    ng = n_pages // pps
    row3 = lambda w_: pl.BlockSpec((1, 1, w_), lambda i, g, pt: (i, 0, 0))

    def page_id(bi, g, pt, i):
        return pt[jnp.minimum(bi, b - 1), n_pages - 1 - (jnp.minimum(g, ng - 1) * pps + i)]

    def page(i, shape):
        zeros = (0,) * len(shape)
        return pl.BlockSpec((1, 1) + shape, lambda bi, g, pt, i=i: (layer, page_id(bi, g, pt, i)) + zeros)

    def lf_group(i):
        return pl.BlockSpec((1, H_FOX, 1, 8, PAGE_SIZE),
                            lambda bi, g, pt, i=i: (layer, 0, page_id(bi, g, pt, i) // 8, 0, 0))

    grid_spec = pltpu.PrefetchScalarGridSpec(
        num_scalar_prefetch=1,
        grid=(b, ng),
        in_specs=[pl.BlockSpec((1, H_FOX, HEAD_DIM), lambda i, g, pt: (i, 0, 0)), row3(LANES),
                  pl.BlockSpec((1, LANES), lambda i, g, pt: (0, 0)),
                  pl.BlockSpec((1, 2, H_FOX, HEAD_DIM), lambda i, g, pt: (i, 0, 0, 0))]
                 + [lf_group(i) for i in range(pps)]
                 + [page(i, (2, H_FOX, HEAD_DIM, PAGE_SIZE)) for i in range(pps)],
        out_specs=[pl.BlockSpec((1, 8, HEAD_DIM), lambda i, g, pt: (i, 0, 0)), row3(LANES)],
        scratch_shapes=[pltpu.VMEM((8, 1), f32), pltpu.VMEM((8, 1), f32), pltpu.VMEM((8, HEAD_DIM), f32),
                        pltpu.VMEM((8, 1), f32)],
    )
    return pl.pallas_call(
        functools.partial(_fox_dec_kernel, pps=pps),
        grid_spec=grid_spec,
        out_shape=[jax.ShapeDtypeStruct((b, 8, HEAD_DIM), f32), jax.ShapeDtypeStruct((b, 1, LANES), f32)],
        compiler_params=_cp(("parallel", "arbitrary")),
        name="fox_decode",
    )(page_table, fq, misc, bias_row, fkv_new, *([lf_t] * pps), *([cache_kv] * pps))


def _nsa_cmp_dec_kernel(pt_ref, q_ref, *rest, pps, past):
    pg_refs = rest[:pps]
    p_ref, o_ref, mean_scr = rest[pps:]
    g = pl.program_id(1)
    per_page = PAGE_SIZE // CMP_BLOCK
    rows = 2 * KV_NSA * HEAD_DIM
    x = jnp.concatenate([pg_refs[i][0, 0].reshape(rows, PAGE_SIZE) for i in range(pps)], axis=1)
    r = lax.broadcasted_iota(i32, (pps * PAGE_SIZE, LANES), 0)
    c = lax.broadcasted_iota(i32, (pps * PAGE_SIZE, LANES), 1)
    avg = jnp.where(r // CMP_BLOCK == c, 1.0 / CMP_BLOCK, 0.0).astype(bf16)
    hi = x.astype(bf16)
    lo = (x - hi.astype(f32)).astype(bf16)
    mean_scr[g] = jnp.dot(hi, avg, preferred_element_type=f32) + jnp.dot(lo, avg, preferred_element_type=f32)

    @pl.when(g == pl.num_programs(1) - 1)
    def _():
        ng = mean_scr.shape[0]
        nbc = ng * pps * per_page
        mean = mean_scr[0]
        for gg in range(1, ng):
            mean = mean + pltpu.roll(mean_scr[gg], gg * pps * per_page, 1)
        mean = mean[:, :nbc].astype(bf16)
        part = lambda feat, kv: mean[(feat * KV_NSA + kv) * HEAD_DIM:(feat * KV_NSA + kv + 1) * HEAD_DIM]
        q6 = _pad8(q_ref[0]).astype(bf16)
        grp0 = lax.broadcasted_iota(i32, (8, 1), 0) // G_NSA == 0
        s = jnp.where(grp0, jnp.dot(q6, part(0, 0), preferred_element_type=f32),
                      jnp.dot(q6, part(0, 1), preferred_element_type=f32))
        j = lax.broadcasted_iota(i32, (8, nbc), 1)
        mask = ((j + 1) * CMP_BLOCK - 1) <= past
        s = jnp.where(mask, s, NEG)
        pe = jnp.where(mask, jnp.exp(s - jnp.max(s, axis=1, keepdims=True)), 0.0)
        den = jnp.sum(pe, axis=1, keepdims=True)
        p = pe / jnp.where(den > 0.0, den, 1.0)
        p_ref[0] = p
        pb = p.astype(bf16)
        o_ref[0] = jnp.where(grp0, _nt(pb, part(1, 0)), _nt(pb, part(1, 1)))


def _nsa_cmp_decode(layer, page_table, nq, cache_nsa, past):
    b, n_pages = page_table.shape
    pps = CMP_PPS
    nbc = past // CMP_BLOCK
    grid_spec = pltpu.PrefetchScalarGridSpec(
        num_scalar_prefetch=1,
        grid=(b, n_pages // pps),
        in_specs=[pl.BlockSpec((1, H_NSA, HEAD_DIM), lambda i, g, pt: (i, 0, 0))]
                 + [pl.BlockSpec((1, 1, 2, KV_NSA, HEAD_DIM, PAGE_SIZE),
                                 lambda bi, g, pt, i=i: (layer, pt[jnp.minimum(bi, b - 1),
                                                                   jnp.minimum(g, n_pages // pps - 1) * pps + i],
                                                         0, 0, 0, 0))
                    for i in range(pps)],
        out_specs=[pl.BlockSpec((1, 8, nbc), lambda i, g, pt: (i, 0, 0)),
                   pl.BlockSpec((1, 8, HEAD_DIM), lambda i, g, pt: (i, 0, 0))],
        scratch_shapes=[pltpu.VMEM((n_pages // pps, 2 * KV_NSA * HEAD_DIM, LANES), f32)],
    )
    return pl.pallas_call(
        functools.partial(_nsa_cmp_dec_kernel, pps=pps, past=past),
        grid_spec=grid_spec,
        out_shape=[jax.ShapeDtypeStruct((b, 8, nbc), f32), jax.ShapeDtypeStruct((b, 8, HEAD_DIM), f32)],
        compiler_params=_cp(("parallel", "arbitrary")),
        name="nsa_cmp_decode",
    )(page_table, nq, *([cache_nsa] * pps))


def _nsa_topk_dec_kernel(p_ref, idx_ref, *, past, width):
    b = p_ref.shape[0]
    nbc = p_ref.shape[2]
    nbs = -(-(past + 1) // SEL_BLOCK)
    imps = []
    for kv in range(KV_NSA):
        imps.append(p_ref[:, kv * G_NSA, :] + p_ref[:, kv * G_NSA + 1, :] + p_ref[:, kv * G_NSA + 2, :])
    imp = jnp.concatenate(imps, axis=0)
    imp = jnp.concatenate([imp, jnp.zeros((2 * b, width - nbc), f32)], axis=1)
    j = lax.broadcasted_iota(i32, (2 * b, width), 1)
    cur = past // SEL_BLOCK
    forced = (j == 0) | (j == cur) | (j == cur - 1)
    score = jnp.where(forced, FORCE, imp)
    score = jnp.where(j * SEL_BLOCK <= past, score, -1.0)
    score = jnp.where(j < nbs, score, -2.0)
    _, firsts = _select_topk(score, min(TOPK_BLOCKS, nbs))
    lane = lax.broadcasted_iota(i32, (2 * b, LANES), 1)
    out = jnp.zeros((2 * b, LANES), i32)
    for k, first in enumerate(firsts):
        out = jnp.where(lane == k, first.astype(i32), out)
    idx_ref[...] = out


def _nsa_topk_decode(p8, past):
    b, _, nbc = p8.shape
    width = 2 * nbc
    return pl.pallas_call(
        functools.partial(_nsa_topk_dec_kernel, past=past, width=width),
        grid=(1,),
        in_specs=[pl.BlockSpec(p8.shape, lambda i: (0, 0, 0))],
        out_specs=pl.BlockSpec((2 * b, LANES), lambda i: (0, 0)),
        out_shape=jax.ShapeDtypeStruct((2 * b, LANES), i32),
        compiler_params=_cp(("arbitrary",)),
        name="nsa_topk_decode",
    )(p8)


def _nsa_sel_dec_kernel(pt_ref, idx_ref, q_ref, misc_ref, ocmp_ref, nkv_ref, wnew_ref, win_ref, *rest, past, nsel):
    blk_refs = rest[:KV_NSA * nsel]
    o_ref = rest[KV_NSA * nsel]
    b = pl.program_id(0)
    n_past_blk = past // SEL_BLOCK
    q8f = _pad8(q_ref[0])
    q8 = q8f.astype(bf16)
    sub = lax.broadcasted_iota(i32, (8, 1), 0)
    grp0 = sub // G_NSA == 0
    per_page = PAGE_SIZE // SEL_BLOCK
    lane_blk = lax.broadcasted_iota(i32, (8, PAGE_SIZE), 1) // SEL_BLOCK
    m = jnp.full((8, 1), NEG, f32)
    l = jnp.zeros((8, 1), f32)
    acc = jnp.zeros((8, HEAD_DIM), f32)
    s_new = jnp.sum(q8f * _by_group(nkv_ref[0, 2]), axis=1, keepdims=True)
    v_new = _by_group(nkv_ref[0, 3])
    for kv in range(KV_NSA):
        in_group = (sub // G_NSA) == kv
        has_new = jnp.zeros((), jnp.bool_)
        for k in range(nsel):
            j = idx_ref[kv * pl.num_programs(0) + b, k]
            has_new = has_new | (j == n_past_blk)
            blk = blk_refs[kv * nsel + k]
            mask = in_group & (j < n_past_blk) & (lane_blk == j % per_page)
            s = jnp.where(mask, jnp.dot(q8, blk[0, 0, 0, 0].astype(bf16), preferred_element_type=f32), NEG)
            m_new = jnp.maximum(m, jnp.max(s, axis=1, keepdims=True))
            alpha = jnp.exp(m - m_new)
            pe = jnp.where(mask, jnp.exp(s - m_new), 0.0)
            l = alpha * l + jnp.sum(pe, axis=1, keepdims=True)
            acc = alpha * acc + _nt(pe.astype(bf16), blk[0, 0, 1, 0].astype(bf16))
            m = m_new
        new_ok = in_group & has_new
        m_new = jnp.maximum(m, jnp.where(new_ok, s_new, NEG))
        alpha = jnp.exp(m - m_new)
        p_new = jnp.where(new_ok, jnp.exp(s_new - m_new), 0.0)
        l = alpha * l + p_new
        acc = alpha * acc + p_new * v_new
        m = m_new
    o_sel = acc / jnp.where(l > 0.0, l, 1.0)
    wb = win_ref.shape[5]
    kpos = past - wb + lax.broadcasted_iota(i32, (8, wb), 1)
    rel = past - kpos
    wmask = (rel >= 0) & (rel < WINDOW) & (kpos >= 0)
    s = jnp.where(grp0, jnp.dot(q8, win_ref[0, 0, 0, 0].astype(bf16), preferred_element_type=f32),
                  jnp.dot(q8, win_ref[0, 0, 0, 1].astype(bf16), preferred_element_type=f32))
    s = jnp.where(wmask, s, NEG)
    s_w = jnp.sum(q8f * _by_group(wnew_ref[0, 0]), axis=1, keepdims=True)
    m = jnp.maximum(jnp.max(s, axis=1, keepdims=True), s_w)
    p = jnp.where(wmask, jnp.exp(s - m), 0.0)
    p_w = jnp.exp(s_w - m)
    pb = p.astype(bf16)
    pv = jnp.where(grp0, _nt(pb, win_ref[0, 0, 1, 0].astype(bf16)), _nt(pb, win_ref[0, 0, 1, 1].astype(bf16)))
    o_win = (pv + p_w * _by_group(wnew_ref[0, 1])) / (jnp.sum(p, axis=1, keepdims=True) + p_w)
    misc = misc_ref[0]
    gates = [jax.nn.sigmoid(_lane_to_sublane(misc, GATE0 + r, 3)) for r in range(3)]
    o_ref[0] = gates[0] * ocmp_ref[0] + gates[1] * o_sel + gates[2] * o_win


def _nsa_sel_decode(layer, page_table, idx, nq, misc, ocmp, nkv_new, win_new, cache_win, cache_nsa, past):
    b = page_table.shape[0]
    nsel = idx.shape[1]
    n_past_blk = past // SEL_BLOCK
    per_page = PAGE_SIZE // SEL_BLOCK
    wb = cache_win.shape[5]

    def blk_spec(kv, k):
        def imap(i, pt, ix):
            ii = jnp.minimum(i, b - 1)
            jp = jnp.clip(ix[kv * b + ii, k], 0, n_past_blk - 1)
            return (layer, pt[ii, jp // per_page], 1, kv, 0, 0)
        return pl.BlockSpec((1, 1, 2, 1, HEAD_DIM, PAGE_SIZE), imap)

    grid_spec = pltpu.PrefetchScalarGridSpec(
        num_scalar_prefetch=2,
        grid=(b,),
        in_specs=[pl.BlockSpec((1, H_NSA, HEAD_DIM), lambda i, pt, ix: (i, 0, 0)),
                  pl.BlockSpec((1, 1, LANES), lambda i, pt, ix: (i, 0, 0)),
                  pl.BlockSpec((1, 8, HEAD_DIM), lambda i, pt, ix: (i, 0, 0)),
                  pl.BlockSpec((1, 4, KV_NSA, HEAD_DIM), lambda i, pt, ix: (i, 0, 0, 0)),
                  pl.BlockSpec((1, 2, KV_NSA, HEAD_DIM), lambda i, pt, ix: (i, 0, 0, 0)),
                  pl.BlockSpec((1, 1, 2, KV_NSA, HEAD_DIM, wb), lambda i, pt, ix: (layer, i, 0, 0, 0, 0))]
                 + [blk_spec(kv, k) for kv in range(KV_NSA) for k in range(nsel)],
        out_specs=pl.BlockSpec((1, 8, HEAD_DIM), lambda i, pt, ix: (i, 0, 0)),
    )
    return pl.pallas_call(
        functools.partial(_nsa_sel_dec_kernel, past=past, nsel=nsel),
        grid_spec=grid_spec,
        out_shape=jax.ShapeDtypeStruct((b, 8, HEAD_DIM), f32),
        compiler_params=_cp(("parallel",)),
        name="nsa_sel_decode",
    )(page_table, idx, nq, misc, ocmp, nkv_new, win_new, cache_win, *([cache_nsa] * (KV_NSA * nsel)))


def _proj_columns():
    off = np.cumsum([0, 384, 384, 384, 6, 384, 128, 128, 128, 128, 128, 128, 18, 256])
    fq, fk, fv, ff, nq, nkc, nvc, nks, nvs, nkw, nvw, ng, u = [int(o) for o in off[:13]]
    cols = list(range(fq, fq + 384))
    for h in NSA_PAIR_ORDER:
        cols += list(range(nq + h * HEAD_DIM, nq + (h + 1) * HEAD_DIM))
    cols += list(range(fk, fk + 768))
    cols += list(range(nkc, nkc + 512))
    cols += list(range(nkw, nkw + 256))
    cols += list(range(u, u + 256))
    cols += list(range(ff, ff + 6)) + list(range(ng, ng + 18)) + [N_IN] * (LANES - 24)
    return np.asarray(cols, np.int32)


def _out_rows():
    rows = list(range(0, 384))
    for h in NSA_PAIR_ORDER:
        rows += list(range(384 + h * HEAD_DIM, 384 + (h + 1) * HEAD_DIM))
    rows += list(range(768, 1024))
    return np.asarray(rows, np.int32)


def _rope_tables(pos):
    half = ROT_DIM // 2
    inv = ROPE_THETA ** (-jnp.arange(0, ROT_DIM, 2, dtype=f32) / ROT_DIM)
    ang = pos.astype(f32)[:, None] * inv[None, :]
    cos, sin = jnp.cos(ang), jnp.sin(ang)
    n = pos.shape[0]
    one = jnp.ones((n, HEAD_DIM - ROT_DIM), f32)
    zero8 = jnp.zeros((n, half), f32)
    zrest = jnp.zeros((n, HEAD_DIM - ROT_DIM), f32)
    c = jnp.concatenate([cos, cos, one], axis=1)
    sa = jnp.concatenate([-sin, zero8, zrest], axis=1)
    sb = jnp.concatenate([zero8, sin, zrest], axis=1)
    return tuple(jnp.concatenate([a, a], axis=1) for a in (c, sa, sb))


def kernel(x_prompt, x_sample, cache_fox_kv, cache_fox_logf, cache_nsa_kv, cache_nsa_win, state_pool, page_table,
           c_prompt, c_sample, w_ada, b_ada, norm_g, w_in, b_fox_f, w_out, w_pool, pool_scale, w_ff1, w_ff2):
    depth = w_in.shape[0]
    bp, t, _ = x_prompt.shape
    bs = x_sample.shape[0]
    past = page_table.shape[1] * PAGE_SIZE
    mp = bp * t

    cols = _proj_columns()
    w_in_p = jnp.concatenate([w_in, jnp.zeros((depth, D_MODEL, 1), f32)], axis=2)[:, :, cols].astype(bf16)
    w_out_p = w_out[:, _out_rows(), :].astype(bf16)
    w_pool_bd = jnp.zeros((depth, C_POOL, C_POOL), f32)
    for g in range(len(POOL_WINDOWS)):
        w_pool_bd = w_pool_bd.at[:, g * POOL_GW:(g + 1) * POOL_GW, g * POOL_GW:(g + 1) * POOL_GW].set(w_pool[:, g])
    w_pool_bd = w_pool_bd.astype(bf16)
    w1 = w_ff1.astype(bf16)
    w2 = w_ff2.astype(bf16)
    bias_rows = jnp.pad(b_fox_f, ((0, 0), (0, LANES - H_FOX))).reshape(depth, 1, LANES)

    rope_p = _rope_tables(jnp.arange(t))
    rope_s = _rope_tables(jnp.full((bs,), past, i32))

    to_last = (0, 1, 3, 4, 5, 2)
    fox_t = jnp.transpose(cache_fox_kv, to_last)
    nsa_t = jnp.transpose(cache_nsa_kv, to_last)
    win_t = jnp.transpose(cache_nsa_win, to_last)
    n_pool = cache_fox_logf.shape[1]
    assert n_pool % 8 == 0
    lf_t = jnp.transpose(cache_fox_logf, (0, 3, 1, 2)).reshape(depth, H_FOX, n_pool // 8, 8, PAGE_SIZE)
    nat_from_pair = np.argsort(np.asarray(NSA_PAIR_ORDER))

    mod = _ada(jnp.concatenate([c_prompt, c_sample], axis=0), w_ada, b_ada)
    mod = mod.reshape(depth, bp + bs, 6, D_MODEL)

    yp = x_prompt.reshape(mp, D_MODEL)
    ys = x_sample.reshape(bs, D_MODEL)
    sp, ss = [], []
    kv_stacks = None
    for l in range(depth):
        g = norm_g[l].reshape(4, 1, D_MODEL)
        modp = [mod[l, :bp, k].reshape(bp, 1, D_MODEL) for k in range(6)]
        mods = [mod[l, bp:, k].reshape(1, bs, D_MODEL) for k in range(6)]

        (fq, nq, fkv, nkv, win, u, misc, fkvb, nkvb, winb, kcm) = _inproj(
            yp, g[0], modp[1], modp[0], w_in_p[l], *rope_p, tm=512, rows_per_mod=t, rope_rows=t, with_means=True,
            layer=l, depth=depth, stacked=kv_stacks)
        kv_stacks = (fkv, nkv)
        logf, ccol, crow = _foxprep(misc, bias_rows[l], bp, t)
        o_fox = _fox_attn(fq, fkvb, ccol, crow, bp, t)
        o_nsa = _nsa_attn(nq, kcm, nkvb, winb, misc, bp, t)
        y1 = _outproj(yp, o_fox, o_nsa, u, w_out_p[l], w_pool_bd[l], pool_scale[l].reshape(1, C_POOL), g[1],
                      modp[2], tm=512, t=t)
        yp = _mlp(y1, g[2], modp[4], modp[3], w1[l], w2[l], g[3], modp[5], tm=512, rows_per_mod=t)
        wl = min(WINDOW, t)
        sp.append((logf.reshape(bp, t, H_FOX),
                   win.reshape(bp, t, 2, KV_NSA, HEAD_DIM)[:, t - wl:],
                   u.reshape(bp, t, C_POOL)[:, t - POOL_STATE:]))

        (fq_s, nq_s, fkv_s, nkv_s, win_s, u_s, misc_s, _, _, _, _) = _inproj(
            ys, g[0], mods[1], mods[0], w_in_p[l], *rope_s, tm=bs, rows_per_mod=bs, rope_rows=bs, with_means=False)
        r3 = lambda a: a.reshape(bs, 1, a.shape[-1])
        fq6 = fq_s.astype(f32).reshape(bs, H_FOX, HEAD_DIM)
        nq6 = nq_s.astype(f32).reshape(bs, H_NSA, HEAD_DIM)[:, nat_from_pair]
        fkv_s, nkv_s = fkv_s[0], nkv_s[0]
        nkv4 = nkv_s.reshape(bs, 4, KV_NSA, HEAD_DIM)
        win2 = win_s.reshape(bs, 2, KV_NSA, HEAD_DIM)
        o_fox_s, logf_s = _fox_decode(l, page_table, fq6, r3(misc_s), bias_rows[l],
                                      fkv_s.reshape(bs, 2, H_FOX, HEAD_DIM), lf_t, fox_t)
        p8, o_cmp = _nsa_cmp_decode(l, page_table, nq6, nsa_t, past)
        idx = _nsa_topk_decode(p8, past)[:, :min(TOPK_BLOCKS, -(-(past + 1) // SEL_BLOCK))]
        o_nsa_s = _nsa_sel_decode(l, page_table, idx, nq6, r3(misc_s), o_cmp, nkv4, win2,
                                  win_t, nsa_t, past)
        o_fox_s = o_fox_s[:, :H_FOX].reshape(bs, 384).astype(bf16)
        o_nsa_s = o_nsa_s[:, np.asarray(NSA_PAIR_ORDER)].reshape(bs, 384).astype(bf16)
        u_ext = jnp.concatenate([state_pool[l], u_s.reshape(bs, 1, C_POOL)], axis=1)
        y1s = _outproj_dec(ys, o_fox_s, o_nsa_s, u_ext, w_out_p[l],
                           w_pool_bd[l], pool_scale[l].reshape(1, C_POOL), g[1], mods[2][0], past)
        ys = _mlp(y1s, g[2], mods[4], mods[3], w1[l], w2[l], g[3], mods[5], tm=bs, rows_per_mod=bs)
        win_all = jnp.concatenate([cache_nsa_win[l], win2[:, None]], axis=1)
        ss.append((fkv_s.reshape(bs, 1, 2, H_FOX, HEAD_DIM), logf_s[:, :, :H_FOX], nkv4[:, None],
                   win_all[:, 1:], u_ext[:, 1:]))

    stk = lambda lst, i: jnp.stack([s[i] for s in lst], axis=0)
    fkv_all, nkv_all = kv_stacks
    return (yp.reshape(bp, t, D_MODEL), ys.reshape(bs, 1, D_MODEL),
            fkv_all.reshape(depth, bp, t, 2, H_FOX, HEAD_DIM), stk(sp, 0),
            nkv_all.reshape(depth, bp, t, 4, KV_NSA, HEAD_DIM), stk(sp, 1), stk(sp, 2),
            stk(ss, 0), stk(ss, 1), stk(ss, 2), stk(ss, 3), stk(ss, 4))
```

```python
import functools

import numpy as np
import jax
import jax.numpy as jnp
from jax import lax
from jax.experimental import pallas as pl
from jax.experimental.pallas import tpu as pltpu

f32 = jnp.float32
bf16 = jnp.bfloat16
i32 = jnp.int32

D_MODEL = 1024
HEAD_DIM = 64
H_FOX = 6
H_NSA = 6
KV_NSA = 2
G_NSA = H_NSA // KV_NSA
POOL_WINDOWS = (2, 4, 8, 16)
C_POOL = 256
POOL_GW = 64
POOL_STATE = 15
ROT_DIM = 16
ROPE_THETA = 500000.0
CMP_BLOCK = 64
SEL_BLOCK = 64
TOPK_BLOCKS = 8
WINDOW = 512
PAGE_SIZE = 128
D_FF = 4 * D_MODEL
EPS = 1e-6
NEG = -1e30
FORCE = 1e4
SCALE = HEAD_DIM ** -0.5
N_IN = 2584

LANES = 128
HALF = LANES // 2
VMEM_LIMIT = 56 * 1024 * 1024

C_FQ = 0
C_NQ = 384
C_FKV = 768
C_NKV = 1536
C_WIN = 2048
C_U = 2304
C_MISC = 2560
N_PROJ = 2688
GATE0 = H_FOX
NSA_PAIR_ORDER = (0, 3, 1, 4, 2, 5)

FOX_TQ = 512
FOX_TK = 512
NSA_TQ = 256
NSA_TK = 512
FOX_PPS = 16
CMP_PPS = 16


def _cp(sem):
    return pltpu.CompilerParams(dimension_semantics=sem, vmem_limit_bytes=VMEM_LIMIT)


def _nt(a, b):
    return lax.dot_general(a, b, (((1,), (1,)), ((), ())), preferred_element_type=f32)


def _rms(x, g):
    return x * lax.rsqrt(jnp.mean(x * x, axis=-1, keepdims=True) + EPS) * g


def _log_sigmoid(x):
    return jnp.minimum(x, 0.0) - jnp.log1p(jnp.exp(-jnp.abs(x)))


def _ada_kernel(c_ref, w_ref, b_ref, o_ref):
    c = c_ref[...]
    a = (c * jax.nn.sigmoid(c)).astype(bf16)
    o_ref[0] = jnp.dot(a, w_ref[0].astype(bf16), preferred_element_type=f32) + b_ref[0]


def _ada(c_all, w_ada, b_ada):
    depth = w_ada.shape[0]
    n = c_all.shape[0]
    tn = 1024
    return pl.pallas_call(
        _ada_kernel,
        grid=(depth, 6 * D_MODEL // tn),
        in_specs=[pl.BlockSpec((n, D_MODEL), lambda l, j: (0, 0)),
                  pl.BlockSpec((1, D_MODEL, tn), lambda l, j: (l, 0, j)),
                  pl.BlockSpec((1, 1, tn), lambda l, j: (l, 0, j))],
        out_specs=pl.BlockSpec((1, n, tn), lambda l, j: (l, 0, j)),
        out_shape=jax.ShapeDtypeStruct((depth, n, 6 * D_MODEL), f32),
        compiler_params=_cp(("parallel", "parallel")),
        name="ada_mod",
    )(c_all, w_ada, b_ada.reshape(depth, 1, 6 * D_MODEL))


def _inproj_kernel(x_ref, g_ref, sc_ref, sh_ref, w_ref, cos_ref, sa_ref, sb_ref, *rest, tm, with_means, state_t):
    (fq_ref, nq_ref, fkv_ref, nkv_ref, win_ref, u_ref, misc_ref, fkvb_ref, nkvb_ref, winb_ref, kcm_ref) = rest[-11:]
    x = x_ref[...]
    h = (_rms(x, g_ref[...]) * (1.0 + sc_ref[0]) + sh_ref[0]).astype(bf16)
    cos = cos_ref[...]
    sa = sa_ref[...]
    sb = sb_ref[...]

    def rope(z):
        return z * cos + pltpu.roll(z, LANES - ROT_DIM // 2, 1) * sa + pltpu.roll(z, ROT_DIM // 2, 1) * sb

    def means(z):
        return jnp.sum(z.reshape(tm // CMP_BLOCK, CMP_BLOCK, LANES), axis=1) * (1.0 / CMP_BLOCK)

    def state(ref, col, z):
        if state_t:
            ref[0, 0, col:col + LANES, :] = z.T
        else:
            ref[:, col:col + LANES] = z

    def emit(blk, z):
        c = blk * LANES
        if c < C_NQ:
            fq_ref[:, c - C_FQ:c - C_FQ + LANES] = (z * SCALE).astype(bf16)
        elif c < C_FKV:
            nq_ref[:, c - C_NQ:c - C_NQ + LANES] = (rope(z) * SCALE).astype(bf16)
        elif c < C_NKV:
            state(fkv_ref, c - C_FKV, z)
            fkvb_ref[:, c - C_FKV:c - C_FKV + LANES] = z.astype(bf16)
        elif c < C_WIN:
            o = c - C_NKV
            if o in (0, 2 * LANES):
                z = rope(z)
            state(nkv_ref, o, z)
            nkvb_ref[:, o:o + LANES] = z.astype(bf16)
            if with_means and o < 2 * LANES:
                kcm_ref[:, o:o + LANES] = means(z)
        elif c < C_U:
            o = c - C_WIN
            if o == 0:
                z = rope(z)
            win_ref[:, o:o + LANES] = z
            winb_ref[:, o:o + LANES] = z.astype(bf16)
        elif c < C_MISC:
            u_ref[:, c - C_U:c - C_U + LANES] = z
        else:
            misc_ref[...] = z

    nblk = N_PROJ // LANES
    for b0 in range(0, nblk, 2):
        nb = min(2, nblk - b0)
        z2 = jnp.dot(h, w_ref[:, b0 * LANES:(b0 + nb) * LANES], preferred_element_type=f32)
        for k in range(nb):
            emit(b0 + k, z2[:, k * LANES:(k + 1) * LANES])
    if not with_means:
        kcm_ref[...] = jnp.zeros(kcm_ref.shape, f32)


def _inproj(x2, g0, sc, sh, w, cos, sa, sb, *, tm, rows_per_mod, rope_rows, with_means, layer=0, depth=None,
            stacked=None):
    m = x2.shape[0]
    nt = m // tm
    mod_rows = sc.shape[1]
    rope_blocks = rope_rows // tm
    kc_rows = max(tm // CMP_BLOCK, 8)
    modmap = lambda i: ((i * tm) // rows_per_mod, 0, 0)
    ropemap = lambda i: (i % rope_blocks, 0)
    row = lambda w_: pl.BlockSpec((tm, w_), lambda i: (i, 0))
    state_t = depth is not None
    if state_t:
        seqs = m // rope_rows
        state_shape = lambda w_: (depth, seqs, w_, rope_rows)
        state_spec = lambda w_: pl.BlockSpec((1, 1, w_, tm), lambda i: (layer, i // rope_blocks, 0, i % rope_blocks))
    else:
        state_shape = lambda w_: (m, w_)
        state_spec = row
    outs = [((m, 384), bf16), ((m, 384), bf16), (state_shape(768), f32), (state_shape(512), f32), ((m, 256), f32),
            ((m, 256), f32), ((m, LANES), f32), ((m, 768), bf16), ((m, 512), bf16), ((m, 256), bf16)]
    out_shape = [jax.ShapeDtypeStruct(s, d) for s, d in outs] + [jax.ShapeDtypeStruct((nt * kc_rows, 256), f32)]
    out_specs = ([row(s[-1]) for s, _ in outs[:2]] + [state_spec(768), state_spec(512)]
                 + [row(s[-1]) for s, _ in outs[4:]] + [pl.BlockSpec((kc_rows, 256), lambda i: (i, 0))])
    in_specs = [row(D_MODEL),
                pl.BlockSpec((1, D_MODEL), lambda i: (0, 0)),
                pl.BlockSpec((1, mod_rows, D_MODEL), modmap),
                pl.BlockSpec((1, mod_rows, D_MODEL), modmap),
                pl.BlockSpec((D_MODEL, N_PROJ), lambda i: (0, 0)),
                pl.BlockSpec((tm, LANES), ropemap),
                pl.BlockSpec((tm, LANES), ropemap),
                pl.BlockSpec((tm, LANES), ropemap)]
    args = [x2, g0, sc, sh, w, cos, sa, sb]
    aliases = {}
    if stacked is not None:
        aliases = {len(args): 2, len(args) + 1: 3}
        in_specs += [pl.BlockSpec(memory_space=pl.ANY)] * 2
        args += list(stacked)
    return pl.pallas_call(
        functools.partial(_inproj_kernel, tm=tm, with_means=with_means, state_t=state_t),
        grid=(nt,),
        in_specs=in_specs,
        out_specs=out_specs,
        out_shape=out_shape,
        input_output_aliases=aliases,
        compiler_params=_cp(("parallel",)),
        name="in_proj",
    )(*args)


def _foxprep_kernel(misc_ref, bias_ref, lf_ref, ccol_ref, crow_ref, *, t):
    blk = 256
    lane = lax.broadcasted_iota(i32, (t, LANES), 1)
    lf = jnp.where(lane < H_FOX, _log_sigmoid(misc_ref[...] + bias_ref[...]), 0.0)
    lf_ref[...] = lf[:, :H_FOX]
    r = lax.broadcasted_iota(i32, (blk, blk), 0)
    c = lax.broadcasted_iota(i32, (blk, blk), 1)
    ltri = (r >= c).astype(f32)
    carry = jnp.zeros((1, LANES), f32)
    for b in range(t // blk):
        cb = jnp.dot(ltri, lf[b * blk:(b + 1) * blk], preferred_element_type=f32,
                     precision=lax.Precision.HIGHEST) + carry
        ccol_ref[b * blk:(b + 1) * blk, :] = cb
        carry = cb[blk - 1:blk, :]
    er = lax.broadcasted_iota(i32, (8, LANES), 0)
    ec = lax.broadcasted_iota(i32, (8, LANES), 1)
    eye = (er == ec).astype(f32)
    crow_ref[0] = lax.dot_general(eye, ccol_ref[...], (((1,), (1,)), ((), ())),
                                  preferred_element_type=f32, precision=lax.Precision.HIGHEST)


def _foxprep(misc, bias_row, b, t):
    return pl.pallas_call(
        functools.partial(_foxprep_kernel, t=t),
        grid=(b,),
        in_specs=[pl.BlockSpec((t, LANES), lambda i: (i, 0)),
                  pl.BlockSpec((1, LANES), lambda i: (0, 0))],
        out_specs=[pl.BlockSpec((t, H_FOX), lambda i: (i, 0)),
                   pl.BlockSpec((t, LANES), lambda i: (i, 0)),
                   pl.BlockSpec((1, 8, t), lambda i: (i, 0, 0))],
        out_shape=[jax.ShapeDtypeStruct((b * t, H_FOX), f32),
                   jax.ShapeDtypeStruct((b * t, LANES), f32),
                   jax.ShapeDtypeStruct((b, 8, t), f32)],
        compiler_params=_cp(("parallel",)),
        name="fox_prep",
    )(misc, bias_row)


def _tn(a, b):
    return lax.dot_general(a, b, (((0,), (0,)), ((), ())), preferred_element_type=f32)


def _softmax_step_t(carry, st, vb):
    m, l, acc = carry
    m_new = jnp.maximum(m, jnp.max(st, axis=0, keepdims=True))
    alpha = jnp.exp(m - m_new)
    p = jnp.exp(st - m_new)
    l = alpha * l + jnp.sum(p, axis=0, keepdims=True)
    acc = alpha * acc + _tn(vb, p.astype(bf16))
    return m_new, l, acc


def _fox_kernel(q_ref, kv_ref, ccol_ref, crow_ref, o_ref, *, tq, tk):
    qi = pl.program_id(1)
    q0 = pl.multiple_of(qi * tq, tq)
    lo = lax.broadcasted_iota(i32, (tq, LANES), 1) < HALF
    top = lax.broadcasted_iota(i32, (LANES, tq), 0) < HALF
    kk = lax.broadcasted_iota(i32, (tk, tq), 0)
    qq = lax.broadcasted_iota(i32, (tk, tq), 1)
    diag = kk <= qq
    diag2 = jnp.concatenate([diag, diag], axis=1)
    zero = jnp.zeros((), bf16)
    for p in range(H_FOX // 2):
        q = q_ref[:, p * LANES:(p + 1) * LANES]
        qst = jnp.concatenate([jnp.where(lo, q, zero), jnp.where(lo, zero, q)], axis=0)
        cq0 = crow_ref[0, 2 * p:2 * p + 1, pl.ds(q0, tq)]
        cq1 = crow_ref[0, 2 * p + 1:2 * p + 2, pl.ds(q0, tq)]

        def chunk(c, carry, mask, p=p, qst=qst, cq0=cq0, cq1=cq1):
            k0 = pl.multiple_of(c * tk, tk)
            kb = kv_ref[pl.ds(k0, tk), p * LANES:(p + 1) * LANES]
            vb = kv_ref[pl.ds(k0, tk), 384 + p * LANES:384 + (p + 1) * LANES]
            ck0 = ccol_ref[pl.ds(k0, tk), 2 * p:2 * p + 1]
            ck1 = ccol_ref[pl.ds(k0, tk), 2 * p + 1:2 * p + 2]
            st = _nt(kb, qst) + jnp.concatenate([cq0 - ck0, cq1 - ck1], axis=1)
            if mask is not None:
                st = jnp.where(mask, st, NEG)
            return _softmax_step_t(carry, st, vb)

        init = (jnp.full((1, 2 * tq), NEG, f32), jnp.zeros((1, 2 * tq), f32), jnp.zeros((LANES, 2 * tq), f32))
        carry = lax.fori_loop(0, qi, lambda c, cr: chunk(c, cr, None), init)
        _, l, acc = chunk(qi, carry, diag2)
        ot = acc / l
        o_ref[:, p * LANES:(p + 1) * LANES] = jnp.where(top, ot[:, :tq], ot[:, tq:]).T.astype(bf16)


def _fox_attn(fq, fkvb, ccol, crow, b, t):
    tq, tk = FOX_TQ, FOX_TK
    assert tq == tk
    nq = t // tq
    return pl.pallas_call(
        functools.partial(_fox_kernel, tq=tq, tk=tk),
        grid=(b, nq),
        in_specs=[pl.BlockSpec((tq, 384), lambda i, j: (i * nq + j, 0)),
                  pl.BlockSpec((t, 768), lambda i, j: (i, 0)),
                  pl.BlockSpec((t, LANES), lambda i, j: (i, 0)),
                  pl.BlockSpec((1, 8, t), lambda i, j: (i, 0, 0))],
        out_specs=pl.BlockSpec((tq, 384), lambda i, j: (i * nq + j, 0)),
        out_shape=jax.ShapeDtypeStruct((b * t, 384), bf16),
        compiler_params=_cp(("parallel", "parallel")),
        name="fox_attn",
    )(fq, fkvb, ccol, crow)


def _select_topk(score, nsel, axis=1):
    nb = score.shape[axis]
    jf = lax.broadcasted_iota(i32, score.shape, axis).astype(f32)
    sel = jnp.zeros(score.shape, f32)
    firsts = []
    for _ in range(nsel):
        mx = jnp.max(score, axis=axis, keepdims=True)
        first = jnp.min(jnp.where(score == mx, jf, float(nb)), axis=axis, keepdims=True)
        hit = jf == first
        sel = jnp.where(hit, 1.0, sel)
        score = jnp.where(hit, -3.0, score)
        firsts.append(first)
    return sel, firsts


def _nsa_kernel(q_ref, kcm_ref, nkv_ref, win_ref, misc_ref, o_ref, *, tq, tk, t):
    qi = pl.program_id(1)
    s0 = qi * tq
    nbc = t // CMP_BLOCK
    lane = lax.broadcasted_iota(i32, (tq, LANES), 1)
    lo = lane < HALF
    zero = jnp.zeros((), bf16)
    kpos = lax.broadcasted_iota(i32, (tk, tq), 0)
    qpos = lax.broadcasted_iota(i32, (tk, tq), 1) + s0
    qpos_b = lax.broadcasted_iota(i32, (nbc, tq), 1) + s0
    jb = lax.broadcasted_iota(i32, (nbc, tq), 0)
    gates = jax.nn.sigmoid(misc_ref[...].T)

    def tile3(a):
        return jnp.concatenate([a, a, a], axis=1)

    def per_head(ot):
        return [ot[:, g * tq:(g + 1) * tq] for g in range(G_NSA)]

    def attend(qk, kref, klane, vlane, c_lo, c_hi, biasfn):
        def body(c, carry, last):
            k0 = pl.multiple_of(c * tk, tk)
            kb = kref[pl.ds(k0, tk), klane:klane + LANES]
            vb = kref[pl.ds(k0, tk), vlane:vlane + LANES]
            return _softmax_step_t(carry, _nt(kb, qk) + tile3(biasfn(k0, last)), vb)
        init = (jnp.full((1, 3 * tq), NEG, f32), jnp.zeros((1, 3 * tq), f32), jnp.zeros((LANES, 3 * tq), f32))
        carry = lax.fori_loop(c_lo, c_hi - 1, lambda c, cr: body(c, cr, False), init)
        _, l, acc = body(c_hi - 1, carry, True)
        return per_head(acc / l)

    kcm = kcm_ref[...]
    kc = kcm[:, :LANES].astype(bf16)
    vc = kcm[:, LANES:].astype(bf16)
    outs = []
    for kv in range(KV_NSA):
        keep = lo if kv == 0 else jnp.logical_not(lo)
        qk = jnp.concatenate([jnp.where(keep, q_ref[:, p * LANES:(p + 1) * LANES], zero)
                              for p in range(G_NSA)], axis=0)
        cmask3 = tile3(((jb + 1) * CMP_BLOCK - 1) <= qpos_b)
        sc = jnp.where(cmask3, _nt(kc, qk), NEG)
        mx = jnp.max(sc, axis=0, keepdims=True)
        pe = jnp.where(cmask3, jnp.exp(sc - mx), 0.0)
        den = jnp.sum(pe, axis=0, keepdims=True)
        pc = pe / jnp.where(den > 0.0, den, 1.0)
        o_cmp = per_head(_tn(vc, pc.astype(bf16)))
        imp = pc[:, :tq] + pc[:, tq:2 * tq] + pc[:, 2 * tq:]
        cur = qpos_b // SEL_BLOCK
        forced = (jb == 0) | (jb == cur) | (jb == cur - 1)
        score = jnp.where(forced, FORCE, imp)
        score = jnp.where(jb * SEL_BLOCK <= qpos_b, score, -1.0)
        sel, _ = _select_topk(score, min(TOPK_BLOCKS, nbc), axis=0)
        selt = jnp.concatenate([sel, jnp.zeros((LANES - nbc, tq), f32)], axis=0).astype(bf16)

        def sel_bias(k0, last, selt=selt):
            er = (lax.broadcasted_iota(i32, (tk, LANES), 0) + k0) // SEL_BLOCK
            ec = lax.broadcasted_iota(i32, (tk, LANES), 1)
            expand = jnp.where(er == ec, 1.0, 0.0).astype(bf16)
            picked = jnp.dot(expand, selt, preferred_element_type=f32)
            bias = (picked - 1.0) * (-NEG)
            return jnp.where((kpos + k0) <= qpos, bias, NEG) if last else bias

        def win_bias(k0, last):
            rel = qpos - (kpos + k0)
            return jnp.where(rel >= 0, jnp.where(rel < WINDOW, 0.0, NEG), NEG)

        o_sel = attend(qk, nkv_ref, 2 * LANES, 3 * LANES, 0, (s0 + tq + tk - 1) // tk, sel_bias)
        w_lo = jnp.maximum(s0 - WINDOW + 1, 0) // tk
        o_win = attend(qk, win_ref, 0, LANES, w_lo, (s0 + tq + tk - 1) // tk, win_bias)
        for g in range(G_NSA):
            r = GATE0 + 3 * (kv * G_NSA + g)
            outs.append(gates[r:r + 1] * o_cmp[g] + gates[r + 1:r + 2] * o_sel[g] + gates[r + 2:r + 3] * o_win[g])
    top = lax.broadcasted_iota(i32, (LANES, tq), 0) < HALF
    for p in range(G_NSA):
        o_ref[:, p * LANES:(p + 1) * LANES] = jnp.where(top, outs[p], outs[G_NSA + p]).T.astype(bf16)


def _nsa_attn(nq, kcm, nkvb, winb, misc, b, t):
    tq, tk = NSA_TQ, NSA_TK
    assert tk % tq == 0
    nq_t = t // tq
    nbc = t // CMP_BLOCK
    return pl.pallas_call(
        functools.partial(_nsa_kernel, tq=tq, tk=tk, t=t),
        grid=(b, nq_t),
        in_specs=[pl.BlockSpec((tq, 384), lambda i, j: (i * nq_t + j, 0)),
                  pl.BlockSpec((nbc, 256), lambda i, j: (i, 0)),
                  pl.BlockSpec((t, 512), lambda i, j: (i, 0)),
                  pl.BlockSpec((t, 256), lambda i, j: (i, 0)),
                  pl.BlockSpec((tq, LANES), lambda i, j: (i * nq_t + j, 0))],
        out_specs=pl.BlockSpec((tq, 384), lambda i, j: (i * nq_t + j, 0)),
        out_shape=jax.ShapeDtypeStruct((b * t, 384), bf16),
        compiler_params=_cp(("parallel", "parallel")),
        name="nsa_attn",
    )(nq, kcm, nkvb, winb, misc)


def _pool_project(d, wp_ref, ps_ref):
    return (jnp.dot(d.astype(bf16), wp_ref[...], preferred_element_type=f32) * ps_ref[...]).astype(bf16)


def _out_tail(x, ofox, onsa, opool, wo_ref, g_ref, gate):
    o = (jnp.dot(ofox, wo_ref[0:384, :], preferred_element_type=f32)
         + jnp.dot(onsa, wo_ref[384:768, :], preferred_element_type=f32)
         + jnp.dot(opool, wo_ref[768:1024, :], preferred_element_type=f32))
    return x + gate * _rms(o, g_ref[...])


def _outproj_kernel(x_ref, ofox_ref, onsa_ref, u_ref, halo_ref, wo_ref, wp_ref, ps_ref, g_ref, gate_ref,
                    o_ref, ext_ref, *, tm, t):
    i = pl.program_id(0)
    pos0 = (i * tm) % t
    u = u_ref[...]
    ext_ref[0:16, :] = jnp.where(pos0 > 0, halo_ref[...], 0.0)
    ext_ref[16:, :] = u
    pos = lax.broadcasted_iota(i32, (tm, LANES), 0) + pos0
    lo = lax.broadcasted_iota(i32, (tm, LANES), 1) < HALF

    def shifted(k, c):
        return ext_ref[16 - k:16 - k + tm, c * LANES:(c + 1) * LANES]

    def cnt(w):
        return jnp.minimum(w, pos + 1).astype(f32)

    ds = []
    for c, (wa, wb) in enumerate(((POOL_WINDOWS[0], POOL_WINDOWS[1]), (POOL_WINDOWS[2], POOL_WINDOWS[3]))):
        run = shifted(0, c)
        sums = {}
        for k in range(1, wb):
            run = run + shifted(k, c)
            if k + 1 in (wa, wb):
                sums[k + 1] = run
        ds.append(jnp.where(lo, sums[wa] / cnt(wa), sums[wb] / cnt(wb)) - u[:, c * LANES:(c + 1) * LANES])
    opool = _pool_project(jnp.concatenate(ds, axis=1), wp_ref, ps_ref)
    o_ref[...] = _out_tail(x_ref[...], ofox_ref[...], onsa_ref[...], opool, wo_ref, g_ref, gate_ref[0])


def _outproj(x2, ofox, onsa, u, wo, wp, ps, g1, gate, *, tm, t):
    m = x2.shape[0]
    row = lambda w_: pl.BlockSpec((tm, w_), lambda i: (i, 0))
    const = lambda s: pl.BlockSpec(s, lambda i: (0, 0))
    return pl.pallas_call(
        functools.partial(_outproj_kernel, tm=tm, t=t),
        grid=(m // tm,),
        in_specs=[row(D_MODEL), row(384), row(384), row(256),
                  pl.BlockSpec((16, 256), lambda i: (jnp.maximum(i * (tm // 16) - 1, 0), 0)),
                  const((D_MODEL, D_MODEL)), const((256, 256)), const((1, 256)), const((1, D_MODEL)),
                  pl.BlockSpec((1, 1, D_MODEL), lambda i: ((i * tm) // t, 0, 0))],
        out_specs=row(D_MODEL),
        out_shape=jax.ShapeDtypeStruct((m, D_MODEL), f32),
        scratch_shapes=[pltpu.VMEM((tm + 16, 256), f32)],
        compiler_params=_cp(("parallel",)),
        name="out_proj",
    )(x2, ofox, onsa, u, u, wo, wp, ps, g1, gate)


def _outproj_dec_kernel(x_ref, ofox_ref, onsa_ref, ext_ref, wo_ref, wp_ref, ps_ref, g_ref, gate_ref, o_ref, *, past):
    ext = ext_ref[...]
    n = ext.shape[1]
    r = lax.broadcasted_iota(i32, ext.shape, 1)
    lane = lax.broadcasted_iota(i32, (ext.shape[0], C_POOL), 1)
    u_new = jnp.sum(jnp.where(r == n - 1, ext, 0.0), axis=1)
    d = jnp.zeros(u_new.shape, f32)
    for g, w in enumerate(POOL_WINDOWS):
        sw = jnp.sum(jnp.where(r >= n - w, ext, 0.0), axis=1)
        d = jnp.where(lane // POOL_GW == g, sw / float(min(w, past + 1)), d)
    opool = _pool_project(d - u_new, wp_ref, ps_ref)
    o_ref[...] = _out_tail(x_ref[...], ofox_ref[...], onsa_ref[...], opool, wo_ref, g_ref, gate_ref[...])


def _outproj_dec(x2, ofox, onsa, ext, wo, wp, ps, g1, gate, past):
    m = x2.shape[0]
    full = lambda a: pl.BlockSpec(a.shape, lambda i: (0,) * a.ndim)
    args = (x2, ofox, onsa, ext, wo, wp, ps, g1, gate)
    return pl.pallas_call(
        functools.partial(_outproj_dec_kernel, past=past),
        grid=(1,),
        in_specs=[full(a) for a in args],
        out_specs=pl.BlockSpec((m, D_MODEL), lambda i: (0, 0)),
        out_shape=jax.ShapeDtypeStruct((m, D_MODEL), f32),
        compiler_params=_cp(("arbitrary",)),
        name="out_proj_dec",
    )(*args)


def _mlp_kernel(x_ref, g2_ref, sc_ref, sh_ref, w1_ref, w2_ref, g3_ref, gate_ref, o_ref, *, tf):
    x = x_ref[...]
    h = (_rms(x, g2_ref[...]) * (1.0 + sc_ref[0]) + sh_ref[0]).astype(bf16)
    acc = jnp.zeros(x.shape, f32)
    for c in range(D_FF // tf):
        a = jnp.maximum(jnp.dot(h, w1_ref[:, c * tf:(c + 1) * tf], preferred_element_type=f32), 0.0)
        acc = acc + jnp.dot((a * a).astype(bf16), w2_ref[c * tf:(c + 1) * tf, :], preferred_element_type=f32)
    o_ref[...] = x + gate_ref[0] * _rms(acc, g3_ref[...])


def _mlp(x2, g2, sc, sh, w1, w2, g3, gate, *, tm, rows_per_mod):
    m = x2.shape[0]
    mod_rows = sc.shape[1]
    modmap = lambda i: ((i * tm) // rows_per_mod, 0, 0)
    const = lambda s: pl.BlockSpec(s, lambda i: (0, 0))
    mod = pl.BlockSpec((1, mod_rows, D_MODEL), modmap)
    return pl.pallas_call(
        functools.partial(_mlp_kernel, tf=512),
        grid=(m // tm,),
        in_specs=[pl.BlockSpec((tm, D_MODEL), lambda i: (i, 0)), const((1, D_MODEL)), mod, mod,
                  const((D_MODEL, D_FF)), const((D_FF, D_MODEL)), const((1, D_MODEL)), mod],
        out_specs=pl.BlockSpec((tm, D_MODEL), lambda i: (i, 0)),
        out_shape=jax.ShapeDtypeStruct((m, D_MODEL), f32),
        compiler_params=_cp(("parallel",)),
        name="mlp",
    )(x2, g2, sc, sh, w1, w2, g3, gate)


def _pad8(a):
    return jnp.concatenate([a, jnp.zeros((8 - a.shape[0], a.shape[1]), a.dtype)], axis=0)


def _by_group(rows):
    grp = lax.broadcasted_iota(i32, (8, rows.shape[1]), 0) // G_NSA
    return jnp.where(grp == 0, jnp.broadcast_to(rows[0:1], (8, rows.shape[1])),
                     jnp.broadcast_to(rows[1:2], (8, rows.shape[1])))


def _lane_to_sublane(row, offset, stride):
    sub = lax.broadcasted_iota(i32, (8, LANES), 0)
    lane = lax.broadcasted_iota(i32, (8, LANES), 1)
    return jnp.sum(jnp.where(lane == offset + stride * sub, jnp.broadcast_to(row, (8, LANES)), 0.0),
                   axis=1, keepdims=True)


def _fox_dec_kernel(pt_ref, fq_ref, misc_ref, bias_ref, knew_ref, *rest, pps):
    lf_refs = rest[:pps]
    kv_refs = rest[pps:2 * pps]
    o_ref, lfo_ref, m_scr, l_scr, acc_scr, c_scr = rest[2 * pps:]
    g = pl.program_id(1)
    n = pps * PAGE_SIZE
    q6f = _pad8(fq_ref[0])
    q6 = q6f.astype(bf16)
    lane = lax.broadcasted_iota(i32, (8, LANES), 1)

    @pl.when(g == 0)
    def _():
        lfrow = jnp.where(lane[0:1] < H_FOX, _log_sigmoid(misc_ref[0] + bias_ref[...]), 0.0)
        lfo_ref[0] = lfrow
        m_scr[...] = jnp.sum(q6f * _pad8(knew_ref[0, 0]), axis=1, keepdims=True)
        l_scr[...] = jnp.ones((8, 1), f32)
        acc_scr[...] = _pad8(knew_ref[0, 1])
        c_scr[...] = _lane_to_sublane(lfrow, 0, 1)

    carry = c_scr[...]
    biases = []
    n_pages = pl.num_programs(1) * pps
    for i in range(pps):
        row = pt_ref[pl.program_id(0), n_pages - 1 - (g * pps + i)] % 8
        lf = _pad8(jnp.concatenate([lf_refs[i][0, h, 0, pl.ds(row, 1), :] for h in range(H_FOX)], axis=0))
        suf = lf
        for sft in (1, 2, 4, 8, 16, 32, 64):
            suf = suf + jnp.where(lane + sft < LANES, pltpu.roll(suf, LANES - sft, 1), 0.0)
        biases.append(carry + (suf - lf))
        carry = carry + suf[:, 0:1]
    c_scr[...] = carry
    s = jnp.concatenate(biases, axis=1)
    sub = lax.broadcasted_iota(i32, (8, n), 0)
    for h in range(H_FOX):
        kht = jnp.concatenate([kv_refs[i][0, 0, 0, h] for i in range(pps)], axis=1).astype(bf16)
        s = jnp.where(sub == h, s + jnp.dot(q6, kht, preferred_element_type=f32), s)
    m = m_scr[...]
    m_new = jnp.maximum(m, jnp.max(s, axis=1, keepdims=True))
    alpha = jnp.exp(m - m_new)
    p = jnp.exp(s - m_new)
    l_scr[...] = alpha * l_scr[...] + jnp.sum(p, axis=1, keepdims=True)
    m_scr[...] = m_new
    pb = p.astype(bf16)
    sub_d = lax.broadcasted_iota(i32, (8, HEAD_DIM), 0)
    pv = jnp.zeros((8, HEAD_DIM), f32)
    for h in range(H_FOX):
        vht = jnp.concatenate([kv_refs[i][0, 0, 1, h] for i in range(pps)], axis=1).astype(bf16)
        pv = jnp.where(sub_d == h, _nt(pb, vht), pv)
    acc_scr[...] = alpha * acc_scr[...] + pv

    @pl.when(g == pl.num_programs(1) - 1)
    def _():
        o_ref[0] = acc_scr[...] / l_scr[...]


def _fox_decode(layer, page_table, fq, misc, bias_row, fkv_new, lf_t, cache_kv):
    b, n_pages = page_table.shape
    pps = FOX_PPS
    ng = n_pages // pps
    row3 = lambda w_: pl.BlockSpec((1, 1, w_), lambda i, g, pt: (i, 0, 0))

    def page_id(bi, g, pt, i):
        return pt[jnp.minimum(bi, b - 1), n_pages - 1 - (jnp.minimum(g, ng - 1) * pps + i)]

    def page(i, shape):
        zeros = (0,) * len(shape)
        return pl.BlockSpec((1, 1) + shape, lambda bi, g, pt, i=i: (layer, page_id(bi, g, pt, i)) + zeros)

    def lf_group(i):
        return pl.BlockSpec((1, H_FOX, 1, 8, PAGE_SIZE),
                            lambda bi, g, pt, i=i: (layer, 0, page_id(bi, g, pt, i) // 8, 0, 0))

    grid_spec = pltpu.PrefetchScalarGridSpec(
        num_scalar_prefetch=1,
        grid=(b, ng),
        in_specs=[pl.BlockSpec((1, H_FOX, HEAD_DIM), lambda i, g, pt: (i, 0, 0)), row3(LANES),
                  pl.BlockSpec((1, LANES), lambda i, g, pt: (0, 0)),
                  pl.BlockSpec((1, 2, H_FOX, HEAD_DIM), lambda i, g, pt: (i, 0, 0, 0))]
                 + [lf_group(i) for i in range(pps)]
                 + [page(i, (2, H_FOX, HEAD_DIM, PAGE_SIZE)) for i in range(pps)],
        out_specs=[pl.BlockSpec((1, 8, HEAD_DIM), lambda i, g, pt: (i, 0, 0)), row3(LANES)],
        scratch_shapes=[pltpu.VMEM((8, 1), f32), pltpu.VMEM((8, 1), f32), pltpu.VMEM((8, HEAD_DIM), f32),
                        pltpu.VMEM((8, 1), f32)],
    )
    return pl.pallas_call(
        functools.partial(_fox_dec_kernel, pps=pps),
        grid_spec=grid_spec,
        out_shape=[jax.ShapeDtypeStruct((b, 8, HEAD_DIM), f32), jax.ShapeDtypeStruct((b, 1, LANES), f32)],
        compiler_params=_cp(("parallel", "arbitrary")),
        name="fox_decode",
    )(page_table, fq, misc, bias_row, fkv_new, *([lf_t] * pps), *([cache_kv] * pps))


def _nsa_cmp_dec_kernel(pt_ref, q_ref, *rest, pps, past):
    pg_refs = rest[:pps]
    p_ref, o_ref, mean_scr = rest[pps:]
    g = pl.program_id(1)
    per_page = PAGE_SIZE // CMP_BLOCK
    rows = 2 * KV_NSA * HEAD_DIM
    x = jnp.concatenate([pg_refs[i][0, 0].reshape(rows, PAGE_SIZE) for i in range(pps)], axis=1)
    r = lax.broadcasted_iota(i32, (pps * PAGE_SIZE, LANES), 0)
    c = lax.broadcasted_iota(i32, (pps * PAGE_SIZE, LANES), 1)
    avg = jnp.where(r // CMP_BLOCK == c, 1.0 / CMP_BLOCK, 0.0).astype(bf16)
    hi = x.astype(bf16)
    lo = (x - hi.astype(f32)).astype(bf16)
    mean_scr[g] = jnp.dot(hi, avg, preferred_element_type=f32) + jnp.dot(lo, avg, preferred_element_type=f32)

    @pl.when(g == pl.num_programs(1) - 1)
    def _():
        ng = mean_scr.shape[0]
        nbc = ng * pps * per_page
        mean = mean_scr[0]
        for gg in range(1, ng):
            mean = mean + pltpu.roll(mean_scr[gg], gg * pps * per_page, 1)
        mean = mean[:, :nbc].astype(bf16)
        part = lambda feat, kv: mean[(feat * KV_NSA + kv) * HEAD_DIM:(feat * KV_NSA + kv + 1) * HEAD_DIM]
        q6 = _pad8(q_ref[0]).astype(bf16)
        grp0 = lax.broadcasted_iota(i32, (8, 1), 0) // G_NSA == 0
        s = jnp.where(grp0, jnp.dot(q6, part(0, 0), preferred_element_type=f32),
                      jnp.dot(q6, part(0, 1), preferred_element_type=f32))
        j = lax.broadcasted_iota(i32, (8, nbc), 1)
        mask = ((j + 1) * CMP_BLOCK - 1) <= past
        s = jnp.where(mask, s, NEG)
        pe = jnp.where(mask, jnp.exp(s - jnp.max(s, axis=1, keepdims=True)), 0.0)
        den = jnp.sum(pe, axis=1, keepdims=True)
        p = pe / jnp.where(den > 0.0, den, 1.0)
        p_ref[0] = p
        pb = p.astype(bf16)
        o_ref[0] = jnp.where(grp0, _nt(pb, part(1, 0)), _nt(pb, part(1, 1)))


def _nsa_cmp_decode(layer, page_table, nq, cache_nsa, past):
    b, n_pages = page_table.shape
    pps = CMP_PPS
    nbc = past // CMP_BLOCK
    grid_spec = pltpu.PrefetchScalarGridSpec(
        num_scalar_prefetch=1,
        grid=(b, n_pages // pps),
        in_specs=[pl.BlockSpec((1, H_NSA, HEAD_DIM), lambda i, g, pt: (i, 0, 0))]
                 + [pl.BlockSpec((1, 1, 2, KV_NSA, HEAD_DIM, PAGE_SIZE),
                                 lambda bi, g, pt, i=i: (layer, pt[jnp.minimum(bi, b - 1),
                                                                   jnp.minimum(g, n_pages // pps - 1) * pps + i],
                                                         0, 0, 0, 0))
                    for i in range(pps)],
        out_specs=[pl.BlockSpec((1, 8, nbc), lambda i, g, pt: (i, 0, 0)),
                   pl.BlockSpec((1, 8, HEAD_DIM), lambda i, g, pt: (i, 0, 0))],
        scratch_shapes=[pltpu.VMEM((n_pages // pps, 2 * KV_NSA * HEAD_DIM, LANES), f32)],
    )
    return pl.pallas_call(
        functools.partial(_nsa_cmp_dec_kernel, pps=pps, past=past),
        grid_spec=grid_spec,
        out_shape=[jax.ShapeDtypeStruct((b, 8, nbc), f32), jax.ShapeDtypeStruct((b, 8, HEAD_DIM), f32)],
        compiler_params=_cp(("parallel", "arbitrary")),
        name="nsa_cmp_decode",
    )(page_table, nq, *([cache_nsa] * pps))


def _nsa_topk_dec_kernel(p_ref, idx_ref, *, past, width):
    b = p_ref.shape[0]
    nbc = p_ref.shape[2]
    nbs = -(-(past + 1) // SEL_BLOCK)
    imps = []
    for kv in range(KV_NSA):
        imps.append(p_ref[:, kv * G_NSA, :] + p_ref[:, kv * G_NSA + 1, :] + p_ref[:, kv * G_NSA + 2, :])
    imp = jnp.concatenate(imps, axis=0)
    imp = jnp.concatenate([imp, jnp.zeros((2 * b, width - nbc), f32)], axis=1)
    j = lax.broadcasted_iota(i32, (2 * b, width), 1)
    cur = past // SEL_BLOCK
    forced = (j == 0) | (j == cur) | (j == cur - 1)
    score = jnp.where(forced, FORCE, imp)
    score = jnp.where(j * SEL_BLOCK <= past, score, -1.0)
    score = jnp.where(j < nbs, score, -2.0)
    _, firsts = _select_topk(score, min(TOPK_BLOCKS, nbs))
    lane = lax.broadcasted_iota(i32, (2 * b, LANES), 1)
    out = jnp.zeros((2 * b, LANES), i32)
    for k, first in enumerate(firsts):
        out = jnp.where(lane == k, first.astype(i32), out)
    idx_ref[...] = out


def _nsa_topk_decode(p8, past):
    b, _, nbc = p8.shape
    width = 2 * nbc
    return pl.pallas_call(
        functools.partial(_nsa_topk_dec_kernel, past=past, width=width),
        grid=(1,),
        in_specs=[pl.BlockSpec(p8.shape, lambda i: (0, 0, 0))],
        out_specs=pl.BlockSpec((2 * b, LANES), lambda i: (0, 0)),
        out_shape=jax.ShapeDtypeStruct((2 * b, LANES), i32),
        compiler_params=_cp(("arbitrary",)),
        name="nsa_topk_decode",
    )(p8)


def _nsa_sel_dec_kernel(pt_ref, idx_ref, q_ref, misc_ref, ocmp_ref, nkv_ref, wnew_ref, win_ref, *rest, past, nsel):
    blk_refs = rest[:KV_NSA * nsel]
    o_ref = rest[KV_NSA * nsel]
    b = pl.program_id(0)
    n_past_blk = past // SEL_BLOCK
    q8f = _pad8(q_ref[0])
    q8 = q8f.astype(bf16)
    sub = lax.broadcasted_iota(i32, (8, 1), 0)
    grp0 = sub // G_NSA == 0
    per_page = PAGE_SIZE // SEL_BLOCK
    lane_blk = lax.broadcasted_iota(i32, (8, PAGE_SIZE), 1) // SEL_BLOCK
    m = jnp.full((8, 1), NEG, f32)
    l = jnp.zeros((8, 1), f32)
    acc = jnp.zeros((8, HEAD_DIM), f32)
    s_new = jnp.sum(q8f * _by_group(nkv_ref[0, 2]), axis=1, keepdims=True)
    v_new = _by_group(nkv_ref[0, 3])
    for kv in range(KV_NSA):
        in_group = (sub // G_NSA) == kv
        has_new = jnp.zeros((), jnp.bool_)
        for k in range(nsel):
            j = idx_ref[kv * pl.num_programs(0) + b, k]
            has_new = has_new | (j == n_past_blk)
            blk = blk_refs[kv * nsel + k]
            mask = in_group & (j < n_past_blk) & (lane_blk == j % per_page)
            s = jnp.where(mask, jnp.dot(q8, blk[0, 0, 0, 0].astype(bf16), preferred_element_type=f32), NEG)
            m_new = jnp.maximum(m, jnp.max(s, axis=1, keepdims=True))
            alpha = jnp.exp(m - m_new)
            pe = jnp.where(mask, jnp.exp(s - m_new), 0.0)
            l = alpha * l + jnp.sum(pe, axis=1, keepdims=True)
            acc = alpha * acc + _nt(pe.astype(bf16), blk[0, 0, 1, 0].astype(bf16))
            m = m_new
        new_ok = in_group & has_new
        m_new = jnp.maximum(m, jnp.where(new_ok, s_new, NEG))
        alpha = jnp.exp(m - m_new)
        p_new = jnp.where(new_ok, jnp.exp(s_new - m_new), 0.0)
        l = alpha * l + p_new
        acc = alpha * acc + p_new * v_new
        m = m_new
    o_sel = acc / jnp.where(l > 0.0, l, 1.0)
    wb = win_ref.shape[5]
    kpos = past - wb + lax.broadcasted_iota(i32, (8, wb), 1)
    rel = past - kpos
    wmask = (rel >= 0) & (rel < WINDOW) & (kpos >= 0)
    s = jnp.where(grp0, jnp.dot(q8, win_ref[0, 0, 0, 0].astype(bf16), preferred_element_type=f32),
                  jnp.dot(q8, win_ref[0, 0, 0, 1].astype(bf16), preferred_element_type=f32))
    s = jnp.where(wmask, s, NEG)
    s_w = jnp.sum(q8f * _by_group(wnew_ref[0, 0]), axis=1, keepdims=True)
    m = jnp.maximum(jnp.max(s, axis=1, keepdims=True), s_w)
    p = jnp.where(wmask, jnp.exp(s - m), 0.0)
    p_w = jnp.exp(s_w - m)
    pb = p.astype(bf16)
    pv = jnp.where(grp0, _nt(pb, win_ref[0, 0, 1, 0].astype(bf16)), _nt(pb, win_ref[0, 0, 1, 1].astype(bf16)))
    o_win = (pv + p_w * _by_group(wnew_ref[0, 1])) / (jnp.sum(p, axis=1, keepdims=True) + p_w)
    misc = misc_ref[0]
    gates = [jax.nn.sigmoid(_lane_to_sublane(misc, GATE0 + r, 3)) for r in range(3)]
    o_ref[0] = gates[0] * ocmp_ref[0] + gates[1] * o_sel + gates[2] * o_win


def _nsa_sel_decode(layer, page_table, idx, nq, misc, ocmp, nkv_new, win_new, cache_win, cache_nsa, past):
    b = page_table.shape[0]
    nsel = idx.shape[1]
    n_past_blk = past // SEL_BLOCK
    per_page = PAGE_SIZE // SEL_BLOCK
    wb = cache_win.shape[5]

    def blk_spec(kv, k):
        def imap(i, pt, ix):
            ii = jnp.minimum(i, b - 1)
            jp = jnp.clip(ix[kv * b + ii, k], 0, n_past_blk - 1)
            return (layer, pt[ii, jp // per_page], 1, kv, 0, 0)
        return pl.BlockSpec((1, 1, 2, 1, HEAD_DIM, PAGE_SIZE), imap)

    grid_spec = pltpu.PrefetchScalarGridSpec(
        num_scalar_prefetch=2,
        grid=(b,),
        in_specs=[pl.BlockSpec((1, H_NSA, HEAD_DIM), lambda i, pt, ix: (i, 0, 0)),
                  pl.BlockSpec((1, 1, LANES), lambda i, pt, ix: (i, 0, 0)),
                  pl.BlockSpec((1, 8, HEAD_DIM), lambda i, pt, ix: (i, 0, 0)),
                  pl.BlockSpec((1, 4, KV_NSA, HEAD_DIM), lambda i, pt, ix: (i, 0, 0, 0)),
                  pl.BlockSpec((1, 2, KV_NSA, HEAD_DIM), lambda i, pt, ix: (i, 0, 0, 0)),
                  pl.BlockSpec((1, 1, 2, KV_NSA, HEAD_DIM, wb), lambda i, pt, ix: (layer, i, 0, 0, 0, 0))]
                 + [blk_spec(kv, k) for kv in range(KV_NSA) for k in range(nsel)],
        out_specs=pl.BlockSpec((1, 8, HEAD_DIM), lambda i, pt, ix: (i, 0, 0)),
    )
    return pl.pallas_call(
        functools.partial(_nsa_sel_dec_kernel, past=past, nsel=nsel),
        grid_spec=grid_spec,
        out_shape=jax.ShapeDtypeStruct((b, 8, HEAD_DIM), f32),
        compiler_params=_cp(("parallel",)),
        name="nsa_sel_decode",
    )(page_table, idx, nq, misc, ocmp, nkv_new, win_new, cache_win, *([cache_nsa] * (KV_NSA * nsel)))


def _proj_columns():
    off = np.cumsum([0, 384, 384, 384, 6, 384, 128, 128, 128, 128, 128, 128, 18, 256])
    fq, fk, fv, ff, nq, nkc, nvc, nks, nvs, nkw, nvw, ng, u = [int(o) for o in off[:13]]
    cols = list(range(fq, fq + 384))
    for h in NSA_PAIR_ORDER:
        cols += list(range(nq + h * HEAD_DIM, nq + (h + 1) * HEAD_DIM))
    cols += list(range(fk, fk + 768))
    cols += list(range(nkc, nkc + 512))
    cols += list(range(nkw, nkw + 256))
    cols += list(range(u, u + 256))
    cols += list(range(ff, ff + 6)) + list(range(ng, ng + 18)) + [N_IN] * (LANES - 24)
    return np.asarray(cols, np.int32)


def _out_rows():
    rows = list(range(0, 384))
    for h in NSA_PAIR_ORDER:
        rows += list(range(384 + h * HEAD_DIM, 384 + (h + 1) * HEAD_DIM))
    rows += list(range(768, 1024))
    return np.asarray(rows, np.int32)


def _rope_tables(pos):
    half = ROT_DIM // 2
    inv = ROPE_THETA ** (-jnp.arange(0, ROT_DIM, 2, dtype=f32) / ROT_DIM)
    ang = pos.astype(f32)[:, None] * inv[None, :]
    cos, sin = jnp.cos(ang), jnp.sin(ang)
    n = pos.shape[0]
    one = jnp.ones((n, HEAD_DIM - ROT_DIM), f32)
    zero8 = jnp.zeros((n, half), f32)
    zrest = jnp.zeros((n, HEAD_DIM - ROT_DIM), f32)
    c = jnp.concatenate([cos, cos, one], axis=1)
    sa = jnp.concatenate([-sin, zero8, zrest], axis=1)
    sb = jnp.concatenate([zero8, sin, zrest], axis=1)
    return tuple(jnp.concatenate([a, a], axis=1) for a in (c, sa, sb))


def kernel(x_prompt, x_sample, cache_fox_kv, cache_fox_logf, cache_nsa_kv, cache_nsa_win, state_pool, page_table,
           c_prompt, c_sample, w_ada, b_ada, norm_g, w_in, b_fox_f, w_out, w_pool, pool_scale, w_ff1, w_ff2):
    depth = w_in.shape[0]
    bp, t, _ = x_prompt.shape
    bs = x_sample.shape[0]
    past = page_table.shape[1] * PAGE_SIZE
    mp = bp * t

    cols = _proj_columns()
    w_in_p = jnp.concatenate([w_in, jnp.zeros((depth, D_MODEL, 1), f32)], axis=2)[:, :, cols].astype(bf16)
    w_out_p = w_out[:, _out_rows(), :].astype(bf16)
    w_pool_bd = jnp.zeros((depth, C_POOL, C_POOL), f32)
    for g in range(len(POOL_WINDOWS)):
        w_pool_bd = w_pool_bd.at[:, g * POOL_GW:(g + 1) * POOL_GW, g * POOL_GW:(g + 1) * POOL_GW].set(w_pool[:, g])
    w_pool_bd = w_pool_bd.astype(bf16)
    w1 = w_ff1.astype(bf16)
    w2 = w_ff2.astype(bf16)
    bias_rows = jnp.pad(b_fox_f, ((0, 0), (0, LANES - H_FOX))).reshape(depth, 1, LANES)

    rope_p = _rope_tables(jnp.arange(t))
    rope_s = _rope_tables(jnp.full((bs,), past, i32))

    to_last = (0, 1, 3, 4, 5, 2)
    fox_t = jnp.transpose(cache_fox_kv, to_last)
    nsa_t = jnp.transpose(cache_nsa_kv, to_last)
    win_t = jnp.transpose(cache_nsa_win, to_last)
    n_pool = cache_fox_logf.shape[1]
    assert n_pool % 8 == 0
    lf_t = jnp.transpose(cache_fox_logf, (0, 3, 1, 2)).reshape(depth, H_FOX, n_pool // 8, 8, PAGE_SIZE)
    nat_from_pair = np.argsort(np.asarray(NSA_PAIR_ORDER))

    mod = _ada(jnp.concatenate([c_prompt, c_sample], axis=0), w_ada, b_ada)
    mod = mod.reshape(depth, bp + bs, 6, D_MODEL)

    yp = x_prompt.reshape(mp, D_MODEL)
    ys = x_sample.reshape(bs, D_MODEL)
    sp, ss = [], []
    kv_stacks = None
    for l in range(depth):
        g = norm_g[l].reshape(4, 1, D_MODEL)
        modp = [mod[l, :bp, k].reshape(bp, 1, D_MODEL) for k in range(6)]
        mods = [mod[l, bp:, k].reshape(1, bs, D_MODEL) for k in range(6)]

        (fq, nq, fkv, nkv, win, u, misc, fkvb, nkvb, winb, kcm) = _inproj(
            yp, g[0], modp[1], modp[0], w_in_p[l], *rope_p, tm=512, rows_per_mod=t, rope_rows=t, with_means=True,
            layer=l, depth=depth, stacked=kv_stacks)
        kv_stacks = (fkv, nkv)
        logf, ccol, crow = _foxprep(misc, bias_rows[l], bp, t)
        o_fox = _fox_attn(fq, fkvb, ccol, crow, bp, t)
        o_nsa = _nsa_attn(nq, kcm, nkvb, winb, misc, bp, t)
        y1 = _outproj(yp, o_fox, o_nsa, u, w_out_p[l], w_pool_bd[l], pool_scale[l].reshape(1, C_POOL), g[1],
                      modp[2], tm=512, t=t)
        yp = _mlp(y1, g[2], modp[4], modp[3], w1[l], w2[l], g[3], modp[5], tm=512, rows_per_mod=t)
        wl = min(WINDOW, t)
        sp.append((logf.reshape(bp, t, H_FOX),
                   win.reshape(bp, t, 2, KV_NSA, HEAD_DIM)[:, t - wl:],
                   u.reshape(bp, t, C_POOL)[:, t - POOL_STATE:]))

        (fq_s, nq_s, fkv_s, nkv_s, win_s, u_s, misc_s, _, _, _, _) = _inproj(
            ys, g[0], mods[1], mods[0], w_in_p[l], *rope_s, tm=bs, rows_per_mod=bs, rope_rows=bs, with_means=False)
        r3 = lambda a: a.reshape(bs, 1, a.shape[-1])
        fq6 = fq_s.astype(f32).reshape(bs, H_FOX, HEAD_DIM)
        nq6 = nq_s.astype(f32).reshape(bs, H_NSA, HEAD_DIM)[:, nat_from_pair]
        nkv4 = nkv_s.reshape(bs, 4, KV_NSA, HEAD_DIM)
        win2 = win_s.reshape(bs, 2, KV_NSA, HEAD_DIM)
        o_fox_s, logf_s = _fox_decode(l, page_table, fq6, r3(misc_s), bias_rows[l],
                                      fkv_s.reshape(bs, 2, H_FOX, HEAD_DIM), lf_t, fox_t)
        p8, o_cmp = _nsa_cmp_decode(l, page_table, nq6, nsa_t, past)
        idx = _nsa_topk_decode(p8, past)[:, :min(TOPK_BLOCKS, -(-(past + 1) // SEL_BLOCK))]
        o_nsa_s = _nsa_sel_decode(l, page_table, idx, nq6, r3(misc_s), o_cmp, nkv4, win2,
                                  win_t, nsa_t, past)
        o_fox_s = o_fox_s[:, :H_FOX].reshape(bs, 384).astype(bf16)
        o_nsa_s = o_nsa_s[:, np.asarray(NSA_PAIR_ORDER)].reshape(bs, 384).astype(bf16)
        u_ext = jnp.concatenate([state_pool[l], u_s.reshape(bs, 1, C_POOL)], axis=1)
        y1s = _outproj_dec(ys, o_fox_s, o_nsa_s, u_ext, w_out_p[l],
                           w_pool_bd[l], pool_scale[l].reshape(1, C_POOL), g[1], mods[2][0], past)
        ys = _mlp(y1s, g[2], mods[4], mods[3], w1[l], w2[l], g[3], mods[5], tm=bs, rows_per_mod=bs)
        win_all = jnp.concatenate([cache_nsa_win[l], win2[:, None]], axis=1)
        ss.append((fkv_s.reshape(bs, 1, 2, H_FOX, HEAD_DIM), logf_s[:, :, :H_FOX], nkv4[:, None],
                   win_all[:, 1:], u_ext[:, 1:]))

    stk = lambda lst, i: jnp.stack([s[i] for s in lst], axis=0)
    tokens_first = (0, 1, 5, 2, 3, 4)
    fkv_all = jnp.transpose(kv_stacks[0].reshape(depth, bp, 2, H_FOX, HEAD_DIM, t), tokens_first)
    nkv_all = jnp.transpose(kv_stacks[1].reshape(depth, bp, 4, KV_NSA, HEAD_DIM, t), tokens_first)
    return (yp.reshape(bp, t, D_MODEL), ys.reshape(bs, 1, D_MODEL),
            fkv_all, stk(sp, 0), nkv_all, stk(sp, 1), stk(sp, 2),
            stk(ss, 0), stk(ss, 1), stk(ss, 2), stk(ss, 3), stk(ss, 4))
```

```python
import functools

import numpy as np
import jax
import jax.numpy as jnp
from jax import lax
from jax.experimental import pallas as pl
from jax.experimental.pallas import tpu as pltpu

f32 = jnp.float32
bf16 = jnp.bfloat16
i32 = jnp.int32

D_MODEL = 1024
HEAD_DIM = 64
H_FOX = 6
H_NSA = 6
KV_NSA = 2
G_NSA = H_NSA // KV_NSA
POOL_WINDOWS = (2, 4, 8, 16)
C_POOL = 256
POOL_GW = 64
POOL_STATE = 15
ROT_DIM = 16
ROPE_THETA = 500000.0
CMP_BLOCK = 64
SEL_BLOCK = 64
TOPK_BLOCKS = 8
WINDOW = 512
PAGE_SIZE = 128
D_FF = 4 * D_MODEL
EPS = 1e-6
NEG = -1e30
FORCE = 1e4
SCALE = HEAD_DIM ** -0.5
N_IN = 2584

LANES = 128
HALF = LANES // 2
VMEM_LIMIT = 56 * 1024 * 1024

C_FQ = 0
C_NQ = 384
C_FKV = 768
C_NKV = 1536
C_WIN = 2048
C_U = 2304
C_MISC = 2560
N_PROJ = 2688
GATE0 = H_FOX
NSA_PAIR_ORDER = (0, 3, 1, 4, 2, 5)

FOX_TQ = 512
FOX_TK = 512
NSA_TQ = 512
NSA_TK = 512
FOX_PPS = 16
CMP_PPS = 16


def _cp(sem):
    return pltpu.CompilerParams(dimension_semantics=sem, vmem_limit_bytes=VMEM_LIMIT)


def _nt(a, b):
    return lax.dot_general(a, b, (((1,), (1,)), ((), ())), preferred_element_type=f32)


def _rms(x, g):
    return x * lax.rsqrt(jnp.mean(x * x, axis=-1, keepdims=True) + EPS) * g


def _log_sigmoid(x):
    return jnp.minimum(x, 0.0) - jnp.log1p(jnp.exp(-jnp.abs(x)))


def _ada_kernel(c_ref, w_ref, b_ref, o_ref):
    c = c_ref[...]
    a = (c * jax.nn.sigmoid(c)).astype(bf16)
    o_ref[0] = jnp.dot(a, w_ref[0].astype(bf16), preferred_element_type=f32) + b_ref[0]


def _ada(c_all, w_ada, b_ada):
    depth = w_ada.shape[0]
    n = c_all.shape[0]
    tn = 1024
    return pl.pallas_call(
        _ada_kernel,
        grid=(depth, 6 * D_MODEL // tn),
        in_specs=[pl.BlockSpec((n, D_MODEL), lambda l, j: (0, 0)),
                  pl.BlockSpec((1, D_MODEL, tn), lambda l, j: (l, 0, j)),
                  pl.BlockSpec((1, 1, tn), lambda l, j: (l, 0, j))],
        out_specs=pl.BlockSpec((1, n, tn), lambda l, j: (l, 0, j)),
        out_shape=jax.ShapeDtypeStruct((depth, n, 6 * D_MODEL), f32),
        compiler_params=_cp(("parallel", "parallel")),
        name="ada_mod",
    )(c_all, w_ada, b_ada.reshape(depth, 1, 6 * D_MODEL))


def _inproj_kernel(x_ref, g_ref, sc_ref, sh_ref, w_ref, cos_ref, sa_ref, sb_ref, *rest, tm, with_means, state_t):
    (fq_ref, nq_ref, fkv_ref, nkv_ref, win_ref, u_ref, misc_ref, fkvb_ref, nkvb_ref, winb_ref, kcm_ref) = rest[-11:]
    x = x_ref[...]
    h = (_rms(x, g_ref[...]) * (1.0 + sc_ref[0]) + sh_ref[0]).astype(bf16)
    cos = cos_ref[...]
    sa = sa_ref[...]
    sb = sb_ref[...]

    def rope(z):
        return z * cos + pltpu.roll(z, LANES - ROT_DIM // 2, 1) * sa + pltpu.roll(z, ROT_DIM // 2, 1) * sb

    def means(z):
        return jnp.sum(z.reshape(tm // CMP_BLOCK, CMP_BLOCK, LANES), axis=1) * (1.0 / CMP_BLOCK)

    def state(ref, col, z):
        if state_t:
            ref[0, 0, col:col + LANES, :] = z.T
        else:
            ref[:, col:col + LANES] = z

    def emit(blk, z):
        c = blk * LANES
        if c < C_NQ:
            fq_ref[:, c - C_FQ:c - C_FQ + LANES] = (z * SCALE).astype(bf16)
        elif c < C_FKV:
            nq_ref[:, c - C_NQ:c - C_NQ + LANES] = (rope(z) * SCALE).astype(bf16)
        elif c < C_NKV:
            state(fkv_ref, c - C_FKV, z)
            fkvb_ref[:, c - C_FKV:c - C_FKV + LANES] = z.astype(bf16)
        elif c < C_WIN:
            o = c - C_NKV
            if o in (0, 2 * LANES):
                z = rope(z)
            state(nkv_ref, o, z)
            nkvb_ref[:, o:o + LANES] = z.astype(bf16)
            if with_means and o < 2 * LANES:
                kcm_ref[:, o:o + LANES] = means(z)
        elif c < C_U:
            o = c - C_WIN
            if o == 0:
                z = rope(z)
            win_ref[:, o:o + LANES] = z
            winb_ref[:, o:o + LANES] = z.astype(bf16)
        elif c < C_MISC:
            u_ref[:, c - C_U:c - C_U + LANES] = z
        else:
            misc_ref[...] = z

    nblk = N_PROJ // LANES
    for b0 in range(0, nblk, 2):
        nb = min(2, nblk - b0)
        z2 = jnp.dot(h, w_ref[:, b0 * LANES:(b0 + nb) * LANES], preferred_element_type=f32)
        for k in range(nb):
            emit(b0 + k, z2[:, k * LANES:(k + 1) * LANES])
    if not with_means:
        kcm_ref[...] = jnp.zeros(kcm_ref.shape, f32)


def _inproj(x2, g0, sc, sh, w, cos, sa, sb, *, tm, rows_per_mod, rope_rows, with_means, layer=0, depth=None,
            stacked=None):
    m = x2.shape[0]
    nt = m // tm
    mod_rows = sc.shape[1]
    rope_blocks = rope_rows // tm
    kc_rows = max(tm // CMP_BLOCK, 8)
    modmap = lambda i: ((i * tm) // rows_per_mod, 0, 0)
    ropemap = lambda i: (i % rope_blocks, 0)
    row = lambda w_: pl.BlockSpec((tm, w_), lambda i: (i, 0))
    state_t = depth is not None
    if state_t:
        seqs = m // rope_rows
        state_shape = lambda w_: (depth, seqs, w_, rope_rows)
        state_spec = lambda w_: pl.BlockSpec((1, 1, w_, tm), lambda i: (layer, i // rope_blocks, 0, i % rope_blocks))
    else:
        state_shape = lambda w_: (m, w_)
        state_spec = row
    outs = [((m, 384), bf16), ((m, 384), bf16), (state_shape(768), f32), (state_shape(512), f32), ((m, 256), f32),
            ((m, 256), f32), ((m, LANES), f32), ((m, 768), bf16), ((m, 512), bf16), ((m, 256), bf16)]
    out_shape = [jax.ShapeDtypeStruct(s, d) for s, d in outs] + [jax.ShapeDtypeStruct((nt * kc_rows, 256), f32)]
    out_specs = ([row(s[-1]) for s, _ in outs[:2]] + [state_spec(768), state_spec(512)]
                 + [row(s[-1]) for s, _ in outs[4:]] + [pl.BlockSpec((kc_rows, 256), lambda i: (i, 0))])
    in_specs = [row(D_MODEL),
                pl.BlockSpec((1, D_MODEL), lambda i: (0, 0)),
                pl.BlockSpec((1, mod_rows, D_MODEL), modmap),
                pl.BlockSpec((1, mod_rows, D_MODEL), modmap),
                pl.BlockSpec((D_MODEL, N_PROJ), lambda i: (0, 0)),
                pl.BlockSpec((tm, LANES), ropemap),
                pl.BlockSpec((tm, LANES), ropemap),
                pl.BlockSpec((tm, LANES), ropemap)]
    args = [x2, g0, sc, sh, w, cos, sa, sb]
    aliases = {}
    if stacked is not None:
        aliases = {len(args): 2, len(args) + 1: 3}
        in_specs += [pl.BlockSpec(memory_space=pl.ANY)] * 2
        args += list(stacked)
    return pl.pallas_call(
        functools.partial(_inproj_kernel, tm=tm, with_means=with_means, state_t=state_t),
        grid=(nt,),
        in_specs=in_specs,
        out_specs=out_specs,
        out_shape=out_shape,
        input_output_aliases=aliases,
        compiler_params=_cp(("parallel",)),
        name="in_proj",
    )(*args)


def _foxprep_kernel(misc_ref, bias_ref, lf_ref, ccol_ref, crow_ref, *, t):
    blk = 256
    lane = lax.broadcasted_iota(i32, (t, LANES), 1)
    lf = jnp.where(lane < H_FOX, _log_sigmoid(misc_ref[...] + bias_ref[...]), 0.0)
    lf_ref[...] = lf[:, :H_FOX]
    r = lax.broadcasted_iota(i32, (blk, blk), 0)
    c = lax.broadcasted_iota(i32, (blk, blk), 1)
    ltri = (r >= c).astype(f32)
    carry = jnp.zeros((1, LANES), f32)
    for b in range(t // blk):
        cb = jnp.dot(ltri, lf[b * blk:(b + 1) * blk], preferred_element_type=f32,
                     precision=lax.Precision.HIGHEST) + carry
        ccol_ref[b * blk:(b + 1) * blk, :] = cb
        carry = cb[blk - 1:blk, :]
    er = lax.broadcasted_iota(i32, (8, LANES), 0)
    ec = lax.broadcasted_iota(i32, (8, LANES), 1)
    eye = (er == ec).astype(f32)
    crow_ref[0] = lax.dot_general(eye, ccol_ref[...], (((1,), (1,)), ((), ())),
                                  preferred_element_type=f32, precision=lax.Precision.HIGHEST)


def _foxprep(misc, bias_row, b, t):
    return pl.pallas_call(
        functools.partial(_foxprep_kernel, t=t),
        grid=(b,),
        in_specs=[pl.BlockSpec((t, LANES), lambda i: (i, 0)),
                  pl.BlockSpec((1, LANES), lambda i: (0, 0))],
        out_specs=[pl.BlockSpec((t, H_FOX), lambda i: (i, 0)),
                   pl.BlockSpec((t, LANES), lambda i: (i, 0)),
                   pl.BlockSpec((1, 8, t), lambda i: (i, 0, 0))],
        out_shape=[jax.ShapeDtypeStruct((b * t, H_FOX), f32),
                   jax.ShapeDtypeStruct((b * t, LANES), f32),
                   jax.ShapeDtypeStruct((b, 8, t), f32)],
        compiler_params=_cp(("parallel",)),
        name="fox_prep",
    )(misc, bias_row)


def _tn(a, b):
    return lax.dot_general(a, b, (((0,), (0,)), ((), ())), preferred_element_type=f32)


def _softmax_step_t(carry, st, vb):
    m, l, acc = carry
    m_new = jnp.maximum(m, jnp.max(st, axis=0, keepdims=True))
    alpha = jnp.exp(m - m_new)
    p = jnp.exp(st - m_new)
    l = alpha * l + jnp.sum(p, axis=0, keepdims=True)
    acc = alpha * acc + _tn(vb, p.astype(bf16))
    return m_new, l, acc


def _fox_kernel(q_ref, kv_ref, ccol_ref, crow_ref, o_ref, *, tq, tk):
    qi = pl.program_id(1)
    q0 = pl.multiple_of(qi * tq, tq)
    lo = lax.broadcasted_iota(i32, (tq, LANES), 1) < HALF
    top = lax.broadcasted_iota(i32, (LANES, tq), 0) < HALF
    kk = lax.broadcasted_iota(i32, (tk, tq), 0)
    qq = lax.broadcasted_iota(i32, (tk, tq), 1)
    diag = kk <= qq
    diag2 = jnp.concatenate([diag, diag], axis=1)
    zero = jnp.zeros((), bf16)
    for p in range(H_FOX // 2):
        q = q_ref[:, p * LANES:(p + 1) * LANES]
        qst = jnp.concatenate([jnp.where(lo, q, zero), jnp.where(lo, zero, q)], axis=0)
        cq0 = crow_ref[0, 2 * p:2 * p + 1, pl.ds(q0, tq)]
        cq1 = crow_ref[0, 2 * p + 1:2 * p + 2, pl.ds(q0, tq)]

        def chunk(c, carry, mask, p=p, qst=qst, cq0=cq0, cq1=cq1):
            k0 = pl.multiple_of(c * tk, tk)
            kb = kv_ref[pl.ds(k0, tk), p * LANES:(p + 1) * LANES]
            vb = kv_ref[pl.ds(k0, tk), 384 + p * LANES:384 + (p + 1) * LANES]
            ck0 = ccol_ref[pl.ds(k0, tk), 2 * p:2 * p + 1]
            ck1 = ccol_ref[pl.ds(k0, tk), 2 * p + 1:2 * p + 2]
            st = _nt(kb, qst) + jnp.concatenate([cq0 - ck0, cq1 - ck1], axis=1)
            if mask is not None:
                st = jnp.where(mask, st, NEG)
            return _softmax_step_t(carry, st, vb)

        init = (jnp.full((1, 2 * tq), NEG, f32), jnp.zeros((1, 2 * tq), f32), jnp.zeros((LANES, 2 * tq), f32))
        carry = lax.fori_loop(0, qi, lambda c, cr: chunk(c, cr, None), init)
        _, l, acc = chunk(qi, carry, diag2)
        ot = acc / l
        o_ref[:, p * LANES:(p + 1) * LANES] = jnp.where(top, ot[:, :tq], ot[:, tq:]).T.astype(bf16)


def _fox_attn(fq, fkvb, ccol, crow, b, t):
    tq, tk = FOX_TQ, FOX_TK
    assert tq == tk
    nq = t // tq
    return pl.pallas_call(
        functools.partial(_fox_kernel, tq=tq, tk=tk),
        grid=(b, nq),
        in_specs=[pl.BlockSpec((tq, 384), lambda i, j: (i * nq + j, 0)),
                  pl.BlockSpec((t, 768), lambda i, j: (i, 0)),
                  pl.BlockSpec((t, LANES), lambda i, j: (i, 0)),
                  pl.BlockSpec((1, 8, t), lambda i, j: (i, 0, 0))],
        out_specs=pl.BlockSpec((tq, 384), lambda i, j: (i * nq + j, 0)),
        out_shape=jax.ShapeDtypeStruct((b * t, 384), bf16),
        compiler_params=_cp(("parallel", "parallel")),
        name="fox_attn",
    )(fq, fkvb, ccol, crow)


def _select_topk(score, nsel, axis=1):
    nb = score.shape[axis]
    jf = lax.broadcasted_iota(i32, score.shape, axis).astype(f32)
    sel = jnp.zeros(score.shape, f32)
    firsts = []
    for _ in range(nsel):
        mx = jnp.max(score, axis=axis, keepdims=True)
        first = jnp.min(jnp.where(score == mx, jf, float(nb)), axis=axis, keepdims=True)
        hit = jf == first
        sel = jnp.where(hit, 1.0, sel)
        score = jnp.where(hit, -3.0, score)
        firsts.append(first)
    return sel, firsts


def _nsa_kernel(q_ref, kcm_ref, nkv_ref, win_ref, misc_ref, o_ref, *, tq, tk, t):
    qi = pl.program_id(1)
    s0 = qi * tq
    nbc = t // CMP_BLOCK
    lane = lax.broadcasted_iota(i32, (tq, LANES), 1)
    lo = lane < HALF
    zero = jnp.zeros((), bf16)
    kpos = lax.broadcasted_iota(i32, (tk, tq), 0)
    qpos = lax.broadcasted_iota(i32, (tk, tq), 1) + s0
    qpos_b = lax.broadcasted_iota(i32, (nbc, tq), 1) + s0
    jb = lax.broadcasted_iota(i32, (nbc, tq), 0)
    gates = jax.nn.sigmoid(misc_ref[...].T)

    def tile3(a):
        return jnp.concatenate([a, a, a], axis=1)

    def per_head(ot):
        return [ot[:, g * tq:(g + 1) * tq] for g in range(G_NSA)]

    def attend(qk, kref, klane, vlane, c_lo, c_hi, biasfn):
        def body(c, carry, last):
            k0 = pl.multiple_of(c * tk, tk)
            kb = kref[pl.ds(k0, tk), klane:klane + LANES]
            vb = kref[pl.ds(k0, tk), vlane:vlane + LANES]
            return _softmax_step_t(carry, _nt(kb, qk) + tile3(biasfn(k0, last)), vb)
        init = (jnp.full((1, 3 * tq), NEG, f32), jnp.zeros((1, 3 * tq), f32), jnp.zeros((LANES, 3 * tq), f32))
        carry = lax.fori_loop(c_lo, c_hi - 1, lambda c, cr: body(c, cr, False), init)
        _, l, acc = body(c_hi - 1, carry, True)
        return per_head(acc / l)

    kcm = kcm_ref[...]
    kc = kcm[:, :LANES].astype(bf16)
    vc = kcm[:, LANES:].astype(bf16)
    outs = []
    for kv in range(KV_NSA):
        keep = lo if kv == 0 else jnp.logical_not(lo)
        qk = jnp.concatenate([jnp.where(keep, q_ref[:, p * LANES:(p + 1) * LANES], zero)
                              for p in range(G_NSA)], axis=0)
        cmask3 = tile3(((jb + 1) * CMP_BLOCK - 1) <= qpos_b)
        sc = jnp.where(cmask3, _nt(kc, qk), NEG)
        mx = jnp.max(sc, axis=0, keepdims=True)
        pe = jnp.where(cmask3, jnp.exp(sc - mx), 0.0)
        den = jnp.sum(pe, axis=0, keepdims=True)
        pc = pe / jnp.where(den > 0.0, den, 1.0)
        o_cmp = per_head(_tn(vc, pc.astype(bf16)))
        imp = pc[:, :tq] + pc[:, tq:2 * tq] + pc[:, 2 * tq:]
        cur = qpos_b // SEL_BLOCK
        forced = (jb == 0) | (jb == cur) | (jb == cur - 1)
        score = jnp.where(forced, FORCE, imp)
        score = jnp.where(jb * SEL_BLOCK <= qpos_b, score, -1.0)
        sel, _ = _select_topk(score, min(TOPK_BLOCKS, nbc), axis=0)
        selt = jnp.concatenate([sel, jnp.zeros((LANES - nbc, tq), f32)], axis=0).astype(bf16)

        def sel_bias(k0, last, selt=selt):
            er = (lax.broadcasted_iota(i32, (tk, LANES), 0) + k0) // SEL_BLOCK
            ec = lax.broadcasted_iota(i32, (tk, LANES), 1)
            expand = jnp.where(er == ec, 1.0, 0.0).astype(bf16)
            picked = jnp.dot(expand, selt, preferred_element_type=f32)
            bias = (picked - 1.0) * (-NEG)
            return jnp.where((kpos + k0) <= qpos, bias, NEG) if last else bias

        def win_bias(k0, last):
            rel = qpos - (kpos + k0)
            return jnp.where(rel >= 0, jnp.where(rel < WINDOW, 0.0, NEG), NEG)

        o_sel = attend(qk, nkv_ref, 2 * LANES, 3 * LANES, 0, (s0 + tq + tk - 1) // tk, sel_bias)
        w_lo = jnp.maximum(s0 - WINDOW + 1, 0) // tk
        o_win = attend(qk, win_ref, 0, LANES, w_lo, (s0 + tq + tk - 1) // tk, win_bias)
        for g in range(G_NSA):
            r = GATE0 + 3 * (kv * G_NSA + g)
            outs.append(gates[r:r + 1] * o_cmp[g] + gates[r + 1:r + 2] * o_sel[g] + gates[r + 2:r + 3] * o_win[g])
    top = lax.broadcasted_iota(i32, (LANES, tq), 0) < HALF
    for p in range(G_NSA):
        o_ref[:, p * LANES:(p + 1) * LANES] = jnp.where(top, outs[p], outs[G_NSA + p]).T.astype(bf16)


def _nsa_attn(nq, kcm, nkvb, winb, misc, b, t):
    tq, tk = NSA_TQ, NSA_TK
    assert tk % tq == 0
    nq_t = t // tq
    nbc = t // CMP_BLOCK
    return pl.pallas_call(
        functools.partial(_nsa_kernel, tq=tq, tk=tk, t=t),
        grid=(b, nq_t),
        in_specs=[pl.BlockSpec((tq, 384), lambda i, j: (i * nq_t + j, 0)),
                  pl.BlockSpec((nbc, 256), lambda i, j: (i, 0)),
                  pl.BlockSpec((t, 512), lambda i, j: (i, 0)),
                  pl.BlockSpec((t, 256), lambda i, j: (i, 0)),
                  pl.BlockSpec((tq, LANES), lambda i, j: (i * nq_t + j, 0))],
        out_specs=pl.BlockSpec((tq, 384), lambda i, j: (i * nq_t + j, 0)),
        out_shape=jax.ShapeDtypeStruct((b * t, 384), bf16),
        compiler_params=_cp(("parallel", "parallel")),
        name="nsa_attn",
    )(nq, kcm, nkvb, winb, misc)


def _pool_project(d, wp_ref, ps_ref):
    return (jnp.dot(d.astype(bf16), wp_ref[...], preferred_element_type=f32) * ps_ref[...]).astype(bf16)


def _out_tail(x, ofox, onsa, opool, wo_ref, g_ref, gate):
    o = (jnp.dot(ofox, wo_ref[0:384, :], preferred_element_type=f32)
         + jnp.dot(onsa, wo_ref[384:768, :], preferred_element_type=f32)
         + jnp.dot(opool, wo_ref[768:1024, :], preferred_element_type=f32))
    return x + gate * _rms(o, g_ref[...])


def _outproj_kernel(x_ref, ofox_ref, onsa_ref, u_ref, halo_ref, wo_ref, wp_ref, ps_ref, g_ref, gate_ref,
                    o_ref, ext_ref, *, tm, t):
    i = pl.program_id(0)
    pos0 = (i * tm) % t
    u = u_ref[...]
    ext_ref[0:16, :] = jnp.where(pos0 > 0, halo_ref[...], 0.0)
    ext_ref[16:, :] = u
    pos = lax.broadcasted_iota(i32, (tm, LANES), 0) + pos0
    lo = lax.broadcasted_iota(i32, (tm, LANES), 1) < HALF

    def shifted(k, c):
        return ext_ref[16 - k:16 - k + tm, c * LANES:(c + 1) * LANES]

    def cnt(w):
        return jnp.minimum(w, pos + 1).astype(f32)

    ds = []
    for c, (wa, wb) in enumerate(((POOL_WINDOWS[0], POOL_WINDOWS[1]), (POOL_WINDOWS[2], POOL_WINDOWS[3]))):
        run = shifted(0, c)
        sums = {}
        for k in range(1, wb):
            run = run + shifted(k, c)
            if k + 1 in (wa, wb):
                sums[k + 1] = run
        ds.append(jnp.where(lo, sums[wa] / cnt(wa), sums[wb] / cnt(wb)) - u[:, c * LANES:(c + 1) * LANES])
    opool = _pool_project(jnp.concatenate(ds, axis=1), wp_ref, ps_ref)
    o_ref[...] = _out_tail(x_ref[...], ofox_ref[...], onsa_ref[...], opool, wo_ref, g_ref, gate_ref[0])


def _outproj(x2, ofox, onsa, u, wo, wp, ps, g1, gate, *, tm, t):
    m = x2.shape[0]
    row = lambda w_: pl.BlockSpec((tm, w_), lambda i: (i, 0))
    const = lambda s: pl.BlockSpec(s, lambda i: (0, 0))
    return pl.pallas_call(
        functools.partial(_outproj_kernel, tm=tm, t=t),
        grid=(m // tm,),
        in_specs=[row(D_MODEL), row(384), row(384), row(256),
                  pl.BlockSpec((16, 256), lambda i: (jnp.maximum(i * (tm // 16) - 1, 0), 0)),
                  const((D_MODEL, D_MODEL)), const((256, 256)), const((1, 256)), const((1, D_MODEL)),
                  pl.BlockSpec((1, 1, D_MODEL), lambda i: ((i * tm) // t, 0, 0))],
        out_specs=row(D_MODEL),
        out_shape=jax.ShapeDtypeStruct((m, D_MODEL), f32),
        scratch_shapes=[pltpu.VMEM((tm + 16, 256), f32)],
        compiler_params=_cp(("parallel",)),
        name="out_proj",
    )(x2, ofox, onsa, u, u, wo, wp, ps, g1, gate)


def _outproj_dec_kernel(x_ref, ofox_ref, onsa_ref, ext_ref, wo_ref, wp_ref, ps_ref, g_ref, gate_ref, o_ref, *, past):
    ext = ext_ref[...]
    n = ext.shape[1]
    r = lax.broadcasted_iota(i32, ext.shape, 1)
    lane = lax.broadcasted_iota(i32, (ext.shape[0], C_POOL), 1)
    u_new = jnp.sum(jnp.where(r == n - 1, ext, 0.0), axis=1)
    d = jnp.zeros(u_new.shape, f32)
    for g, w in enumerate(POOL_WINDOWS):
        sw = jnp.sum(jnp.where(r >= n - w, ext, 0.0), axis=1)
        d = jnp.where(lane // POOL_GW == g, sw / float(min(w, past + 1)), d)
    opool = _pool_project(d - u_new, wp_ref, ps_ref)
    o_ref[...] = _out_tail(x_ref[...], ofox_ref[...], onsa_ref[...], opool, wo_ref, g_ref, gate_ref[...])


def _outproj_dec(x2, ofox, onsa, ext, wo, wp, ps, g1, gate, past):
    m = x2.shape[0]
    full = lambda a: pl.BlockSpec(a.shape, lambda i: (0,) * a.ndim)
    args = (x2, ofox, onsa, ext, wo, wp, ps, g1, gate)
    return pl.pallas_call(
        functools.partial(_outproj_dec_kernel, past=past),
        grid=(1,),
        in_specs=[full(a) for a in args],
        out_specs=pl.BlockSpec((m, D_MODEL), lambda i: (0, 0)),
        out_shape=jax.ShapeDtypeStruct((m, D_MODEL), f32),
        compiler_params=_cp(("arbitrary",)),
        name="out_proj_dec",
    )(*args)


def _mlp_kernel(x_ref, g2_ref, sc_ref, sh_ref, w1_ref, w2_ref, g3_ref, gate_ref, o_ref, *, tf):
    x = x_ref[...]
    h = (_rms(x, g2_ref[...]) * (1.0 + sc_ref[0]) + sh_ref[0]).astype(bf16)
    acc = jnp.zeros(x.shape, f32)
    for c in range(D_FF // tf):
        a = jnp.maximum(jnp.dot(h, w1_ref[:, c * tf:(c + 1) * tf], preferred_element_type=f32), 0.0)
        acc = acc + jnp.dot((a * a).astype(bf16), w2_ref[c * tf:(c + 1) * tf, :], preferred_element_type=f32)
    o_ref[...] = x + gate_ref[0] * _rms(acc, g3_ref[...])


def _mlp(x2, g2, sc, sh, w1, w2, g3, gate, *, tm, rows_per_mod):
    m = x2.shape[0]
    mod_rows = sc.shape[1]
    modmap = lambda i: ((i * tm) // rows_per_mod, 0, 0)
    const = lambda s: pl.BlockSpec(s, lambda i: (0, 0))
    mod = pl.BlockSpec((1, mod_rows, D_MODEL), modmap)
    return pl.pallas_call(
        functools.partial(_mlp_kernel, tf=512),
        grid=(m // tm,),
        in_specs=[pl.BlockSpec((tm, D_MODEL), lambda i: (i, 0)), const((1, D_MODEL)), mod, mod,
                  const((D_MODEL, D_FF)), const((D_FF, D_MODEL)), const((1, D_MODEL)), mod],
        out_specs=pl.BlockSpec((tm, D_MODEL), lambda i: (i, 0)),
        out_shape=jax.ShapeDtypeStruct((m, D_MODEL), f32),
        compiler_params=_cp(("parallel",)),
        name="mlp",
    )(x2, g2, sc, sh, w1, w2, g3, gate)


def _pad8(a):
    return jnp.concatenate([a, jnp.zeros((8 - a.shape[0], a.shape[1]), a.dtype)], axis=0)


def _by_group(rows):
    grp = lax.broadcasted_iota(i32, (8, rows.shape[1]), 0) // G_NSA
    return jnp.where(grp == 0, jnp.broadcast_to(rows[0:1], (8, rows.shape[1])),
                     jnp.broadcast_to(rows[1:2], (8, rows.shape[1])))


def _lane_to_sublane(row, offset, stride):
    sub = lax.broadcasted_iota(i32, (8, LANES), 0)
    lane = lax.broadcasted_iota(i32, (8, LANES), 1)
    return jnp.sum(jnp.where(lane == offset + stride * sub, jnp.broadcast_to(row, (8, LANES)), 0.0),
                   axis=1, keepdims=True)


def _fox_dec_kernel(pt_ref, fq_ref, misc_ref, bias_ref, knew_ref, *rest, pps):
    lf_refs = rest[:pps]
    kv_refs = rest[pps:2 * pps]
    o_ref, lfo_ref, m_scr, l_scr, acc_scr, c_scr = rest[2 * pps:]
    g = pl.program_id(1)
    n = pps * PAGE_SIZE
    q6f = _pad8(fq_ref[0])
    q6 = q6f.astype(bf16)
    lane = lax.broadcasted_iota(i32, (8, LANES), 1)

    @pl.when(g == 0)
    def _():
        lfrow = jnp.where(lane[0:1] < H_FOX, _log_sigmoid(misc_ref[0] + bias_ref[...]), 0.0)
        lfo_ref[0] = lfrow
        m_scr[...] = jnp.sum(q6f * _pad8(knew_ref[0, 0]), axis=1, keepdims=True)
        l_scr[...] = jnp.ones((8, 1), f32)
        acc_scr[...] = _pad8(knew_ref[0, 1])
        c_scr[...] = _lane_to_sublane(lfrow, 0, 1)

    carry = c_scr[...]
    biases = []
    n_pages = pl.num_programs(1) * pps
    for i in range(pps):
        row = pt_ref[pl.program_id(0), n_pages - 1 - (g * pps + i)] % 8
        lf = _pad8(jnp.concatenate([lf_refs[i][0, h, 0, pl.ds(row, 1), :] for h in range(H_FOX)], axis=0))
        suf = lf
        for sft in (1, 2, 4, 8, 16, 32, 64):
            suf = suf + jnp.where(lane + sft < LANES, pltpu.roll(suf, LANES - sft, 1), 0.0)
        biases.append(carry + (suf - lf))
        carry = carry + suf[:, 0:1]
    c_scr[...] = carry
    s = jnp.concatenate(biases, axis=1)
    sub = lax.broadcasted_iota(i32, (8, n), 0)
    for h in range(H_FOX):
        kht = jnp.concatenate([kv_refs[i][0, 0, 0, h] for i in range(pps)], axis=1).astype(bf16)
        s = jnp.where(sub == h, s + jnp.dot(q6, kht, preferred_element_type=f32), s)
    m = m_scr[...]
    m_new = jnp.maximum(m, jnp.max(s, axis=1, keepdims=True))
    alpha = jnp.exp(m - m_new)
    p = jnp.exp(s - m_new)
    l_scr[...] = alpha * l_scr[...] + jnp.sum(p, axis=1, keepdims=True)
    m_scr[...] = m_new
    pb = p.astype(bf16)
    sub_d = lax.broadcasted_iota(i32, (8, HEAD_DIM), 0)
    pv = jnp.zeros((8, HEAD_DIM), f32)
    for h in range(H_FOX):
        vht = jnp.concatenate([kv_refs[i][0, 0, 1, h] for i in range(pps)], axis=1).astype(bf16)
        pv = jnp.where(sub_d == h, _nt(pb, vht), pv)
    acc_scr[...] = alpha * acc_scr[...] + pv

    @pl.when(g == pl.num_programs(1) - 1)
    def _():
        o_ref[0] = acc_scr[...] / l_scr[...]


def _fox_decode(layer, page_table, fq, misc, bias_row, fkv_new, lf_t, cache_kv):
    b, n_pages = page_table.shape
    pps = FOX_PPS
    ng = n_pages // pps
    row3 = lambda w_: pl.BlockSpec((1, 1, w_), lambda i, g, pt: (i, 0, 0))

    def page_id(bi, g, pt, i):
        return pt[jnp.minimum(bi, b - 1), n_pages - 1 - (jnp.minimum(g, ng - 1) * pps + i)]

    def page(i, shape):
        zeros = (0,) * len(shape)
        return pl.BlockSpec((1, 1) + shape, lambda bi, g, pt, i=i: (layer, page_id(bi, g, pt, i)) + zeros)

    def lf_group(i):
        return pl.BlockSpec((1, H_FOX, 1, 8, PAGE_SIZE),
                            lambda bi, g, pt, i=i: (layer, 0, page_id(bi, g, pt, i) // 8, 0, 0))

    grid_spec = pltpu.PrefetchScalarGridSpec(
        num_scalar_prefetch=1,
        grid=(b, ng),
        in_specs=[pl.BlockSpec((1, H_FOX, HEAD_DIM), lambda i, g, pt: (i, 0, 0)), row3(LANES),
                  pl.BlockSpec((1, LANES), lambda i, g, pt: (0, 0)),
                  pl.BlockSpec((1, 2, H_FOX, HEAD_DIM), lambda i, g, pt: (i, 0, 0, 0))]
                 + [lf_group(i) for i in range(pps)]
                 + [page(i, (2, H_FOX, HEAD_DIM, PAGE_SIZE)) for i in range(pps)],
        out_specs=[pl.BlockSpec((1, 8, HEAD_DIM), lambda i, g, pt: (i, 0, 0)), row3(LANES)],
        scratch_shapes=[pltpu.VMEM((8, 1), f32), pltpu.VMEM((8, 1), f32), pltpu.VMEM((8, HEAD_DIM), f32),
                        pltpu.VMEM((8, 1), f32)],
    )
    return pl.pallas_call(
        functools.partial(_fox_dec_kernel, pps=pps),
        grid_spec=grid_spec,
        out_shape=[jax.ShapeDtypeStruct((b, 8, HEAD_DIM), f32), jax.ShapeDtypeStruct((b, 1, LANES), f32)],
        compiler_params=_cp(("parallel", "arbitrary")),
        name="fox_decode",
    )(page_table, fq, misc, bias_row, fkv_new, *([lf_t] * pps), *([cache_kv] * pps))


def _nsa_cmp_dec_kernel(pt_ref, q_ref, *rest, pps, past):
    pg_refs = rest[:pps]
    p_ref, o_ref, mean_scr = rest[pps:]
    g = pl.program_id(1)
    per_page = PAGE_SIZE // CMP_BLOCK
    rows = 2 * KV_NSA * HEAD_DIM
    x = jnp.concatenate([pg_refs[i][0, 0].reshape(rows, PAGE_SIZE) for i in range(pps)], axis=1)
    r = lax.broadcasted_iota(i32, (pps * PAGE_SIZE, LANES), 0)
    c = lax.broadcasted_iota(i32, (pps * PAGE_SIZE, LANES), 1)
    avg = jnp.where(r // CMP_BLOCK == c, 1.0 / CMP_BLOCK, 0.0).astype(bf16)
    hi = x.astype(bf16)
    lo = (x - hi.astype(f32)).astype(bf16)
    mean_scr[g] = jnp.dot(hi, avg, preferred_element_type=f32) + jnp.dot(lo, avg, preferred_element_type=f32)

    @pl.when(g == pl.num_programs(1) - 1)
    def _():
        ng = mean_scr.shape[0]
        nbc = ng * pps * per_page
        mean = mean_scr[0]
        for gg in range(1, ng):
            mean = mean + pltpu.roll(mean_scr[gg], gg * pps * per_page, 1)
        mean = mean[:, :nbc].astype(bf16)
        part = lambda feat, kv: mean[(feat * KV_NSA + kv) * HEAD_DIM:(feat * KV_NSA + kv + 1) * HEAD_DIM]
        q6 = _pad8(q_ref[0]).astype(bf16)
        grp0 = lax.broadcasted_iota(i32, (8, 1), 0) // G_NSA == 0
        s = jnp.where(grp0, jnp.dot(q6, part(0, 0), preferred_element_type=f32),
                      jnp.dot(q6, part(0, 1), preferred_element_type=f32))
        j = lax.broadcasted_iota(i32, (8, nbc), 1)
        mask = ((j + 1) * CMP_BLOCK - 1) <= past
        s = jnp.where(mask, s, NEG)
        pe = jnp.where(mask, jnp.exp(s - jnp.max(s, axis=1, keepdims=True)), 0.0)
        den = jnp.sum(pe, axis=1, keepdims=True)
        p = pe / jnp.where(den > 0.0, den, 1.0)
        p_ref[0] = p
        pb = p.astype(bf16)
        o_ref[0] = jnp.where(grp0, _nt(pb, part(1, 0)), _nt(pb, part(1, 1)))


def _nsa_cmp_decode(layer, page_table, nq, cache_nsa, past):
    b, n_pages = page_table.shape
    pps = CMP_PPS
    nbc = past // CMP_BLOCK
    grid_spec = pltpu.PrefetchScalarGridSpec(
        num_scalar_prefetch=1,
        grid=(b, n_pages // pps),
        in_specs=[pl.BlockSpec((1, H_NSA, HEAD_DIM), lambda i, g, pt: (i, 0, 0))]
                 + [pl.BlockSpec((1, 1, 2, KV_NSA, HEAD_DIM, PAGE_SIZE),
                                 lambda bi, g, pt, i=i: (layer, pt[jnp.minimum(bi, b - 1),
                                                                   jnp.minimum(g, n_pages // pps - 1) * pps + i],
                                                         0, 0, 0, 0))
                    for i in range(pps)],
        out_specs=[pl.BlockSpec((1, 8, nbc), lambda i, g, pt: (i, 0, 0)),
                   pl.BlockSpec((1, 8, HEAD_DIM), lambda i, g, pt: (i, 0, 0))],
        scratch_shapes=[pltpu.VMEM((n_pages // pps, 2 * KV_NSA * HEAD_DIM, LANES), f32)],
    )
    return pl.pallas_call(
        functools.partial(_nsa_cmp_dec_kernel, pps=pps, past=past),
        grid_spec=grid_spec,
        out_shape=[jax.ShapeDtypeStruct((b, 8, nbc), f32), jax.ShapeDtypeStruct((b, 8, HEAD_DIM), f32)],
        compiler_params=_cp(("parallel", "arbitrary")),
        name="nsa_cmp_decode",
    )(page_table, nq, *([cache_nsa] * pps))


def _nsa_topk_dec_kernel(p_ref, idx_ref, *, past, width):
    b = p_ref.shape[0]
    nbc = p_ref.shape[2]
    nbs = -(-(past + 1) // SEL_BLOCK)
    imps = []
    for kv in range(KV_NSA):
        imps.append(p_ref[:, kv * G_NSA, :] + p_ref[:, kv * G_NSA + 1, :] + p_ref[:, kv * G_NSA + 2, :])
    imp = jnp.concatenate(imps, axis=0)
    imp = jnp.concatenate([imp, jnp.zeros((2 * b, width - nbc), f32)], axis=1)
    j = lax.broadcasted_iota(i32, (2 * b, width), 1)
    cur = past // SEL_BLOCK
    forced = (j == 0) | (j == cur) | (j == cur - 1)
    score = jnp.where(forced, FORCE, imp)
    score = jnp.where(j * SEL_BLOCK <= past, score, -1.0)
    score = jnp.where(j < nbs, score, -2.0)
    _, firsts = _select_topk(score, min(TOPK_BLOCKS, nbs))
    lane = lax.broadcasted_iota(i32, (2 * b, LANES), 1)
    out = jnp.zeros((2 * b, LANES), i32)
    for k, first in enumerate(firsts):
        out = jnp.where(lane == k, first.astype(i32), out)
    idx_ref[...] = out


def _nsa_topk_decode(p8, past):
    b, _, nbc = p8.shape
    width = 2 * nbc
    return pl.pallas_call(
        functools.partial(_nsa_topk_dec_kernel, past=past, width=width),
        grid=(1,),
        in_specs=[pl.BlockSpec(p8.shape, lambda i: (0, 0, 0))],
        out_specs=pl.BlockSpec((2 * b, LANES), lambda i: (0, 0)),
        out_shape=jax.ShapeDtypeStruct((2 * b, LANES), i32),
        compiler_params=_cp(("arbitrary",)),
        name="nsa_topk_decode",
    )(p8)


def _nsa_sel_dec_kernel(pt_ref, idx_ref, q_ref, misc_ref, ocmp_ref, nkv_ref, wnew_ref, win_ref, *rest, past, nsel):
    blk_refs = rest[:KV_NSA * nsel]
    o_ref = rest[KV_NSA * nsel]
    b = pl.program_id(0)
    n_past_blk = past // SEL_BLOCK
    q8f = _pad8(q_ref[0])
    q8 = q8f.astype(bf16)
    sub = lax.broadcasted_iota(i32, (8, 1), 0)
    grp0 = sub // G_NSA == 0
    per_page = PAGE_SIZE // SEL_BLOCK
    lane_blk = lax.broadcasted_iota(i32, (8, PAGE_SIZE), 1) // SEL_BLOCK
    m = jnp.full((8, 1), NEG, f32)
    l = jnp.zeros((8, 1), f32)
    acc = jnp.zeros((8, HEAD_DIM), f32)
    s_new = jnp.sum(q8f * _by_group(nkv_ref[0, 2]), axis=1, keepdims=True)
    v_new = _by_group(nkv_ref[0, 3])
    for kv in range(KV_NSA):
        in_group = (sub // G_NSA) == kv
        has_new = jnp.zeros((), jnp.bool_)
        for k in range(nsel):
            j = idx_ref[kv * pl.num_programs(0) + b, k]
            has_new = has_new | (j == n_past_blk)
            blk = blk_refs[kv * nsel + k]
            mask = in_group & (j < n_past_blk) & (lane_blk == j % per_page)
            s = jnp.where(mask, jnp.dot(q8, blk[0, 0, 0, 0].astype(bf16), preferred_element_type=f32), NEG)
            m_new = jnp.maximum(m, jnp.max(s, axis=1, keepdims=True))
            alpha = jnp.exp(m - m_new)
            pe = jnp.where(mask, jnp.exp(s - m_new), 0.0)
            l = alpha * l + jnp.sum(pe, axis=1, keepdims=True)
            acc = alpha * acc + _nt(pe.astype(bf16), blk[0, 0, 1, 0].astype(bf16))
            m = m_new
        new_ok = in_group & has_new
        m_new = jnp.maximum(m, jnp.where(new_ok, s_new, NEG))
        alpha = jnp.exp(m - m_new)
        p_new = jnp.where(new_ok, jnp.exp(s_new - m_new), 0.0)
        l = alpha * l + p_new
        acc = alpha * acc + p_new * v_new
        m = m_new
    o_sel = acc / jnp.where(l > 0.0, l, 1.0)
    wb = win_ref.shape[5]
    kpos = past - wb + lax.broadcasted_iota(i32, (8, wb), 1)
    rel = past - kpos
    wmask = (rel >= 0) & (rel < WINDOW) & (kpos >= 0)
    s = jnp.where(grp0, jnp.dot(q8, win_ref[0, 0, 0, 0].astype(bf16), preferred_element_type=f32),
                  jnp.dot(q8, win_ref[0, 0, 0, 1].astype(bf16), preferred_element_type=f32))
    s = jnp.where(wmask, s, NEG)
    s_w = jnp.sum(q8f * _by_group(wnew_ref[0, 0]), axis=1, keepdims=True)
    m = jnp.maximum(jnp.max(s, axis=1, keepdims=True), s_w)
    p = jnp.where(wmask, jnp.exp(s - m), 0.0)
    p_w = jnp.exp(s_w - m)
    pb = p.astype(bf16)
    pv = jnp.where(grp0, _nt(pb, win_ref[0, 0, 1, 0].astype(bf16)), _nt(pb, win_ref[0, 0, 1, 1].astype(bf16)))
    o_win = (pv + p_w * _by_group(wnew_ref[0, 1])) / (jnp.sum(p, axis=1, keepdims=True) + p_w)
    misc = misc_ref[0]
    gates = [jax.nn.sigmoid(_lane_to_sublane(misc, GATE0 + r, 3)) for r in range(3)]
    o_ref[0] = gates[0] * ocmp_ref[0] + gates[1] * o_sel + gates[2] * o_win


def _nsa_sel_decode(layer, page_table, idx, nq, misc, ocmp, nkv_new, win_new, cache_win, cache_nsa, past):
    b = page_table.shape[0]
    nsel = idx.shape[1]
    n_past_blk = past // SEL_BLOCK
    per_page = PAGE_SIZE // SEL_BLOCK
    wb = cache_win.shape[5]

    def blk_spec(kv, k):
        def imap(i, pt, ix):
            ii = jnp.minimum(i, b - 1)
            jp = jnp.clip(ix[kv * b + ii, k], 0, n_past_blk - 1)
            return (layer, pt[ii, jp // per_page], 1, kv, 0, 0)
        return pl.BlockSpec((1, 1, 2, 1, HEAD_DIM, PAGE_SIZE), imap)

    grid_spec = pltpu.PrefetchScalarGridSpec(
        num_scalar_prefetch=2,
        grid=(b,),
        in_specs=[pl.BlockSpec((1, H_NSA, HEAD_DIM), lambda i, pt, ix: (i, 0, 0)),
                  pl.BlockSpec((1, 1, LANES), lambda i, pt, ix: (i, 0, 0)),
                  pl.BlockSpec((1, 8, HEAD_DIM), lambda i, pt, ix: (i, 0, 0)),
                  pl.BlockSpec((1, 4, KV_NSA, HEAD_DIM), lambda i, pt, ix: (i, 0, 0, 0)),
                  pl.BlockSpec((1, 2, KV_NSA, HEAD_DIM), lambda i, pt, ix: (i, 0, 0, 0)),
                  pl.BlockSpec((1, 1, 2, KV_NSA, HEAD_DIM, wb), lambda i, pt, ix: (layer, i, 0, 0, 0, 0))]
                 + [blk_spec(kv, k) for kv in range(KV_NSA) for k in range(nsel)],
        out_specs=pl.BlockSpec((1, 8, HEAD_DIM), lambda i, pt, ix: (i, 0, 0)),
    )
    return pl.pallas_call(
        functools.partial(_nsa_sel_dec_kernel, past=past, nsel=nsel),
        grid_spec=grid_spec,
        out_shape=jax.ShapeDtypeStruct((b, 8, HEAD_DIM), f32),
        compiler_params=_cp(("parallel",)),
        name="nsa_sel_decode",
    )(page_table, idx, nq, misc, ocmp, nkv_new, win_new, cache_win, *([cache_nsa] * (KV_NSA * nsel)))


def _proj_columns():
    off = np.cumsum([0, 384, 384, 384, 6, 384, 128, 128, 128, 128, 128, 128, 18, 256])
    fq, fk, fv, ff, nq, nkc, nvc, nks, nvs, nkw, nvw, ng, u = [int(o) for o in off[:13]]
    cols = list(range(fq, fq + 384))
    for h in NSA_PAIR_ORDER:
        cols += list(range(nq + h * HEAD_DIM, nq + (h + 1) * HEAD_DIM))
    cols += list(range(fk, fk + 768))
    cols += list(range(nkc, nkc + 512))
    cols += list(range(nkw, nkw + 256))
    cols += list(range(u, u + 256))
    cols += list(range(ff, ff + 6)) + list(range(ng, ng + 18)) + [N_IN] * (LANES - 24)
    return np.asarray(cols, np.int32)


def _out_rows():
    rows = list(range(0, 384))
    for h in NSA_PAIR_ORDER:
        rows += list(range(384 + h * HEAD_DIM, 384 + (h + 1) * HEAD_DIM))
    rows += list(range(768, 1024))
    return np.asarray(rows, np.int32)


def _rope_tables(pos):
    half = ROT_DIM // 2
    inv = ROPE_THETA ** (-jnp.arange(0, ROT_DIM, 2, dtype=f32) / ROT_DIM)
    ang = pos.astype(f32)[:, None] * inv[None, :]
    cos, sin = jnp.cos(ang), jnp.sin(ang)
    n = pos.shape[0]
    one = jnp.ones((n, HEAD_DIM - ROT_DIM), f32)
    zero8 = jnp.zeros((n, half), f32)
    zrest = jnp.zeros((n, HEAD_DIM - ROT_DIM), f32)
    c = jnp.concatenate([cos, cos, one], axis=1)
    sa = jnp.concatenate([-sin, zero8, zrest], axis=1)
    sb = jnp.concatenate([zero8, sin, zrest], axis=1)
    return tuple(jnp.concatenate([a, a], axis=1) for a in (c, sa, sb))


def kernel(x_prompt, x_sample, cache_fox_kv, cache_fox_logf, cache_nsa_kv, cache_nsa_win, state_pool, page_table,
           c_prompt, c_sample, w_ada, b_ada, norm_g, w_in, b_fox_f, w_out, w_pool, pool_scale, w_ff1, w_ff2):
    depth = w_in.shape[0]
    bp, t, _ = x_prompt.shape
    bs = x_sample.shape[0]
    past = page_table.shape[1] * PAGE_SIZE
    mp = bp * t

    cols = _proj_columns()
    w_in_p = jnp.concatenate([w_in, jnp.zeros((depth, D_MODEL, 1), f32)], axis=2)[:, :, cols].astype(bf16)
    w_out_p = w_out[:, _out_rows(), :].astype(bf16)
    w_pool_bd = jnp.zeros((depth, C_POOL, C_POOL), f32)
    for g in range(len(POOL_WINDOWS)):
        w_pool_bd = w_pool_bd.at[:, g * POOL_GW:(g + 1) * POOL_GW, g * POOL_GW:(g + 1) * POOL_GW].set(w_pool[:, g])
    w_pool_bd = w_pool_bd.astype(bf16)
    w1 = w_ff1.astype(bf16)
    w2 = w_ff2.astype(bf16)
    bias_rows = jnp.pad(b_fox_f, ((0, 0), (0, LANES - H_FOX))).reshape(depth, 1, LANES)

    rope_p = _rope_tables(jnp.arange(t))
    rope_s = _rope_tables(jnp.full((bs,), past, i32))

    to_last = (0, 1, 3, 4, 5, 2)
    fox_t = jnp.transpose(cache_fox_kv, to_last)
    nsa_t = jnp.transpose(cache_nsa_kv, to_last)
    win_t = jnp.transpose(cache_nsa_win, to_last)
    n_pool = cache_fox_logf.shape[1]
    assert n_pool % 8 == 0
    lf_t = jnp.transpose(cache_fox_logf, (0, 3, 1, 2)).reshape(depth, H_FOX, n_pool // 8, 8, PAGE_SIZE)
    nat_from_pair = np.argsort(np.asarray(NSA_PAIR_ORDER))

    mod = _ada(jnp.concatenate([c_prompt, c_sample], axis=0), w_ada, b_ada)
    mod = mod.reshape(depth, bp + bs, 6, D_MODEL)

    yp = x_prompt.reshape(mp, D_MODEL)
    ys = x_sample.reshape(bs, D_MODEL)
    sp, ss = [], []
    kv_stacks = None
    for l in range(depth):
        g = norm_g[l].reshape(4, 1, D_MODEL)
        modp = [mod[l, :bp, k].reshape(bp, 1, D_MODEL) for k in range(6)]
        mods = [mod[l, bp:, k].reshape(1, bs, D_MODEL) for k in range(6)]

        (fq, nq, fkv, nkv, win, u, misc, fkvb, nkvb, winb, kcm) = _inproj(
            yp, g[0], modp[1], modp[0], w_in_p[l], *rope_p, tm=512, rows_per_mod=t, rope_rows=t, with_means=True,
            layer=l, depth=depth, stacked=kv_stacks)
        kv_stacks = (fkv, nkv)
        logf, ccol, crow = _foxprep(misc, bias_rows[l], bp, t)
        o_fox = _fox_attn(fq, fkvb, ccol, crow, bp, t)
        o_nsa = _nsa_attn(nq, kcm, nkvb, winb, misc, bp, t)
        y1 = _outproj(yp, o_fox, o_nsa, u, w_out_p[l], w_pool_bd[l], pool_scale[l].reshape(1, C_POOL), g[1],
                      modp[2], tm=512, t=t)
        yp = _mlp(y1, g[2], modp[4], modp[3], w1[l], w2[l], g[3], modp[5], tm=512, rows_per_mod=t)
        wl = min(WINDOW, t)
        sp.append((logf.reshape(bp, t, H_FOX),
                   win.reshape(bp, t, 2, KV_NSA, HEAD_DIM)[:, t - wl:],
                   u.reshape(bp, t, C_POOL)[:, t - POOL_STATE:]))

        (fq_s, nq_s, fkv_s, nkv_s, win_s, u_s, misc_s, _, _, _, _) = _inproj(
            ys, g[0], mods[1], mods[0], w_in_p[l], *rope_s, tm=bs, rows_per_mod=bs, rope_rows=bs, with_means=False)
        r3 = lambda a: a.reshape(bs, 1, a.shape[-1])
        fq6 = fq_s.astype(f32).reshape(bs, H_FOX, HEAD_DIM)
        nq6 = nq_s.astype(f32).reshape(bs, H_NSA, HEAD_DIM)[:, nat_from_pair]
        nkv4 = nkv_s.reshape(bs, 4, KV_NSA, HEAD_DIM)
        win2 = win_s.reshape(bs, 2, KV_NSA, HEAD_DIM)
        o_fox_s, logf_s = _fox_decode(l, page_table, fq6, r3(misc_s), bias_rows[l],
                                      fkv_s.reshape(bs, 2, H_FOX, HEAD_DIM), lf_t, fox_t)
        p8, o_cmp = _nsa_cmp_decode(l, page_table, nq6, nsa_t, past)
        idx = _nsa_topk_decode(p8, past)[:, :min(TOPK_BLOCKS, -(-(past + 1) // SEL_BLOCK))]
        o_nsa_s = _nsa_sel_decode(l, page_table, idx, nq6, r3(misc_s), o_cmp, nkv4, win2,
                                  win_t, nsa_t, past)
        o_fox_s = o_fox_s[:, :H_FOX].reshape(bs, 384).astype(bf16)
        o_nsa_s = o_nsa_s[:, np.asarray(NSA_PAIR_ORDER)].reshape(bs, 384).astype(bf16)
        u_ext = jnp.concatenate([state_pool[l], u_s.reshape(bs, 1, C_POOL)], axis=1)
        y1s = _outproj_dec(ys, o_fox_s, o_nsa_s, u_ext, w_out_p[l],
                           w_pool_bd[l], pool_scale[l].reshape(1, C_POOL), g[1], mods[2][0], past)
        ys = _mlp(y1s, g[2], mods[4], mods[3], w1[l], w2[l], g[3], mods[5], tm=bs, rows_per_mod=bs)
        win_all = jnp.concatenate([cache_nsa_win[l], win2[:, None]], axis=1)
        ss.append((fkv_s.reshape(bs, 1, 2, H_FOX, HEAD_DIM), logf_s[:, :, :H_FOX], nkv4[:, None],
                   win_all[:, 1:], u_ext[:, 1:]))

    stk = lambda lst, i: jnp.stack([s[i] for s in lst], axis=0)
    tokens_first = (0, 1, 5, 2, 3, 4)
    fkv_all = jnp.transpose(kv_stacks[0].reshape(depth, bp, 2, H_FOX, HEAD_DIM, t), tokens_first)
    nkv_all = jnp.transpose(kv_stacks[1].reshape(depth, bp, 4, KV_NSA, HEAD_DIM, t), tokens_first)
    return (yp.reshape(bp, t, D_MODEL), ys.reshape(bs, 1, D_MODEL),
            fkv_all, stk(sp, 0), nkv_all, stk(sp, 1), stk(sp, 2),
            stk(ss, 0), stk(ss, 1), stk(ss, 2), stk(ss, 3), stk(ss, 4))
```

```python
import functools

import numpy as np
import jax
import jax.numpy as jnp
from jax import lax
from jax.experimental import pallas as pl
from jax.experimental.pallas import tpu as pltpu

f32 = jnp.float32
bf16 = jnp.bfloat16
i32 = jnp.int32

D_MODEL = 1024
HEAD_DIM = 64
H_FOX = 6
H_NSA = 6
KV_NSA = 2
G_NSA = H_NSA // KV_NSA
POOL_WINDOWS = (2, 4, 8, 16)
C_POOL = 256
POOL_GW = 64
POOL_STATE = 15
ROT_DIM = 16
ROPE_THETA = 500000.0
CMP_BLOCK = 64
SEL_BLOCK = 64
TOPK_BLOCKS = 8
WINDOW = 512
PAGE_SIZE = 128
D_FF = 4 * D_MODEL
EPS = 1e-6
NEG = -1e30
FORCE = 1e4
SCALE = HEAD_DIM ** -0.5
N_IN = 2584

LANES = 128
HALF = LANES // 2
VMEM_LIMIT = 56 * 1024 * 1024

C_FQ = 0
C_NQ = 384
C_FKV = 768
C_NKV = 1536
C_WIN = 2048
C_U = 2304
C_MISC = 2560
N_PROJ = 2688
GATE0 = H_FOX
NSA_PAIR_ORDER = (0, 3, 1, 4, 2, 5)

FOX_TQ = 512
FOX_TK = 512
NSA_TQ = 512
NSA_TK = 512
FOX_PPS = 16
CMP_PPS = 16


def _cp(sem):
    return pltpu.CompilerParams(dimension_semantics=sem, vmem_limit_bytes=VMEM_LIMIT)


def _nt(a, b):
    return lax.dot_general(a, b, (((1,), (1,)), ((), ())), preferred_element_type=f32)


def _rms(x, g):
    return x * lax.rsqrt(jnp.mean(x * x, axis=-1, keepdims=True) + EPS) * g


def _log_sigmoid(x):
    return jnp.minimum(x, 0.0) - jnp.log1p(jnp.exp(-jnp.abs(x)))


def _ada_kernel(c_ref, w_ref, b_ref, o_ref):
    c = c_ref[...]
    a = (c * jax.nn.sigmoid(c)).astype(bf16)
    o_ref[0] = jnp.dot(a, w_ref[0].astype(bf16), preferred_element_type=f32) + b_ref[0]


def _ada(c_all, w_ada, b_ada):
    depth = w_ada.shape[0]
    n = c_all.shape[0]
    tn = 1024
    return pl.pallas_call(
        _ada_kernel,
        grid=(depth, 6 * D_MODEL // tn),
        in_specs=[pl.BlockSpec((n, D_MODEL), lambda l, j: (0, 0)),
                  pl.BlockSpec((1, D_MODEL, tn), lambda l, j: (l, 0, j)),
                  pl.BlockSpec((1, 1, tn), lambda l, j: (l, 0, j))],
        out_specs=pl.BlockSpec((1, n, tn), lambda l, j: (l, 0, j)),
        out_shape=jax.ShapeDtypeStruct((depth, n, 6 * D_MODEL), f32),
        compiler_params=_cp(("parallel", "parallel")),
        name="ada_mod",
    )(c_all, w_ada, b_ada.reshape(depth, 1, 6 * D_MODEL))


def _inproj_kernel(x_ref, g_ref, sc_ref, sh_ref, w_ref, cos_ref, sa_ref, sb_ref, *rest, tm, with_means, state_t):
    (fq_ref, nq_ref, fkv_ref, nkv_ref, win_ref, u_ref, misc_ref, fkvb_ref, nkvb_ref, winb_ref, kcm_ref) = rest[-11:]
    x = x_ref[...]
    h = (_rms(x, g_ref[...]) * (1.0 + sc_ref[0]) + sh_ref[0]).astype(bf16)
    cos = cos_ref[...]
    sa = sa_ref[...]
    sb = sb_ref[...]

    def rope(z):
        return z * cos + pltpu.roll(z, LANES - ROT_DIM // 2, 1) * sa + pltpu.roll(z, ROT_DIM // 2, 1) * sb

    def means(z):
        return jnp.sum(z.reshape(tm // CMP_BLOCK, CMP_BLOCK, LANES), axis=1) * (1.0 / CMP_BLOCK)

    def state(ref, col, z):
        if state_t:
            ref[0, 0, col:col + LANES, :] = z.T
        else:
            ref[:, col:col + LANES] = z

    def emit(blk, z):
        c = blk * LANES
        if c < C_NQ:
            fq_ref[:, c - C_FQ:c - C_FQ + LANES] = (z * SCALE).astype(bf16)
        elif c < C_FKV:
            nq_ref[:, c - C_NQ:c - C_NQ + LANES] = (rope(z) * SCALE).astype(bf16)
        elif c < C_NKV:
            state(fkv_ref, c - C_FKV, z)
            fkvb_ref[:, c - C_FKV:c - C_FKV + LANES] = z.astype(bf16)
        elif c < C_WIN:
            o = c - C_NKV
            if o in (0, 2 * LANES):
                z = rope(z)
            state(nkv_ref, o, z)
            nkvb_ref[:, o:o + LANES] = z.astype(bf16)
            if with_means and o < 2 * LANES:
                kcm_ref[:, o:o + LANES] = means(z)
        elif c < C_U:
            o = c - C_WIN
            if o == 0:
                z = rope(z)
            win_ref[:, o:o + LANES] = z
            winb_ref[:, o:o + LANES] = z.astype(bf16)
        elif c < C_MISC:
            u_ref[:, c - C_U:c - C_U + LANES] = z
        else:
            misc_ref[...] = z

    nblk = N_PROJ // LANES
    for b0 in range(0, nblk, 2):
        nb = min(2, nblk - b0)
        z2 = jnp.dot(h, w_ref[:, b0 * LANES:(b0 + nb) * LANES], preferred_element_type=f32)
        for k in range(nb):
            emit(b0 + k, z2[:, k * LANES:(k + 1) * LANES])
    if not with_means:
        kcm_ref[...] = jnp.zeros(kcm_ref.shape, f32)


def _inproj(x2, g0, sc, sh, w, cos, sa, sb, *, tm, rows_per_mod, rope_rows, with_means, layer=0, depth=None,
            stacked=None):
    m = x2.shape[0]
    nt = m // tm
    mod_rows = sc.shape[1]
    rope_blocks = rope_rows // tm
    kc_rows = max(tm // CMP_BLOCK, 8)
    modmap = lambda i: ((i * tm) // rows_per_mod, 0, 0)
    ropemap = lambda i: (i % rope_blocks, 0)
    row = lambda w_: pl.BlockSpec((tm, w_), lambda i: (i, 0))
    state_t = depth is not None
    if state_t:
        seqs = m // rope_rows
        state_shape = lambda w_: (depth, seqs, w_, rope_rows)
        state_spec = lambda w_: pl.BlockSpec((1, 1, w_, tm), lambda i: (layer, i // rope_blocks, 0, i % rope_blocks))
    else:
        state_shape = lambda w_: (m, w_)
        state_spec = row
    outs = [((m, 384), bf16), ((m, 384), bf16), (state_shape(768), f32), (state_shape(512), f32), ((m, 256), f32),
            ((m, 256), f32), ((m, LANES), f32), ((m, 768), bf16), ((m, 512), bf16), ((m, 256), bf16)]
    out_shape = [jax.ShapeDtypeStruct(s, d) for s, d in outs] + [jax.ShapeDtypeStruct((nt * kc_rows, 256), f32)]
    out_specs = ([row(s[-1]) for s, _ in outs[:2]] + [state_spec(768), state_spec(512)]
                 + [row(s[-1]) for s, _ in outs[4:]] + [pl.BlockSpec((kc_rows, 256), lambda i: (i, 0))])
    in_specs = [row(D_MODEL),
                pl.BlockSpec((1, D_MODEL), lambda i: (0, 0)),
                pl.BlockSpec((1, mod_rows, D_MODEL), modmap),
                pl.BlockSpec((1, mod_rows, D_MODEL), modmap),
                pl.BlockSpec((D_MODEL, N_PROJ), lambda i: (0, 0)),
                pl.BlockSpec((tm, LANES), ropemap),
                pl.BlockSpec((tm, LANES), ropemap),
                pl.BlockSpec((tm, LANES), ropemap)]
    args = [x2, g0, sc, sh, w, cos, sa, sb]
    aliases = {}
    if stacked is not None:
        aliases = {len(args): 2, len(args) + 1: 3}
        in_specs += [pl.BlockSpec(memory_space=pl.ANY)] * 2
        args += list(stacked)
    return pl.pallas_call(
        functools.partial(_inproj_kernel, tm=tm, with_means=with_means, state_t=state_t),
        grid=(nt,),
        in_specs=in_specs,
        out_specs=out_specs,
        out_shape=out_shape,
        input_output_aliases=aliases,
        compiler_params=_cp(("parallel",)),
        name="in_proj",
    )(*args)


def _foxprep_kernel(misc_ref, bias_ref, lf_ref, ccol_ref, crow_ref, *, t):
    blk = 256
    lane = lax.broadcasted_iota(i32, (t, LANES), 1)
    lf = jnp.where(lane < H_FOX, _log_sigmoid(misc_ref[...] + bias_ref[...]), 0.0)
    lf_ref[...] = lf[:, :H_FOX]
    r = lax.broadcasted_iota(i32, (blk, blk), 0)
    c = lax.broadcasted_iota(i32, (blk, blk), 1)
    ltri = (r >= c).astype(f32)
    carry = jnp.zeros((1, LANES), f32)
    for b in range(t // blk):
        cb = jnp.dot(ltri, lf[b * blk:(b + 1) * blk], preferred_element_type=f32,
                     precision=lax.Precision.HIGHEST) + carry
        ccol_ref[b * blk:(b + 1) * blk, :] = cb
        carry = cb[blk - 1:blk, :]
    er = lax.broadcasted_iota(i32, (8, LANES), 0)
    ec = lax.broadcasted_iota(i32, (8, LANES), 1)
    eye = (er == ec).astype(f32)
    crow_ref[0] = lax.dot_general(eye, ccol_ref[...], (((1,), (1,)), ((), ())),
                                  preferred_element_type=f32, precision=lax.Precision.HIGHEST)


def _foxprep(misc, bias_row, b, t):
    return pl.pallas_call(
        functools.partial(_foxprep_kernel, t=t),
        grid=(b,),
        in_specs=[pl.BlockSpec((t, LANES), lambda i: (i, 0)),
                  pl.BlockSpec((1, LANES), lambda i: (0, 0))],
        out_specs=[pl.BlockSpec((t, H_FOX), lambda i: (i, 0)),
                   pl.BlockSpec((t, LANES), lambda i: (i, 0)),
                   pl.BlockSpec((1, 8, t), lambda i: (i, 0, 0))],
        out_shape=[jax.ShapeDtypeStruct((b * t, H_FOX), f32),
                   jax.ShapeDtypeStruct((b * t, LANES), f32),
                   jax.ShapeDtypeStruct((b, 8, t), f32)],
        compiler_params=_cp(("parallel",)),
        name="fox_prep",
    )(misc, bias_row)


def _tn(a, b):
    return lax.dot_general(a, b, (((0,), (0,)), ((), ())), preferred_element_type=f32)


def _softmax_step_t(carry, st, vb):
    m, l, acc = carry
    m_new = jnp.maximum(m, jnp.max(st, axis=0, keepdims=True))
    alpha = jnp.exp(m - m_new)
    p = jnp.exp(st - m_new)
    l = alpha * l + jnp.sum(p, axis=0, keepdims=True)
    acc = alpha * acc + _tn(vb, p.astype(bf16))
    return m_new, l, acc


def _fox_kernel(q_ref, kv_ref, ccol_ref, crow_ref, o_ref, *, tq, tk):
    qi = pl.program_id(1)
    q0 = pl.multiple_of(qi * tq, tq)
    lo = lax.broadcasted_iota(i32, (tq, LANES), 1) < HALF
    top = lax.broadcasted_iota(i32, (LANES, tq), 0) < HALF
    kk = lax.broadcasted_iota(i32, (tk, tq), 0)
    qq = lax.broadcasted_iota(i32, (tk, tq), 1)
    diag = kk <= qq
    diag2 = jnp.concatenate([diag, diag], axis=1)
    zero = jnp.zeros((), bf16)
    for p in range(H_FOX // 2):
        q = q_ref[:, p * LANES:(p + 1) * LANES]
        qst = jnp.concatenate([jnp.where(lo, q, zero), jnp.where(lo, zero, q)], axis=0)
        cq0 = crow_ref[0, 2 * p:2 * p + 1, pl.ds(q0, tq)]
        cq1 = crow_ref[0, 2 * p + 1:2 * p + 2, pl.ds(q0, tq)]

        def chunk(c, carry, mask, p=p, qst=qst, cq0=cq0, cq1=cq1):
            k0 = pl.multiple_of(c * tk, tk)
            kb = kv_ref[pl.ds(k0, tk), p * LANES:(p + 1) * LANES]
            vb = kv_ref[pl.ds(k0, tk), 384 + p * LANES:384 + (p + 1) * LANES]
            ck0 = ccol_ref[pl.ds(k0, tk), 2 * p:2 * p + 1]
            ck1 = ccol_ref[pl.ds(k0, tk), 2 * p + 1:2 * p + 2]
            st = _nt(kb, qst) + jnp.concatenate([cq0 - ck0, cq1 - ck1], axis=1)
            if mask is not None:
                st = jnp.where(mask, st, NEG)
            return _softmax_step_t(carry, st, vb)

        init = (jnp.full((1, 2 * tq), NEG, f32), jnp.zeros((1, 2 * tq), f32), jnp.zeros((LANES, 2 * tq), f32))
        carry = lax.fori_loop(0, qi, lambda c, cr: chunk(c, cr, None), init)
        _, l, acc = chunk(qi, carry, diag2)
        ot = acc / l
        o_ref[:, p * LANES:(p + 1) * LANES] = jnp.where(top, ot[:, :tq], ot[:, tq:]).T.astype(bf16)


def _fox_attn(fq, fkvb, ccol, crow, b, t):
    tq, tk = FOX_TQ, FOX_TK
    assert tq == tk
    nq = t // tq
    return pl.pallas_call(
        functools.partial(_fox_kernel, tq=tq, tk=tk),
        grid=(b, nq),
        in_specs=[pl.BlockSpec((tq, 384), lambda i, j: (i * nq + j, 0)),
                  pl.BlockSpec((t, 768), lambda i, j: (i, 0)),
                  pl.BlockSpec((t, LANES), lambda i, j: (i, 0)),
                  pl.BlockSpec((1, 8, t), lambda i, j: (i, 0, 0))],
        out_specs=pl.BlockSpec((tq, 384), lambda i, j: (i * nq + j, 0)),
        out_shape=jax.ShapeDtypeStruct((b * t, 384), bf16),
        compiler_params=_cp(("parallel", "parallel")),
        name="fox_attn",
    )(fq, fkvb, ccol, crow)


def _select_topk(score, nsel, axis=1):
    nb = score.shape[axis]
    jf = lax.broadcasted_iota(i32, score.shape, axis).astype(f32)
    sel = jnp.zeros(score.shape, f32)
    firsts = []
    for _ in range(nsel):
        mx = jnp.max(score, axis=axis, keepdims=True)
        first = jnp.min(jnp.where(score == mx, jf, float(nb)), axis=axis, keepdims=True)
        hit = jf == first
        sel = jnp.where(hit, 1.0, sel)
        score = jnp.where(hit, -3.0, score)
        firsts.append(first)
    return sel, firsts


def _nsa_kernel(q_ref, kcm_ref, nkv_ref, win_ref, misc_ref, o_ref, *, tq, tk, t):
    qi = pl.program_id(1)
    s0 = qi * tq
    nbc = t // CMP_BLOCK
    lane = lax.broadcasted_iota(i32, (tq, LANES), 1)
    lo = lane < HALF
    zero = jnp.zeros((), bf16)
    kpos = lax.broadcasted_iota(i32, (tk, tq), 0)
    qpos = lax.broadcasted_iota(i32, (tk, tq), 1) + s0
    qpos_b = lax.broadcasted_iota(i32, (nbc, tq), 1) + s0
    jb = lax.broadcasted_iota(i32, (nbc, tq), 0)
    gates = jax.nn.sigmoid(misc_ref[...].T)

    def tile3(a):
        return jnp.concatenate([a, a, a], axis=1)

    def per_head(ot):
        return [ot[:, g * tq:(g + 1) * tq] for g in range(G_NSA)]

    def attend(qk, kref, klane, vlane, c_lo, c_hi, biasfn):
        def body(c, carry, last):
            k0 = pl.multiple_of(c * tk, tk)
            kb = kref[pl.ds(k0, tk), klane:klane + LANES]
            vb = kref[pl.ds(k0, tk), vlane:vlane + LANES]
            return _softmax_step_t(carry, _nt(kb, qk) + tile3(biasfn(k0, last)), vb)
        init = (jnp.full((1, 3 * tq), NEG, f32), jnp.zeros((1, 3 * tq), f32), jnp.zeros((LANES, 3 * tq), f32))
        carry = lax.fori_loop(c_lo, c_hi - 1, lambda c, cr: body(c, cr, False), init)
        _, l, acc = body(c_hi - 1, carry, True)
        return per_head(acc / l)

    kcm = kcm_ref[...]
    kc = kcm[:, :LANES].astype(bf16)
    vc = kcm[:, LANES:].astype(bf16)
    outs = []
    for kv in range(KV_NSA):
        keep = lo if kv == 0 else jnp.logical_not(lo)
        qk = jnp.concatenate([jnp.where(keep, q_ref[:, p * LANES:(p + 1) * LANES], zero)
                              for p in range(G_NSA)], axis=0)
        cmask3 = tile3(((jb + 1) * CMP_BLOCK - 1) <= qpos_b)
        sc = jnp.where(cmask3, _nt(kc, qk), NEG)
        mx = jnp.max(sc, axis=0, keepdims=True)
        pe = jnp.where(cmask3, jnp.exp(sc - mx), 0.0)
        den = jnp.sum(pe, axis=0, keepdims=True)
        pc = pe / jnp.where(den > 0.0, den, 1.0)
        o_cmp = per_head(_tn(vc, pc.astype(bf16)))
        imp = pc[:, :tq] + pc[:, tq:2 * tq] + pc[:, 2 * tq:]
        cur = qpos_b // SEL_BLOCK
        forced = (jb == 0) | (jb == cur) | (jb == cur - 1)
        score = jnp.where(forced, FORCE, imp)
        score = jnp.where(jb * SEL_BLOCK <= qpos_b, score, -1.0)
        sel, _ = _select_topk(score, min(TOPK_BLOCKS, nbc), axis=0)
        selt = jnp.concatenate([sel, jnp.zeros((LANES - nbc, tq), f32)], axis=0).astype(bf16)

        def sel_bias(k0, last, selt=selt):
            er = (lax.broadcasted_iota(i32, (tk, LANES), 0) + k0) // SEL_BLOCK
            ec = lax.broadcasted_iota(i32, (tk, LANES), 1)
            expand = jnp.where(er == ec, 1.0, 0.0).astype(bf16)
            picked = jnp.dot(expand, selt, preferred_element_type=f32)
            bias = (picked - 1.0) * (-NEG)
            return jnp.where((kpos + k0) <= qpos, bias, NEG) if last else bias

        def win_bias(k0, last):
            rel = qpos - (kpos + k0)
            return jnp.where(rel >= 0, jnp.where(rel < WINDOW, 0.0, NEG), NEG)

        o_sel = attend(qk, nkv_ref, 2 * LANES, 3 * LANES, 0, (s0 + tq + tk - 1) // tk, sel_bias)
        w_lo = jnp.maximum(s0 - WINDOW + 1, 0) // tk
        o_win = attend(qk, win_ref, 0, LANES, w_lo, (s0 + tq + tk - 1) // tk, win_bias)
        for g in range(G_NSA):
            r = GATE0 + 3 * (kv * G_NSA + g)
            outs.append(gates[r:r + 1] * o_cmp[g] + gates[r + 1:r + 2] * o_sel[g] + gates[r + 2:r + 3] * o_win[g])
    top = lax.broadcasted_iota(i32, (LANES, tq), 0) < HALF
    for p in range(G_NSA):
        o_ref[:, p * LANES:(p + 1) * LANES] = jnp.where(top, outs[p], outs[G_NSA + p]).T.astype(bf16)


def _nsa_attn(nq, kcm, nkvb, winb, misc, b, t):
    tq, tk = NSA_TQ, NSA_TK
    assert tk % tq == 0
    nq_t = t // tq
    nbc = t // CMP_BLOCK
    return pl.pallas_call(
        functools.partial(_nsa_kernel, tq=tq, tk=tk, t=t),
        grid=(b, nq_t),
        in_specs=[pl.BlockSpec((tq, 384), lambda i, j: (i * nq_t + j, 0)),
                  pl.BlockSpec((nbc, 256), lambda i, j: (i, 0)),
                  pl.BlockSpec((t, 512), lambda i, j: (i, 0)),
                  pl.BlockSpec((t, 256), lambda i, j: (i, 0)),
                  pl.BlockSpec((tq, LANES), lambda i, j: (i * nq_t + j, 0))],
        out_specs=pl.BlockSpec((tq, 384), lambda i, j: (i * nq_t + j, 0)),
        out_shape=jax.ShapeDtypeStruct((b * t, 384), bf16),
        compiler_params=_cp(("parallel", "parallel")),
        name="nsa_attn",
    )(nq, kcm, nkvb, winb, misc)


def _pool_project(d, wp_ref, ps_ref):
    return (jnp.dot(d.astype(bf16), wp_ref[...], preferred_element_type=f32) * ps_ref[...]).astype(bf16)


def _out_tail(x, ofox, onsa, opool, wo_ref, g_ref, gate):
    o = (jnp.dot(ofox, wo_ref[0:384, :], preferred_element_type=f32)
         + jnp.dot(onsa, wo_ref[384:768, :], preferred_element_type=f32)
         + jnp.dot(opool, wo_ref[768:1024, :], preferred_element_type=f32))
    return x + gate * _rms(o, g_ref[...])


def _outproj_kernel(x_ref, ofox_ref, onsa_ref, u_ref, halo_ref, wo_ref, wp_ref, ps_ref, g_ref, gate_ref,
                    o_ref, ext_ref, *, tm, t):
    i = pl.program_id(0)
    pos0 = (i * tm) % t
    u = u_ref[...]
    ext_ref[0:16, :] = jnp.where(pos0 > 0, halo_ref[...], 0.0)
    ext_ref[16:, :] = u
    pos = lax.broadcasted_iota(i32, (tm, LANES), 0) + pos0
    lo = lax.broadcasted_iota(i32, (tm, LANES), 1) < HALF

    def shifted(k, c):
        return ext_ref[16 - k:16 - k + tm, c * LANES:(c + 1) * LANES]

    def cnt(w):
        return jnp.minimum(w, pos + 1).astype(f32)

    ds = []
    for c, (wa, wb) in enumerate(((POOL_WINDOWS[0], POOL_WINDOWS[1]), (POOL_WINDOWS[2], POOL_WINDOWS[3]))):
        run = shifted(0, c)
        sums = {}
        for k in range(1, wb):
            run = run + shifted(k, c)
            if k + 1 in (wa, wb):
                sums[k + 1] = run
        ds.append(jnp.where(lo, sums[wa] / cnt(wa), sums[wb] / cnt(wb)) - u[:, c * LANES:(c + 1) * LANES])
    opool = _pool_project(jnp.concatenate(ds, axis=1), wp_ref, ps_ref)
    o_ref[...] = _out_tail(x_ref[...], ofox_ref[...], onsa_ref[...], opool, wo_ref, g_ref, gate_ref[0])


def _outproj(x2, ofox, onsa, u, wo, wp, ps, g1, gate, *, tm, t):
    m = x2.shape[0]
    row = lambda w_: pl.BlockSpec((tm, w_), lambda i: (i, 0))
    const = lambda s: pl.BlockSpec(s, lambda i: (0, 0))
    return pl.pallas_call(
        functools.partial(_outproj_kernel, tm=tm, t=t),
        grid=(m // tm,),
        in_specs=[row(D_MODEL), row(384), row(384), row(256),
                  pl.BlockSpec((16, 256), lambda i: (jnp.maximum(i * (tm // 16) - 1, 0), 0)),
                  const((D_MODEL, D_MODEL)), const((256, 256)), const((1, 256)), const((1, D_MODEL)),
                  pl.BlockSpec((1, 1, D_MODEL), lambda i: ((i * tm) // t, 0, 0))],
        out_specs=row(D_MODEL),
        out_shape=jax.ShapeDtypeStruct((m, D_MODEL), f32),
        scratch_shapes=[pltpu.VMEM((tm + 16, 256), f32)],
        compiler_params=_cp(("parallel",)),
        name="out_proj",
    )(x2, ofox, onsa, u, u, wo, wp, ps, g1, gate)


def _outproj_dec_kernel(x_ref, ofox_ref, onsa_ref, ext_ref, wo_ref, wp_ref, ps_ref, g_ref, gate_ref, o_ref, *, past):
    ext = ext_ref[...]
    n = ext.shape[1]
    r = lax.broadcasted_iota(i32, ext.shape, 1)
    lane = lax.broadcasted_iota(i32, (ext.shape[0], C_POOL), 1)
    u_new = jnp.sum(jnp.where(r == n - 1, ext, 0.0), axis=1)
    d = jnp.zeros(u_new.shape, f32)
    for g, w in enumerate(POOL_WINDOWS):
        sw = jnp.sum(jnp.where(r >= n - w, ext, 0.0), axis=1)
        d = jnp.where(lane // POOL_GW == g, sw / float(min(w, past + 1)), d)
    opool = _pool_project(d - u_new, wp_ref, ps_ref)
    o_ref[...] = _out_tail(x_ref[...], ofox_ref[...], onsa_ref[...], opool, wo_ref, g_ref, gate_ref[...])


def _outproj_dec(x2, ofox, onsa, ext, wo, wp, ps, g1, gate, past):
    m = x2.shape[0]
    full = lambda a: pl.BlockSpec(a.shape, lambda i: (0,) * a.ndim)
    args = (x2, ofox, onsa, ext, wo, wp, ps, g1, gate)
    return pl.pallas_call(
        functools.partial(_outproj_dec_kernel, past=past),
        grid=(1,),
        in_specs=[full(a) for a in args],
        out_specs=pl.BlockSpec((m, D_MODEL), lambda i: (0, 0)),
        out_shape=jax.ShapeDtypeStruct((m, D_MODEL), f32),
        compiler_params=_cp(("arbitrary",)),
        name="out_proj_dec",
    )(*args)


def _mlp_kernel(x_ref, g2_ref, sc_ref, sh_ref, w1_ref, w2_ref, g3_ref, gate_ref, o_ref, *, tf):
    x = x_ref[...]
    h = (_rms(x, g2_ref[...]) * (1.0 + sc_ref[0]) + sh_ref[0]).astype(bf16)
    acc = jnp.zeros(x.shape, f32)
    for c in range(D_FF // tf):
        a = jnp.maximum(jnp.dot(h, w1_ref[:, c * tf:(c + 1) * tf], preferred_element_type=f32), 0.0)
        acc = acc + jnp.dot((a * a).astype(bf16), w2_ref[c * tf:(c + 1) * tf, :], preferred_element_type=f32)
    o_ref[...] = x + gate_ref[0] * _rms(acc, g3_ref[...])


def _mlp(x2, g2, sc, sh, w1, w2, g3, gate, *, tm, rows_per_mod):
    m = x2.shape[0]
    mod_rows = sc.shape[1]
    modmap = lambda i: ((i * tm) // rows_per_mod, 0, 0)
    const = lambda s: pl.BlockSpec(s, lambda i: (0, 0))
    mod = pl.BlockSpec((1, mod_rows, D_MODEL), modmap)
    return pl.pallas_call(
        functools.partial(_mlp_kernel, tf=512),
        grid=(m // tm,),
        in_specs=[pl.BlockSpec((tm, D_MODEL), lambda i: (i, 0)), const((1, D_MODEL)), mod, mod,
                  const((D_MODEL, D_FF)), const((D_FF, D_MODEL)), const((1, D_MODEL)), mod],
        out_specs=pl.BlockSpec((tm, D_MODEL), lambda i: (i, 0)),
        out_shape=jax.ShapeDtypeStruct((m, D_MODEL), f32),
        compiler_params=_cp(("parallel",)),
        name="mlp",
    )(x2, g2, sc, sh, w1, w2, g3, gate)


def _pad8(a):
    return jnp.concatenate([a, jnp.zeros((8 - a.shape[0], a.shape[1]), a.dtype)], axis=0)


def _by_group(rows):
    grp = lax.broadcasted_iota(i32, (8, rows.shape[1]), 0) // G_NSA
    return jnp.where(grp == 0, jnp.broadcast_to(rows[0:1], (8, rows.shape[1])),
                     jnp.broadcast_to(rows[1:2], (8, rows.shape[1])))


def _lane_to_sublane(row, offset, stride):
    sub = lax.broadcasted_iota(i32, (8, LANES), 0)
    lane = lax.broadcasted_iota(i32, (8, LANES), 1)
    return jnp.sum(jnp.where(lane == offset + stride * sub, jnp.broadcast_to(row, (8, LANES)), 0.0),
                   axis=1, keepdims=True)


def _fox_dec_kernel(pt_ref, fq_ref, misc_ref, bias_ref, knew_ref, *rest, pps):
    lf_refs = rest[:pps]
    kv_refs = rest[pps:2 * pps]
    o_ref, lfo_ref, m_scr, l_scr, acc_scr, c_scr = rest[2 * pps:]
    g = pl.program_id(1)
    n = pps * PAGE_SIZE
    q6f = _pad8(fq_ref[0])
    q6 = q6f.astype(bf16)
    lane = lax.broadcasted_iota(i32, (8, LANES), 1)

    @pl.when(g == 0)
    def _():
        lfrow = jnp.where(lane[0:1] < H_FOX, _log_sigmoid(misc_ref[0] + bias_ref[...]), 0.0)
        lfo_ref[0] = lfrow
        m_scr[...] = jnp.sum(q6f * _pad8(knew_ref[0, 0]), axis=1, keepdims=True)
        l_scr[...] = jnp.ones((8, 1), f32)
        acc_scr[...] = _pad8(knew_ref[0, 1])
        c_scr[...] = _lane_to_sublane(lfrow, 0, 1)

    carry = c_scr[...]
    biases = []
    n_pages = pl.num_programs(1) * pps
    for i in range(pps):
        row = pt_ref[pl.program_id(0), n_pages - 1 - (g * pps + i)] % 8
        lf = _pad8(jnp.concatenate([lf_refs[i][0, h, 0, pl.ds(row, 1), :] for h in range(H_FOX)], axis=0))
        suf = lf
        for sft in (1, 2, 4, 8, 16, 32, 64):
            suf = suf + jnp.where(lane + sft < LANES, pltpu.roll(suf, LANES - sft, 1), 0.0)
        biases.append(carry + (suf - lf))
        carry = carry + suf[:, 0:1]
    c_scr[...] = carry
    s = jnp.concatenate(biases, axis=1)
    sub = lax.broadcasted_iota(i32, (8, n), 0)
    for h in range(H_FOX):
        kht = jnp.concatenate([kv_refs[i][0, 0, 0, h] for i in range(pps)], axis=1).astype(bf16)
        s = jnp.where(sub == h, s + jnp.dot(q6, kht, preferred_element_type=f32), s)
    m = m_scr[...]
    m_new = jnp.maximum(m, jnp.max(s, axis=1, keepdims=True))
    alpha = jnp.exp(m - m_new)
    p = jnp.exp(s - m_new)
    l_scr[...] = alpha * l_scr[...] + jnp.sum(p, axis=1, keepdims=True)
    m_scr[...] = m_new
    pb = p.astype(bf16)
    sub_d = lax.broadcasted_iota(i32, (8, HEAD_DIM), 0)
    pv = jnp.zeros((8, HEAD_DIM), f32)
    for h in range(H_FOX):
        vht = jnp.concatenate([kv_refs[i][0, 0, 1, h] for i in range(pps)], axis=1).astype(bf16)
        pv = jnp.where(sub_d == h, _nt(pb, vht), pv)
    acc_scr[...] = alpha * acc_scr[...] + pv

    @pl.when(g == pl.num_programs(1) - 1)
    def _():
        o_ref[0] = acc_scr[...] / l_scr[...]


def _fox_decode(layer, page_table, fq, misc, bias_row, fkv_new, lf_t, cache_kv):
    b, n_pages = page_table.shape
    pps = FOX_PPS
    ng = n_pages // pps
    row3 = lambda w_: pl.BlockSpec((1, 1, w_), lambda i, g, pt: (i, 0, 0))

    def page_id(bi, g, pt, i):
        return pt[jnp.minimum(bi, b - 1), n_pages - 1 - (jnp.minimum(g, ng - 1) * pps + i)]

    def page(i, shape):
        zeros = (0,) * len(shape)
        return pl.BlockSpec((1, 1) + shape, lambda bi, g, pt, i=i: (layer, page_id(bi, g, pt, i)) + zeros)

    def lf_group(i):
        return pl.BlockSpec((1, H_FOX, 1, 8, PAGE_SIZE),
                            lambda bi, g, pt, i=i: (layer, 0, page_id(bi, g, pt, i) // 8, 0, 0))

    grid_spec = pltpu.PrefetchScalarGridSpec(
        num_scalar_prefetch=1,
        grid=(b, ng),
        in_specs=[pl.BlockSpec((1, H_FOX, HEAD_DIM), lambda i, g, pt: (i, 0, 0)), row3(LANES),
                  pl.BlockSpec((1, LANES), lambda i, g, pt: (0, 0)),
                  pl.BlockSpec((1, 2, H_FOX, HEAD_DIM), lambda i, g, pt: (i, 0, 0, 0))]
                 + [lf_group(i) for i in range(pps)]
                 + [page(i, (2, H_FOX, HEAD_DIM, PAGE_SIZE)) for i in range(pps)],
        out_specs=[pl.BlockSpec((1, 8, HEAD_DIM), lambda i, g, pt: (i, 0, 0)), row3(LANES)],
        scratch_shapes=[pltpu.VMEM((8, 1), f32), pltpu.VMEM((8, 1), f32), pltpu.VMEM((8, HEAD_DIM), f32),
                        pltpu.VMEM((8, 1), f32)],
    )
    return pl.pallas_call(
        functools.partial(_fox_dec_kernel, pps=pps),
        grid_spec=grid_spec,
        out_shape=[jax.ShapeDtypeStruct((b, 8, HEAD_DIM), f32), jax.ShapeDtypeStruct((b, 1, LANES), f32)],
        compiler_params=_cp(("parallel", "arbitrary")),
        name="fox_decode",
    )(page_table, fq, misc, bias_row, fkv_new, *([lf_t] * pps), *([cache_kv] * pps))


def _nsa_cmp_dec_kernel(pt_ref, q_ref, *rest, pps, past):
    pg_refs = rest[:pps]
    p_ref, o_ref, mean_scr = rest[pps:]
    g = pl.program_id(1)
    per_page = PAGE_SIZE // CMP_BLOCK
    rows = 2 * KV_NSA * HEAD_DIM
    x = jnp.concatenate([pg_refs[i][0, 0].reshape(rows, PAGE_SIZE) for i in range(pps)], axis=1)
    r = lax.broadcasted_iota(i32, (pps * PAGE_SIZE, LANES), 0)
    c = lax.broadcasted_iota(i32, (pps * PAGE_SIZE, LANES), 1)
    avg = jnp.where(r // CMP_BLOCK == c, 1.0 / CMP_BLOCK, 0.0).astype(bf16)
    hi = x.astype(bf16)
    lo = (x - hi.astype(f32)).astype(bf16)
    mean_scr[g] = jnp.dot(hi, avg, preferred_element_type=f32) + jnp.dot(lo, avg, preferred_element_type=f32)

    @pl.when(g == pl.num_programs(1) - 1)
    def _():
        ng = mean_scr.shape[0]
        nbc = ng * pps * per_page
        mean = mean_scr[0]
        for gg in range(1, ng):
            mean = mean + pltpu.roll(mean_scr[gg], gg * pps * per_page, 1)
        mean = mean[:, :nbc].astype(bf16)
        part = lambda feat, kv: mean[(feat * KV_NSA + kv) * HEAD_DIM:(feat * KV_NSA + kv + 1) * HEAD_DIM]
        q6 = _pad8(q_ref[0]).astype(bf16)
        grp0 = lax.broadcasted_iota(i32, (8, 1), 0) // G_NSA == 0
        s = jnp.where(grp0, jnp.dot(q6, part(0, 0), preferred_element_type=f32),
                      jnp.dot(q6, part(0, 1), preferred_element_type=f32))
        j = lax.broadcasted_iota(i32, (8, nbc), 1)
        mask = ((j + 1) * CMP_BLOCK - 1) <= past
        s = jnp.where(mask, s, NEG)
        pe = jnp.where(mask, jnp.exp(s - jnp.max(s, axis=1, keepdims=True)), 0.0)
        den = jnp.sum(pe, axis=1, keepdims=True)
        p = pe / jnp.where(den > 0.0, den, 1.0)
        p_ref[0] = p
        pb = p.astype(bf16)
        o_ref[0] = jnp.where(grp0, _nt(pb, part(1, 0)), _nt(pb, part(1, 1)))


def _nsa_cmp_decode(layer, page_table, nq, cache_nsa, past):
    b, n_pages = page_table.shape
    pps = CMP_PPS
    nbc = past // CMP_BLOCK
    grid_spec = pltpu.PrefetchScalarGridSpec(
        num_scalar_prefetch=1,
        grid=(b, n_pages // pps),
        in_specs=[pl.BlockSpec((1, H_NSA, HEAD_DIM), lambda i, g, pt: (i, 0, 0))]
                 + [pl.BlockSpec((1, 1, 2, KV_NSA, HEAD_DIM, PAGE_SIZE),
                                 lambda bi, g, pt, i=i: (layer, pt[jnp.minimum(bi, b - 1),
                                                                   jnp.minimum(g, n_pages // pps - 1) * pps + i],
                                                         0, 0, 0, 0))
                    for i in range(pps)],
        out_specs=[pl.BlockSpec((1, 8, nbc), lambda i, g, pt: (i, 0, 0)),
                   pl.BlockSpec((1, 8, HEAD_DIM), lambda i, g, pt: (i, 0, 0))],
        scratch_shapes=[pltpu.VMEM((n_pages // pps, 2 * KV_NSA * HEAD_DIM, LANES), f32)],
    )
    return pl.pallas_call(
        functools.partial(_nsa_cmp_dec_kernel, pps=pps, past=past),
        grid_spec=grid_spec,
        out_shape=[jax.ShapeDtypeStruct((b, 8, nbc), f32), jax.ShapeDtypeStruct((b, 8, HEAD_DIM), f32)],
        compiler_params=_cp(("parallel", "arbitrary")),
        name="nsa_cmp_decode",
    )(page_table, nq, *([cache_nsa] * pps))


def _nsa_topk_dec_kernel(p_ref, idx_ref, *, past, width):
    b = p_ref.shape[0]
    nbc = p_ref.shape[2]
    nbs = -(-(past + 1) // SEL_BLOCK)
    imps = []
    for kv in range(KV_NSA):
        imps.append(p_ref[:, kv * G_NSA, :] + p_ref[:, kv * G_NSA + 1, :] + p_ref[:, kv * G_NSA + 2, :])
    imp = jnp.concatenate(imps, axis=0)
    imp = jnp.concatenate([imp, jnp.zeros((2 * b, width - nbc), f32)], axis=1)
    j = lax.broadcasted_iota(i32, (2 * b, width), 1)
    cur = past // SEL_BLOCK
    forced = (j == 0) | (j == cur) | (j == cur - 1)
    score = jnp.where(forced, FORCE, imp)
    score = jnp.where(j * SEL_BLOCK <= past, score, -1.0)
    score = jnp.where(j < nbs, score, -2.0)
    _, firsts = _select_topk(score, min(TOPK_BLOCKS, nbs))
    lane = lax.broadcasted_iota(i32, (2 * b, LANES), 1)
    out = jnp.zeros((2 * b, LANES), i32)
    for k, first in enumerate(firsts):
        out = jnp.where(lane == k, first.astype(i32), out)
    idx_ref[...] = out


def _nsa_topk_decode(p8, past):
    b, _, nbc = p8.shape
    width = 2 * nbc
    return pl.pallas_call(
        functools.partial(_nsa_topk_dec_kernel, past=past, width=width),
        grid=(1,),
        in_specs=[pl.BlockSpec(p8.shape, lambda i: (0, 0, 0))],
        out_specs=pl.BlockSpec((2 * b, LANES), lambda i: (0, 0)),
        out_shape=jax.ShapeDtypeStruct((2 * b, LANES), i32),
        compiler_params=_cp(("arbitrary",)),
        name="nsa_topk_decode",
    )(p8)


def _nsa_sel_dec_kernel(pt_ref, idx_ref, q_ref, misc_ref, ocmp_ref, nkv_ref, wnew_ref, win_ref, *rest, past, nsel):
    blk_refs = rest[:KV_NSA * nsel]
    o_ref = rest[KV_NSA * nsel]
    b = pl.program_id(0)
    n_past_blk = past // SEL_BLOCK
    q8f = _pad8(q_ref[0])
    q8 = q8f.astype(bf16)
    sub = lax.broadcasted_iota(i32, (8, 1), 0)
    grp0 = sub // G_NSA == 0
    per_page = PAGE_SIZE // SEL_BLOCK
    lane_blk = lax.broadcasted_iota(i32, (8, PAGE_SIZE), 1) // SEL_BLOCK
    m = jnp.full((8, 1), NEG, f32)
    l = jnp.zeros((8, 1), f32)
    acc = jnp.zeros((8, HEAD_DIM), f32)
    s_new = jnp.sum(q8f * _by_group(nkv_ref[0, 2]), axis=1, keepdims=True)
    v_new = _by_group(nkv_ref[0, 3])
    for kv in range(KV_NSA):
        in_group = (sub // G_NSA) == kv
        has_new = jnp.zeros((), jnp.bool_)
        for k in range(nsel):
            j = idx_ref[kv * pl.num_programs(0) + b, k]
            has_new = has_new | (j == n_past_blk)
            blk = blk_refs[kv * nsel + k]
            mask = in_group & (j < n_past_blk) & (lane_blk == j % per_page)
            s = jnp.where(mask, jnp.dot(q8, blk[0, 0, 0, 0].astype(bf16), preferred_element_type=f32), NEG)
            m_new = jnp.maximum(m, jnp.max(s, axis=1, keepdims=True))
            alpha = jnp.exp(m - m_new)
            pe = jnp.where(mask, jnp.exp(s - m_new), 0.0)
            l = alpha * l + jnp.sum(pe, axis=1, keepdims=True)
            acc = alpha * acc + _nt(pe.astype(bf16), blk[0, 0, 1, 0].astype(bf16))
            m = m_new
        new_ok = in_group & has_new
        m_new = jnp.maximum(m, jnp.where(new_ok, s_new, NEG))
        alpha = jnp.exp(m - m_new)
        p_new = jnp.where(new_ok, jnp.exp(s_new - m_new), 0.0)
        l = alpha * l + p_new
        acc = alpha * acc + p_new * v_new
        m = m_new
    o_sel = acc / jnp.where(l > 0.0, l, 1.0)
    wb = win_ref.shape[5]
    kpos = past - wb + lax.broadcasted_iota(i32, (8, wb), 1)
    rel = past - kpos
    wmask = (rel >= 0) & (rel < WINDOW) & (kpos >= 0)
    s = jnp.where(grp0, jnp.dot(q8, win_ref[0, 0, 0, 0].astype(bf16), preferred_element_type=f32),
                  jnp.dot(q8, win_ref[0, 0, 0, 1].astype(bf16), preferred_element_type=f32))
    s = jnp.where(wmask, s, NEG)
    s_w = jnp.sum(q8f * _by_group(wnew_ref[0, 0]), axis=1, keepdims=True)
    m = jnp.maximum(jnp.max(s, axis=1, keepdims=True), s_w)
    p = jnp.where(wmask, jnp.exp(s - m), 0.0)
    p_w = jnp.exp(s_w - m)
    pb = p.astype(bf16)
    pv = jnp.where(grp0, _nt(pb, win_ref[0, 0, 1, 0].astype(bf16)), _nt(pb, win_ref[0, 0, 1, 1].astype(bf16)))
    o_win = (pv + p_w * _by_group(wnew_ref[0, 1])) / (jnp.sum(p, axis=1, keepdims=True) + p_w)
    misc = misc_ref[0]
    gates = [jax.nn.sigmoid(_lane_to_sublane(misc, GATE0 + r, 3)) for r in range(3)]
    o_ref[0] = gates[0] * ocmp_ref[0] + gates[1] * o_sel + gates[2] * o_win


def _nsa_sel_decode(layer, page_table, idx, nq, misc, ocmp, nkv_new, win_new, cache_win, cache_nsa, past):
    b = page_table.shape[0]
    nsel = idx.shape[1]
    n_past_blk = past // SEL_BLOCK
    per_page = PAGE_SIZE // SEL_BLOCK
    wb = cache_win.shape[5]

    def blk_spec(kv, k):
        def imap(i, pt, ix):
            ii = jnp.minimum(i, b - 1)
            jp = jnp.clip(ix[kv * b + ii, k], 0, n_past_blk - 1)
            return (layer, pt[ii, jp // per_page], 1, kv, 0, 0)
        return pl.BlockSpec((1, 1, 2, 1, HEAD_DIM, PAGE_SIZE), imap)

    grid_spec = pltpu.PrefetchScalarGridSpec(
        num_scalar_prefetch=2,
        grid=(b,),
        in_specs=[pl.BlockSpec((1, H_NSA, HEAD_DIM), lambda i, pt, ix: (i, 0, 0)),
                  pl.BlockSpec((1, 1, LANES), lambda i, pt, ix: (i, 0, 0)),
                  pl.BlockSpec((1, 8, HEAD_DIM), lambda i, pt, ix: (i, 0, 0)),
                  pl.BlockSpec((1, 4, KV_NSA, HEAD_DIM), lambda i, pt, ix: (i, 0, 0, 0)),
                  pl.BlockSpec((1, 2, KV_NSA, HEAD_DIM), lambda i, pt, ix: (i, 0, 0, 0)),
                  pl.BlockSpec((1, 1, 2, KV_NSA, HEAD_DIM, wb), lambda i, pt, ix: (layer, i, 0, 0, 0, 0))]
                 + [blk_spec(kv, k) for kv in range(KV_NSA) for k in range(nsel)],
        out_specs=pl.BlockSpec((1, 8, HEAD_DIM), lambda i, pt, ix: (i, 0, 0)),
    )
    return pl.pallas_call(
        functools.partial(_nsa_sel_dec_kernel, past=past, nsel=nsel),
        grid_spec=grid_spec,
        out_shape=jax.ShapeDtypeStruct((b, 8, HEAD_DIM), f32),
        compiler_params=_cp(("parallel",)),
        name="nsa_sel_decode",
    )(page_table, idx, nq, misc, ocmp, nkv_new, win_new, cache_win, *([cache_nsa] * (KV_NSA * nsel)))


def _proj_columns():
    off = np.cumsum([0, 384, 384, 384, 6, 384, 128, 128, 128, 128, 128, 128, 18, 256])
    fq, fk, fv, ff, nq, nkc, nvc, nks, nvs, nkw, nvw, ng, u = [int(o) for o in off[:13]]
    cols = list(range(fq, fq + 384))
    for h in NSA_PAIR_ORDER:
        cols += list(range(nq + h * HEAD_DIM, nq + (h + 1) * HEAD_DIM))
    cols += list(range(fk, fk + 768))
    cols += list(range(nkc, nkc + 512))
    cols += list(range(nkw, nkw + 256))
    cols += list(range(u, u + 256))
    cols += list(range(ff, ff + 6)) + list(range(ng, ng + 18)) + [N_IN] * (LANES - 24)
    return np.asarray(cols, np.int32)


def _out_rows():
    rows = list(range(0, 384))
    for h in NSA_PAIR_ORDER:
        rows += list(range(384 + h * HEAD_DIM, 384 + (h + 1) * HEAD_DIM))
    rows += list(range(768, 1024))
    return np.asarray(rows, np.int32)


def _rope_tables(pos):
    half = ROT_DIM // 2
    inv = ROPE_THETA ** (-jnp.arange(0, ROT_DIM, 2, dtype=f32) / ROT_DIM)
    ang = pos.astype(f32)[:, None] * inv[None, :]
    cos, sin = jnp.cos(ang), jnp.sin(ang)
    n = pos.shape[0]
    one = jnp.ones((n, HEAD_DIM - ROT_DIM), f32)
    zero8 = jnp.zeros((n, half), f32)
    zrest = jnp.zeros((n, HEAD_DIM - ROT_DIM), f32)
    c = jnp.concatenate([cos, cos, one], axis=1)
    sa = jnp.concatenate([-sin, zero8, zrest], axis=1)
    sb = jnp.concatenate([zero8, sin, zrest], axis=1)
    return tuple(jnp.concatenate([a, a], axis=1) for a in (c, sa, sb))


def kernel(x_prompt, x_sample, cache_fox_kv, cache_fox_logf, cache_nsa_kv, cache_nsa_win, state_pool, page_table,
           c_prompt, c_sample, w_ada, b_ada, norm_g, w_in, b_fox_f, w_out, w_pool, pool_scale, w_ff1, w_ff2):
    depth = w_in.shape[0]
    bp, t, _ = x_prompt.shape
    bs = x_sample.shape[0]
    past = page_table.shape[1] * PAGE_SIZE
    mp = bp * t

    cols = _proj_columns()
    w_in_p = jnp.concatenate([w_in, jnp.zeros((depth, D_MODEL, 1), f32)], axis=2)[:, :, cols].astype(bf16)
    w_out_p = w_out[:, _out_rows(), :].astype(bf16)
    w_pool_bd = jnp.zeros((depth, C_POOL, C_POOL), f32)
    for g in range(len(POOL_WINDOWS)):
        w_pool_bd = w_pool_bd.at[:, g * POOL_GW:(g + 1) * POOL_GW, g * POOL_GW:(g + 1) * POOL_GW].set(w_pool[:, g])
    w_pool_bd = w_pool_bd.astype(bf16)
    w1 = w_ff1.astype(bf16)
    w2 = w_ff2.astype(bf16)
    bias_rows = jnp.pad(b_fox_f, ((0, 0), (0, LANES - H_FOX))).reshape(depth, 1, LANES)

    rope_p = _rope_tables(jnp.arange(t))
    rope_s = _rope_tables(jnp.full((bs,), past, i32))

    to_last = (0, 1, 3, 4, 5, 2)
    fox_t = jnp.transpose(cache_fox_kv, to_last)
    nsa_t = jnp.transpose(cache_nsa_kv, to_last)
    win_t = jnp.transpose(cache_nsa_win, to_last)
    n_pool = cache_fox_logf.shape[1]
    assert n_pool % 8 == 0
    lf_t = jnp.transpose(cache_fox_logf, (0, 3, 1, 2)).reshape(depth, H_FOX, n_pool // 8, 8, PAGE_SIZE)
    nat_from_pair = np.argsort(np.asarray(NSA_PAIR_ORDER))

    mod = _ada(jnp.concatenate([c_prompt, c_sample], axis=0), w_ada, b_ada)
    mod = mod.reshape(depth, bp + bs, 6, D_MODEL)

    yp = x_prompt.reshape(mp, D_MODEL)
    ys = x_sample.reshape(bs, D_MODEL)
    sp, ss = [], []
    kv_stacks = (jnp.zeros((depth, bp, 2 * H_FOX * HEAD_DIM, t), f32),
                 jnp.zeros((depth, bp, 4 * KV_NSA * HEAD_DIM, t), f32))
    for l in range(depth):
        g = norm_g[l].reshape(4, 1, D_MODEL)
        modp = [mod[l, :bp, k].reshape(bp, 1, D_MODEL) for k in range(6)]
        mods = [mod[l, bp:, k].reshape(1, bs, D_MODEL) for k in range(6)]

        (fq, nq, fkv, nkv, win, u, misc, fkvb, nkvb, winb, kcm) = _inproj(
            yp, g[0], modp[1], modp[0], w_in_p[l], *rope_p, tm=512, rows_per_mod=t, rope_rows=t, with_means=True,
            layer=l, depth=depth, stacked=kv_stacks)
        kv_stacks = (fkv, nkv)
        logf, ccol, crow = _foxprep(misc, bias_rows[l], bp, t)
        o_fox = _fox_attn(fq, fkvb, ccol, crow, bp, t)
        o_nsa = _nsa_attn(nq, kcm, nkvb, winb, misc, bp, t)
        y1 = _outproj(yp, o_fox, o_nsa, u, w_out_p[l], w_pool_bd[l], pool_scale[l].reshape(1, C_POOL), g[1],
                      modp[2], tm=512, t=t)
        yp = _mlp(y1, g[2], modp[4], modp[3], w1[l], w2[l], g[3], modp[5], tm=512, rows_per_mod=t)
        wl = min(WINDOW, t)
        sp.append((logf.reshape(bp, t, H_FOX),
                   win.reshape(bp, t, 2, KV_NSA, HEAD_DIM)[:, t - wl:],
                   u.reshape(bp, t, C_POOL)[:, t - POOL_STATE:]))

        (fq_s, nq_s, fkv_s, nkv_s, win_s, u_s, misc_s, _, _, _, _) = _inproj(
            ys, g[0], mods[1], mods[0], w_in_p[l], *rope_s, tm=bs, rows_per_mod=bs, rope_rows=bs, with_means=False)
        r3 = lambda a: a.reshape(bs, 1, a.shape[-1])
        fq6 = fq_s.astype(f32).reshape(bs, H_FOX, HEAD_DIM)
        nq6 = nq_s.astype(f32).reshape(bs, H_NSA, HEAD_DIM)[:, nat_from_pair]
        nkv4 = nkv_s.reshape(bs, 4, KV_NSA, HEAD_DIM)
        win2 = win_s.reshape(bs, 2, KV_NSA, HEAD_DIM)
        o_fox_s, logf_s = _fox_decode(l, page_table, fq6, r3(misc_s), bias_rows[l],
                                      fkv_s.reshape(bs, 2, H_FOX, HEAD_DIM), lf_t, fox_t)
        p8, o_cmp = _nsa_cmp_decode(l, page_table, nq6, nsa_t, past)
        idx = _nsa_topk_decode(p8, past)[:, :min(TOPK_BLOCKS, -(-(past + 1) // SEL_BLOCK))]
        o_nsa_s = _nsa_sel_decode(l, page_table, idx, nq6, r3(misc_s), o_cmp, nkv4, win2,
                                  win_t, nsa_t, past)
        o_fox_s = o_fox_s[:, :H_FOX].reshape(bs, 384).astype(bf16)
        o_nsa_s = o_nsa_s[:, np.asarray(NSA_PAIR_ORDER)].reshape(bs, 384).astype(bf16)
        u_ext = jnp.concatenate([state_pool[l], u_s.reshape(bs, 1, C_POOL)], axis=1)
        y1s = _outproj_dec(ys, o_fox_s, o_nsa_s, u_ext, w_out_p[l],
                           w_pool_bd[l], pool_scale[l].reshape(1, C_POOL), g[1], mods[2][0], past)
        ys = _mlp(y1s, g[2], mods[4], mods[3], w1[l], w2[l], g[3], mods[5], tm=bs, rows_per_mod=bs)
        win_all = jnp.concatenate([cache_nsa_win[l], win2[:, None]], axis=1)
        ss.append((fkv_s.reshape(bs, 1, 2, H_FOX, HEAD_DIM), logf_s[:, :, :H_FOX], nkv4[:, None],
                   win_all[:, 1:], u_ext[:, 1:]))

    stk = lambda lst, i: jnp.stack([s[i] for s in lst], axis=0)
    tokens_first = (0, 1, 5, 2, 3, 4)
    fkv_all = jnp.transpose(kv_stacks[0].reshape(depth, bp, 2, H_FOX, HEAD_DIM, t), tokens_first)
    nkv_all = jnp.transpose(kv_stacks[1].reshape(depth, bp, 4, KV_NSA, HEAD_DIM, t), tokens_first)
    return (yp.reshape(bp, t, D_MODEL), ys.reshape(bs, 1, D_MODEL),
            fkv_all, stk(sp, 0), nkv_all, stk(sp, 1), stk(sp, 2),
            stk(ss, 0), stk(ss, 1), stk(ss, 2), stk(ss, 3), stk(ss, 4))
```
